```python
import math
import jax, jax.numpy as jnp
from jax import lax
import numpy as np

D_MODEL = 2048
BATCH = 4
SEQ = 4096
DEPTH = 4
DEC_BATCH = 4
DEC_SEQ = 2048
PAST_LEN = 128

H_A = D_MODEL // 256
DK_A = 128
DV_A = 128
QK_A = H_A * DK_A
V_A = H_A * DV_A
CONV_K = 3
CHUNK = 64
H_B = D_MODEL // 256
DH_B = 64
DV_B = 2 * DH_B
QK_B = H_B * 2 * DH_B
V_B = H_B * DV_B
N_BUCKETS = 32
MAX_DIST = 128
H_C = D_MODEL // 128
Q_LORA = D_MODEL // 4
KV_LORA = D_MODEL // 4
NOPE_C = 128
ROPE_C = 64
DV_C = 128
ROPE_THETA = 10000.0
D_FF = 4 * D_MODEL
EPS = 1e-6
Q_BLOCK = 128
N_EVEN = (DEPTH + 1) // 2
N_ODD = DEPTH // 2
E_IN = 2 * QK_A + 2 * V_A + 4 * H_A + 2 * QK_B + V_B
E_OUT = V_A + V_B
C_IN = Q_LORA + KV_LORA + ROPE_C

kernel_name = "hybrid_gdn_diffattn_mla_encoder"


def rmsnorm(x, g):
    xf = x.astype(jnp.float32)
    y = xf * lax.rsqrt(jnp.mean(xf * xf, axis=-1, keepdims=True) + EPS)
    return (y * g.astype(jnp.float32)).astype(x.dtype)


def l2norm(x):
    return x * lax.rsqrt(jnp.sum(x * x, axis=-1, keepdims=True) + EPS)


def centred_dwconv(x, w):
    c = x.shape[-1]
    pad = (CONV_K - 1) // 2
    return lax.conv_general_dilated(x, w[:, None, :].astype(x.dtype), window_strides=(1,),
                                    padding=[(pad, pad)], dimension_numbers=("NWC", "WIO", "NWC"),
                                    feature_group_count=c)


def gated_delta_chunked(q, k, v, g, beta):
    b, h, s, dk = q.shape
    dv = v.shape[-1]
    n = s // CHUNK
    rs = lambda t: t.reshape(b, h, n, CHUNK, *t.shape[3:])
    q, k, v, g, beta = rs(q), rs(k), rs(v), rs(g), rs(beta)
    g = jnp.cumsum(g, axis=-1)
    idx = jnp.arange(CHUNK)
    incl = idx[:, None] >= idx[None, :]
    strict = idx[:, None] > idx[None, :]
    decay = jnp.exp(jnp.where(incl, g[..., :, None] - g[..., None, :], -jnp.inf))
    k_beta = k * beta[..., None]
    v_beta = v * beta[..., None]
    lower = jnp.where(strict, jnp.einsum("bhncd,bhnmd->bhncm", k_beta, k) * decay, 0.0)
    eye = jnp.eye(CHUNK, dtype=q.dtype)
    t_mat = lax.linalg.triangular_solve(eye + lower, jnp.broadcast_to(eye, lower.shape),
                                        left_side=True, lower=True, unit_diagonal=True)
    u = jnp.einsum("bhncm,bhnme->bhnce", t_mat, v_beta)
    w = jnp.einsum("bhncm,bhnmd->bhncd", t_mat, k_beta * jnp.exp(g)[..., None])
    intra = jnp.einsum("bhncd,bhnmd->bhncm", q, k) * decay
    g_last = g[..., -1]
    q_dec = q * jnp.exp(g)[..., None]
    k_dec = k * jnp.exp(g_last[..., None] - g)[..., None]

    def step(state, inp):
        q_c, k_c, u_c, w_c, intra_c, gl_c = inp
        v_new = u_c - jnp.einsum("bhcd,bhde->bhce", w_c, state)
        o_c = jnp.einsum("bhcd,bhde->bhce", q_c, state) + jnp.einsum("bhcm,bhme->bhce", intra_c, v_new)
        state = state * jnp.exp(gl_c)[..., None, None] + jnp.einsum("bhcd,bhce->bhde", k_c, v_new)
        return state, o_c

    xs = tuple(jnp.moveaxis(t, 2, 0) for t in (q_dec, k_dec, u, w, intra, g_last))
    state0 = jnp.zeros((b, h, dk, dv), q.dtype)
    _, o = lax.scan(step, state0, xs)
    return jnp.moveaxis(o, 0, 2).reshape(b, h, s, dv)


def gated_deltanet(qkv, z, b_gate, a_gate, conv_w, a_log, dt_bias, norm_g):
    bsz, s, _ = qkv.shape
    f32 = jnp.float32
    qkv = jax.nn.silu(centred_dwconv(qkv, conv_w).astype(f32))
    q, k, v = jnp.split(qkv, [QK_A, 2 * QK_A], axis=-1)
    to_heads = lambda t, d: t.reshape(bsz, s, H_A, d).transpose(0, 2, 1, 3)
    q = l2norm(to_heads(q, DK_A)) * (DK_A ** -0.5)
    k = l2norm(to_heads(k, DK_A))
    v = to_heads(v, DV_A)
    beta = jax.nn.sigmoid(b_gate.astype(f32)).transpose(2, 0, 3, 1)
    g = (-jnp.exp(a_log.astype(f32)) * jax.nn.softplus(a_gate.astype(f32) + dt_bias.astype(f32)))
    g = g.transpose(2, 0, 3, 1)
    flip = lambda t: jnp.flip(t, axis=2)
    o_f = gated_delta_chunked(q, k, v, g[0], beta[0])
    o_b = flip(gated_delta_chunked(flip(q), flip(k), flip(v), flip(g[1]), flip(beta[1])))
    o = (o_f + o_b).transpose(0, 2, 1, 3)
    o = rmsnorm(o, norm_g) * jax.nn.silu(z.astype(f32)).reshape(bsz, s, H_A, DV_A)
    return o.reshape(bsz, s, V_A)


def t5_bucket(rel):
    nb = N_BUCKETS // 2
    max_exact = nb // 2
    ret = jnp.where(rel > 0, nb, 0)
    n = jnp.abs(rel)
    nf = jnp.maximum(n, 1).astype(jnp.float32)
    large = max_exact + (jnp.log(nf / max_exact) / math.log(MAX_DIST / max_exact)
                         * (nb - max_exact)).astype(jnp.int32)
    large = jnp.minimum(large, nb - 1)
    return ret + jnp.where(n < max_exact, n, large)


def diff_attention(q, k, v, rel_bias, lam, lam_init, subln_g):
    bsz, s = q.shape[:2]
    nblk = s // Q_BLOCK
    qb = jnp.moveaxis(q.reshape(bsz, nblk, Q_BLOCK, H_B, 2, DH_B), 1, 0)
    kpos = jnp.arange(s)
    scale = DH_B ** -0.5

    def block(args):
        q_blk, start = args
        qpos = start + jnp.arange(Q_BLOCK)
        bias = rel_bias[t5_bucket(kpos[None, :] - qpos[:, None])]
        bias = jnp.transpose(bias, (2, 0, 1)).astype(jnp.float32)
        logits = jnp.einsum("bqhtd,bkhtd->bhtqk", q_blk, k).astype(jnp.float32) * scale
        p = jax.nn.softmax(logits + bias[None, :, None], axis=-1)
        attn = p[:, :, 0] - lam * p[:, :, 1]
        return jnp.einsum("bhqk,bkhd->bqhd", attn.astype(v.dtype), v)

    o = lax.map(block, (qb, jnp.arange(nblk) * Q_BLOCK))
    o = jnp.moveaxis(o, 0, 1).reshape(bsz, s, H_B, DV_B)
    o = rmsnorm(o, subln_g) * (1.0 - lam_init)
    return o.reshape(bsz, s, V_B)


def rope_tables(s):
    inv_freq = ROPE_THETA ** (-jnp.arange(0, ROPE_C, 2, dtype=jnp.float32) / ROPE_C)
    ang = jnp.arange(s, dtype=jnp.float32)[:, None] * inv_freq[None, :]
    return jnp.cos(ang), jnp.sin(ang)


def apply_rope(x, cos, sin):
    xf = x.astype(jnp.float32)
    x1, x2 = jnp.split(xf, 2, axis=-1)
    return jnp.concatenate([x1 * cos - x2 * sin, x2 * cos + x1 * sin], axis=-1).astype(x.dtype)


def mla(h, w_in, q_norm, w_qb, kv_norm, w_kvb):
    bsz, s, _ = h.shape
    c = h @ w_in
    cq, ckv, k_rope = jnp.split(c, [Q_LORA, Q_LORA + KV_LORA], axis=-1)
    q = (rmsnorm(cq, q_norm) @ w_qb).reshape(bsz, s, H_C, NOPE_C + ROPE_C)
    kv = (rmsnorm(ckv, kv_norm) @ w_kvb).reshape(bsz, s, H_C, NOPE_C + DV_C)
    q_nope, q_rope = jnp.split(q, [NOPE_C], axis=-1)
    k_nope, v = jnp.split(kv, [NOPE_C], axis=-1)
    cos, sin = rope_tables(s)
    q_rope = apply_rope(q_rope, cos[:, None, :], sin[:, None, :])
    k_rope = apply_rope(k_rope, cos, sin)
    scale = (NOPE_C + ROPE_C) ** -0.5
    nblk = s // Q_BLOCK
    blocks = lambda t: jnp.moveaxis(t.reshape(bsz, nblk, Q_BLOCK, *t.shape[2:]), 1, 0)

    def block(args):
        qn, qr = args
        logits = (jnp.einsum("bqhd,bkhd->bhqk", qn, k_nope)
                  + jnp.einsum("bqhd,bkd->bhqk", qr, k_rope)).astype(jnp.float32) * scale
        p = jax.nn.softmax(logits, axis=-1)
        return jnp.einsum("bhqk,bkhd->bqhd", p.astype(v.dtype), v)

    o = lax.map(block, (blocks(q_nope), blocks(q_rope)))
    return jnp.moveaxis(o, 0, 1).reshape(bsz, s, H_C * DV_C)


def even_mixer(h, layer, w_in, conv_w, a_log, dt_bias, gdn_norm, diff_lambda, subln, w_out, rel_bias):
    bsz, s, _ = h.shape
    o1 = 2 * QK_A + V_A
    o2 = o1 + V_A
    o3 = o2 + 4 * H_A
    o4 = o3 + QK_B
    o5 = o4 + QK_B
    p = h @ w_in
    qkv_a, z_a, gates, q_b, k_b, v_b = jnp.split(p, [o1, o2, o3, o4, o5], axis=-1)
    gates = gates.reshape(bsz, s, 2, 2, H_A)
    o_a = gated_deltanet(qkv_a, z_a, gates[:, :, 0], gates[:, :, 1], conv_w, a_log, dt_bias, gdn_norm)
    lam_init = 0.8 - 0.6 * math.exp(-0.3 * layer)
    lv = diff_lambda.astype(jnp.float32)
    lam = jnp.exp(jnp.sum(lv[0] * lv[1])) - jnp.exp(jnp.sum(lv[2] * lv[3])) + lam_init
    o_b = diff_attention(q_b.reshape(bsz, s, H_B, 2, DH_B), k_b.reshape(bsz, s, H_B, 2, DH_B),
                         v_b.reshape(bsz, s, H_B, DV_B), rel_bias, lam, lam_init, subln)
    return jnp.concatenate([o_a.astype(h.dtype), o_b.astype(h.dtype)], axis=-1) @ w_out


def trunk(x, norm_mix, norm_mlp, norm_final, rel_bias, w_in_e, conv_w, a_log, dt_bias, gdn_norm,
          diff_lambda, subln, w_out_e, w_in_c, q_norm, w_qb, kv_norm, w_kvb, w_out_c, w_mlp1, w_mlp2):
    for layer in range(DEPTH):
        h = rmsnorm(x, norm_mix[layer])
        i = layer // 2
        if layer % 2 == 0:
            x = x + even_mixer(h, layer, w_in_e[i], conv_w[i], a_log[i], dt_bias[i], gdn_norm[i],
                               diff_lambda[i], subln[i], w_out_e[i], rel_bias)
        else:
            x = x + mla(h, w_in_c[i], q_norm[i], w_qb[i], kv_norm[i], w_kvb[i]) @ w_out_c[i]
        h = rmsnorm(x, norm_mlp[layer])
        x = x + jnp.square(jax.nn.relu(h @ w_mlp1[layer])) @ w_mlp2[layer]
    return rmsnorm(x, norm_final)


def setup_inputs(seed: int = 0) -> dict:
    key = jax.random.key(seed)
    ks = jax.random.split(key, 22)
    f32 = jnp.float32
    nrm = lambda k, shape, scale: jax.random.normal(k, shape, f32) * scale
    gain = lambda k, shape: 1.0 + 0.05 * jax.random.normal(k, shape, f32)
    dt = jnp.exp(jax.random.uniform(ks[9], (N_EVEN, 2, H_A), f32, math.log(1e-3), math.log(1e-1)))
    return {
        "x_prompt": nrm(ks[0], (BATCH, SEQ, D_MODEL), 1.0),
        "x_sample": nrm(ks[1], (DEC_BATCH, DEC_SEQ, D_MODEL), 1.0),
        "norm_mix": gain(ks[2], (DEPTH, D_MODEL)),
        "norm_mlp": gain(ks[3], (DEPTH, D_MODEL)),
        "norm_final": gain(ks[4], (D_MODEL,)),
        "rel_bias": nrm(ks[5], (N_BUCKETS, H_B), 0.2),
        "w_in_e": nrm(ks[6], (N_EVEN, D_MODEL, E_IN), D_MODEL ** -0.5),
        "conv_w": nrm(ks[7], (N_EVEN, CONV_K, 2 * QK_A + V_A), CONV_K ** -0.5),
        "a_log": jnp.log(jax.random.uniform(ks[8], (N_EVEN, 2, H_A), f32, 1.0, 16.0)),
        "dt_bias": dt + jnp.log(-jnp.expm1(-dt)),
        "gdn_norm": gain(ks[10], (N_EVEN, DV_A)),
        "diff_lambda": nrm(ks[11], (N_EVEN, 4, DH_B), 0.1),
        "subln": gain(ks[12], (N_EVEN, DV_B)),
        "w_out_e": nrm(ks[13], (N_EVEN, E_OUT, D_MODEL), E_OUT ** -0.5),
        "w_in_c": nrm(ks[14], (N_ODD, D_MODEL, C_IN), D_MODEL ** -0.5),
        "q_norm": gain(ks[15], (N_ODD, Q_LORA)),
        "w_qb": nrm(ks[16], (N_ODD, Q_LORA, H_C * (NOPE_C + ROPE_C)), Q_LORA ** -0.5),
        "kv_norm": gain(ks[17], (N_ODD, KV_LORA)),
        "w_kvb": nrm(ks[18], (N_ODD, KV_LORA, H_C * (NOPE_C + DV_C)), KV_LORA ** -0.5),
        "w_out_c": nrm(ks[19], (N_ODD, H_C * DV_C, D_MODEL), (H_C * DV_C) ** -0.5),
        "w_mlp1": nrm(ks[20], (DEPTH, D_MODEL, D_FF), D_MODEL ** -0.5),
        "w_mlp2": nrm(ks[21], (DEPTH, D_FF, D_MODEL), D_FF ** -0.5),
    }


def reference(x_prompt, x_sample, norm_mix, norm_mlp, norm_final, rel_bias, w_in_e, conv_w, a_log, dt_bias,
              gdn_norm, diff_lambda, subln, w_out_e, w_in_c, q_norm, w_qb, kv_norm, w_kvb, w_out_c,
              w_mlp1, w_mlp2):
    y_prompt = trunk(x_prompt, norm_mix, norm_mlp, norm_final, rel_bias, w_in_e, conv_w, a_log, dt_bias,
                     gdn_norm, diff_lambda, subln, w_out_e, w_in_c, q_norm, w_qb, kv_norm, w_kvb, w_out_c,
                     w_mlp1, w_mlp2)
    y_sample = trunk(x_sample, norm_mix, norm_mlp, norm_final, rel_bias, w_in_e, conv_w, a_log, dt_bias,
                     gdn_norm, diff_lambda, subln, w_out_e, w_in_c, q_norm, w_qb, kv_norm, w_kvb, w_out_c,
                     w_mlp1, w_mlp2)
    return (y_prompt, y_sample)
```

```python
import functools
import math

import jax
import jax.numpy as jnp
from jax import lax
from jax.experimental import pallas as pl
from jax.experimental.pallas import tpu as pltpu

F32 = jnp.float32
BF16 = jnp.bfloat16

D_MODEL = 2048
DEPTH = 4
H_A = 8
DK_A = 128
DV_A = 128
QK_A = H_A * DK_A
V_A = H_A * DV_A
H_B = 8
DH_B = 64
DV_B = 128
QK_B = H_B * 2 * DH_B
V_B = H_B * DV_B
N_BUCKETS = 32
MAX_DIST = 128
H_C = 16
Q_LORA = 512
KV_LORA = 512
NOPE_C = 128
ROPE_C = 64
DV_C = 128
ROPE_THETA = 10000.0
D_FF = 4 * D_MODEL
EPS = 1e-6

LANES = 128
BF16_SUBLANES = 16
VMEM_LIMIT = 56 * 1024 * 1024

GDN_CHUNK = 256
ATT_TILE = 256
N_MAIN_E = 2 * QK_A + 2 * V_A + 2 * QK_B + V_B


def _params(sem):
    return pltpu.CompilerParams(dimension_semantics=sem, vmem_limit_bytes=VMEM_LIMIT)


def _dot(a, b):
    return jnp.dot(a, b, preferred_element_type=F32)


def _dot_nt(a, b):
    return lax.dot_general(a, b, (((1,), (1,)), ((), ())), preferred_element_type=F32)


def _rms(x, g):
    ms = jnp.mean(x * x, axis=-1, keepdims=True)
    return x * lax.rsqrt(ms + EPS) * g


def _norm_matmul_kernel(x_ref, g_ref, w_ref, o_ref, h_ref):
    @pl.when(pl.program_id(1) == 0)
    def _():
        h_ref[...] = _rms(x_ref[...].astype(F32), g_ref[...]).astype(BF16)

    o_ref[...] = _dot(h_ref[...], w_ref[...]).astype(o_ref.dtype)


def norm_matmul(x, kblock, k, g, w, out_dtype, tm, tn):
    t = x.shape[0]
    n = w.shape[1]
    tm = min(tm, t)
    tn = min(tn, n)
    return pl.pallas_call(
        _norm_matmul_kernel,
        grid=(t // tm, n // tn),
        in_specs=[
            pl.BlockSpec((tm, k), lambda i, j: (i, kblock)),
            pl.BlockSpec((1, k), lambda i, j: (0, 0)),
            pl.BlockSpec((k, tn), lambda i, j: (0, j)),
        ],
        out_specs=pl.BlockSpec((tm, tn), lambda i, j: (i, j)),
        out_shape=jax.ShapeDtypeStruct((t, n), out_dtype),
        scratch_shapes=[pltpu.VMEM((tm, k), BF16)],
        compiler_params=_params(("parallel", "arbitrary")),
        name="norm_matmul",
    )(x, g.reshape(1, k).astype(F32), w)


def _matmul_res_kernel(*refs, widths):
    n = len(widths)
    w_ref, r_ref, o_ref = refs[n:]
    acc = r_ref[...]
    off = 0
    for a_ref, wd in zip(refs[:n], widths):
        acc = acc + _dot(a_ref[...], w_ref[off:off + wd, :])
        off += wd
    o_ref[...] = acc


def matmul_res(parts, w, res, tm, tn):
    t = res.shape[0]
    n = w.shape[1]
    tm = min(tm, t)
    widths = tuple(p.shape[1] for p in parts)
    return pl.pallas_call(
        functools.partial(_matmul_res_kernel, widths=widths),
        grid=(t // tm, n // tn),
        in_specs=[pl.BlockSpec((tm, wd), lambda i, j: (i, 0)) for wd in widths]
        + [
            pl.BlockSpec((sum(widths), tn), lambda i, j: (0, j)),
            pl.BlockSpec((tm, tn), lambda i, j: (i, j)),
        ],
        out_specs=pl.BlockSpec((tm, tn), lambda i, j: (i, j)),
        out_shape=jax.ShapeDtypeStruct((t, n), F32),
        compiler_params=_params(("parallel", "parallel")),
        name="matmul_res",
    )(*parts, w, res)


def _mlp_kernel(x_ref, g_ref, w1_ref, w2_ref, gf_ref, o_ref, h_ref, *, nf, final_norm):
    f = pl.program_id(1)

    @pl.when(f == 0)
    def _():
        x = x_ref[...]
        h_ref[...] = _rms(x, g_ref[...]).astype(BF16)
        o_ref[...] = x

    a = _dot(h_ref[...], w1_ref[...])
    a = jnp.square(jnp.maximum(a, 0.0)).astype(BF16)
    o_ref[...] += _dot(a, w2_ref[...])

    if final_norm:
        @pl.when(f == nf - 1)
        def _():
            o_ref[...] = _rms(o_ref[...], gf_ref[...])


def mlp(x, g, w1, w2, g_final, final_norm, tm, tf):
    t, d = x.shape
    tm = min(tm, t)
    nf = D_FF // tf
    return pl.pallas_call(
        functools.partial(_mlp_kernel, nf=nf, final_norm=final_norm),
        grid=(t // tm, nf),
        in_specs=[
            pl.BlockSpec((tm, d), lambda i, f: (i, 0)),
            pl.BlockSpec((1, d), lambda i, f: (0, 0)),
            pl.BlockSpec((d, tf), lambda i, f: (0, f)),
            pl.BlockSpec((tf, d), lambda i, f: (f, 0)),
            pl.BlockSpec((1, d), lambda i, f: (0, 0)),
        ],
        out_specs=pl.BlockSpec((tm, d), lambda i, f: (i, 0)),
        out_shape=jax.ShapeDtypeStruct((t, d), F32),
        scratch_shapes=[pltpu.VMEM((tm, d), BF16)],
        compiler_params=_params(("parallel", "arbitrary")),
        name="mlp",
    )(x, g.reshape(1, d), w1, w2, g_final.reshape(1, d))


def _softplus(x):
    return jnp.maximum(x, 0.0) + jnp.log1p(jnp.exp(-jnp.abs(x)))


def _gdn_prep_kernel(q_ref, qp_ref, qn_ref, k_ref, kp_ref, kn_ref, v_ref, vp_ref, vn_ref,
                     cwq_ref, cwk_ref, cwv_ref, gt_ref, alog_ref, dtb_ref,
                     qd_ref, kdt_ref, w_ref, u_ref, in_ref, egl_ref, *, cc, n):
    h = pl.program_id(1)
    i = pl.program_id(2)
    row = lax.broadcasted_iota(jnp.int32, (cc, LANES), 0)
    lane = lax.broadcasted_iota(jnp.int32, (cc, LANES), 1)

    def conv_silu(x_ref, p_ref, n_ref, cw_ref):
        x = x_ref[...].astype(F32)
        prev = jnp.where(i == 0, 0.0, p_ref[BF16_SUBLANES - 1:BF16_SUBLANES, :].astype(F32))
        nxt = jnp.where(i == n - 1, 0.0, n_ref[0:1, :].astype(F32))
        xm = jnp.where(row == 0, prev, pltpu.roll(x, 1, axis=0))
        xp = jnp.where(row == cc - 1, nxt, pltpu.roll(x, cc - 1, axis=0))
        cw = cw_ref[...]
        y = xm * cw[0:1, :] + x * cw[1:2, :] + xp * cw[2:3, :]
        return y * jax.nn.sigmoid(y)

    q = conv_silu(q_ref, qp_ref, qn_ref, cwq_ref)
    k = conv_silu(k_ref, kp_ref, kn_ref, cwk_ref)
    v = conv_silu(v_ref, vp_ref, vn_ref, cwv_ref)
    q = q * lax.rsqrt(jnp.sum(q * q, axis=-1, keepdims=True) + EPS) * (DK_A ** -0.5)
    k = k * lax.rsqrt(jnp.sum(k * k, axis=-1, keepdims=True) + EPS)
    k16 = k.astype(BF16)
    qk = _dot_nt(q.astype(BF16), k16)
    kk = _dot_nt(k16, k16)

    gt = gt_ref[...]
    beta_all = jax.nn.sigmoid(gt)
    g_all = -jnp.exp(alog_ref[...]) * _softplus(gt + dtb_ref[...])
    pre = g_all
    s = 1
    while s < cc:
        pre = pre + jnp.where(row >= s, pltpu.roll(pre, s, axis=0), 0.0)
        s *= 2
    tot = pre[cc - 1:cc, :]
    suf = tot - pre + g_all
    pre_t = pre.T
    suf_t = suf.T
    sub_t = lax.broadcasted_iota(jnp.int32, (LANES, cc), 0)
    lane1 = lax.broadcasted_iota(jnp.int32, (1, LANES), 1)

    def col(x, j):
        return jnp.sum(jnp.where(lane == j, x, 0.0), axis=1, keepdims=True)

    def rowv(xt, j):
        return jnp.sum(jnp.where(sub_t == j, xt, 0.0), axis=0, keepdims=True)

    ii = lax.broadcasted_iota(jnp.int32, (cc, cc), 0)
    jj = lax.broadcasted_iota(jnp.int32, (cc, cc), 1)
    eye = (ii == jj).astype(F32)

    for d in range(2):
        jb = d * H_A + h
        jg = 2 * H_A + d * H_A + h
        if d == 0:
            incl, strict = ii >= jj, ii > jj
            gcol, grow = col(pre, jg), rowv(pre_t, jg)
        else:
            incl, strict = ii <= jj, ii < jj
            gcol, grow = col(suf, jg), rowv(suf_t, jg)
        beta = col(beta_all, jb)
        glast = jnp.sum(jnp.where(lane1 == jg, tot, 0.0), axis=1, keepdims=True)
        dec = jnp.where(incl, jnp.exp(jnp.where(incl, gcol - grow, 0.0)), 0.0)
        kb = k * beta
        vb = v * beta
        a = jnp.where(strict, kk * dec, 0.0) * beta
        x = eye - jnp.where((ii >> 1) == (jj >> 1), a, 0.0)
        lg = 1
        while (1 << lg) < cc:
            off = ((ii >> (lg + 1)) == (jj >> (lg + 1))) & ((ii >> lg) != (jj >> lg))
            x16 = x.astype(BF16)
            y = _dot(jnp.where(off, a, 0.0).astype(BF16), x16)
            x = x - _dot(x16, y.astype(BF16))
            lg += 1
        x16 = x.astype(BF16)
        eg = jnp.exp(gcol)
        u_ref[d] = _dot(x16, vb.astype(BF16))
        w_ref[d] = _dot(x16, (kb * eg).astype(BF16)).astype(BF16)
        in_ref[d] = (qk * dec).astype(BF16)
        qd_ref[d] = (q * eg).astype(BF16)
        kdt_ref[d] = (k * jnp.exp(glast - gcol)).T.astype(BF16)
        egl_ref[d] = jnp.broadcast_to(jnp.exp(glast), (8, LANES))


def gdn_prep(p, gates, conv_w, alog_row, dtb_row, bsz, s):
    cc = GDN_CHUNK
    n = s // cc
    t = bsz * s
    rpb = cc // BF16_SUBLANES
    last_blk = t // BF16_SUBLANES - 1

    def main(cb):
        return pl.BlockSpec((cc, LANES), lambda b, h, i: (b * n + i, cb + h))

    def prev(cb):
        return pl.BlockSpec((BF16_SUBLANES, LANES),
                            lambda b, h, i: (jnp.maximum((b * n + i) * rpb - 1, 0), cb + h))

    def nxt(cb):
        return pl.BlockSpec((BF16_SUBLANES, LANES),
                            lambda b, h, i: (jnp.minimum((b * n + i + 1) * rpb, last_blk), cb + h))

    def cw(cb):
        return pl.BlockSpec((3, LANES), lambda b, h, i: (0, cb + h))

    row_spec = pl.BlockSpec((1, LANES), lambda b, h, i: (0, 0))
    seq_out = lambda last: pl.BlockSpec((2, None, None, cc, last), lambda b, h, i: (0, b, h, i, 0))
    out_shape = [
        jax.ShapeDtypeStruct((2, bsz, H_A, s, DK_A), BF16),
        jax.ShapeDtypeStruct((2, bsz, H_A, DK_A, s), BF16),
        jax.ShapeDtypeStruct((2, bsz, H_A, s, DK_A), BF16),
        jax.ShapeDtypeStruct((2, bsz, H_A, s, DV_A), F32),
        jax.ShapeDtypeStruct((2, bsz, H_A, s, cc), BF16),
        jax.ShapeDtypeStruct((2, bsz, H_A, n, 8, LANES), F32),
    ]
    out_specs = [
        seq_out(DK_A),
        pl.BlockSpec((2, None, None, DK_A, cc), lambda b, h, i: (0, b, h, 0, i)),
        seq_out(DK_A),
        seq_out(DV_A),
        seq_out(cc),
        pl.BlockSpec((2, None, None, None, 8, LANES), lambda b, h, i: (0, b, h, i, 0, 0)),
    ]
    kb0, vb0 = QK_A // LANES, 2 * QK_A // LANES
    return pl.pallas_call(
        functools.partial(_gdn_prep_kernel, cc=cc, n=n),
        grid=(bsz, H_A, n),
        in_specs=[main(0), prev(0), nxt(0), main(kb0), prev(kb0), nxt(kb0), main(vb0), prev(vb0), nxt(vb0),
                  cw(0), cw(kb0), cw(vb0),
                  pl.BlockSpec((cc, LANES), lambda b, h, i: (b * n + i, 0)),
                  row_spec, row_spec],
        out_specs=out_specs,
        out_shape=out_shape,
        compiler_params=_params(("parallel", "parallel", "parallel")),
        name="gdn_prep",
    )(p, p, p, p, p, p, p, p, p, conv_w, conv_w, conv_w, gates, alog_row, dtb_row)


def _gdn_scan_kernel(qdf, kdf, wf, uf, inf, eglf, qdb, kdb, wb, ub, inb, eglb, z_ref, gn_ref,
                     o_ref, st_ref, acc_ref, *, cc, n):
    i = pl.program_id(2)

    @pl.when(i == 0)
    def _():
        st_ref[...] = jnp.zeros_like(st_ref)
        acc_ref[...] = jnp.zeros_like(acc_ref)

    def step(d, qd, kdt, w, u, intra, egl, chunk):
        st = st_ref[d]
        st16 = st.astype(BF16)
        v_new = u[...] - _dot(w[...], st16)
        v16 = v_new.astype(BF16)
        o = _dot(qd[...], st16) + _dot(intra[...], v16)
        st_ref[d] = st * egl[0:1, :] + _dot(kdt[...], v16)
        r0 = pl.multiple_of(chunk * cc, cc)
        acc_ref[pl.ds(r0, cc), :] += o

    step(0, qdf, kdf, wf, uf, inf, eglf, i)
    step(1, qdb, kdb, wb, ub, inb, eglb, n - 1 - i)

    @pl.when(i == n - 1)
    def _():
        z = z_ref[...].astype(F32)
        y = _rms(acc_ref[...], gn_ref[...]) * (z * jax.nn.sigmoid(z))
        o_ref[...] = y.astype(o_ref.dtype)


def gdn_scan(prep, p, gdn_norm, bsz, s):
    qd, kdt, w, u, intra, egl = prep
    cc = GDN_CHUNK
    n = s // cc

    def seq(d, last):
        if d == 0:
            return pl.BlockSpec((None, None, None, cc, last), lambda b, h, i: (0, b, h, i, 0))
        return pl.BlockSpec((None, None, None, cc, last), lambda b, h, i: (1, b, h, n - 1 - i, 0))

    def kdt_spec(d):
        if d == 0:
            return pl.BlockSpec((None, None, None, DK_A, cc), lambda b, h, i: (0, b, h, 0, i))
        return pl.BlockSpec((None, None, None, DK_A, cc), lambda b, h, i: (1, b, h, 0, n - 1 - i))

    def egl_spec(d):
        if d == 0:
            return pl.BlockSpec((None, None, None, None, 8, LANES), lambda b, h, i: (0, b, h, i, 0, 0))
        return pl.BlockSpec((None, None, None, None, 8, LANES), lambda b, h, i: (1, b, h, n - 1 - i, 0, 0))

    def dir_specs(d):
        return [seq(d, DK_A), kdt_spec(d), seq(d, DK_A), seq(d, DV_A), seq(d, cc), egl_spec(d)]

    zb0 = (2 * QK_A + V_A) // LANES
    return pl.pallas_call(
        functools.partial(_gdn_scan_kernel, cc=cc, n=n),
        grid=(bsz, H_A, n),
        in_specs=dir_specs(0) + dir_specs(1) + [
            pl.BlockSpec((s, LANES), lambda b, h, i: (b, zb0 + h)),
            pl.BlockSpec((1, DV_A), lambda b, h, i: (0, 0)),
        ],
        out_specs=pl.BlockSpec((s, DV_A), lambda b, h, i: (b, h)),
        out_shape=jax.ShapeDtypeStruct((bsz * s, V_A), BF16),
        scratch_shapes=[pltpu.VMEM((2, DK_A, DV_A), F32), pltpu.VMEM((s, DV_A), F32)],
        compiler_params=_params(("parallel", "parallel", "arbitrary")),
        name="gdn_scan",
    )(qd, kdt, w, u, intra, egl, qd, kdt, w, u, intra, egl, p, gdn_norm.reshape(1, DV_A))


def _bucket_thresholds():
    nb = N_BUCKETS // 2
    max_exact = nb // 2
    out = []
    prev = 0
    for dist in range(max_exact, MAX_DIST):
        val = int(math.log(dist / max_exact) / math.log(MAX_DIST / max_exact) * (nb - max_exact))
        val = min(val, nb - 1 - max_exact)
        if val > prev:
            out.extend([dist] * (val - prev))
            prev = val
    return tuple(out)


def _bias_kernel(rb_ref, o_ref, *, t, thresholds):
    d = pl.program_id(0) - 2
    h = pl.program_id(1)
    nb = N_BUCKETS // 2
    max_exact = nb // 2
    r = lax.broadcasted_iota(jnp.int32, (t, t), 0)
    c = lax.broadcasted_iota(jnp.int32, (t, t), 1)
    rel = d * t + c - r
    dist = jnp.abs(rel)
    large = jnp.full((t, t), max_exact, jnp.int32)
    for th in thresholds:
        large = large + (dist >= th).astype(jnp.int32)
    bucket = jnp.where(rel > 0, nb, 0) + jnp.where(dist < max_exact, dist, large)
    out = jnp.zeros((t, t), F32)
    for b in range(N_BUCKETS):
        out = jnp.where(bucket == b, rb_ref[b, h], out)
    o_ref[...] = out


def bias_tiles(rel_bias):
    t = ATT_TILE
    return pl.pallas_call(
        functools.partial(_bias_kernel, t=t, thresholds=_bucket_thresholds()),
        grid=(5, H_B),
        in_specs=[pl.BlockSpec(memory_space=pltpu.SMEM)],
        out_specs=pl.BlockSpec((None, None, t, t), lambda d, h: (d, h, 0, 0)),
        out_shape=jax.ShapeDtypeStruct((5, H_B, t, t), F32),
        compiler_params=_params(("parallel", "parallel")),
        name="t5_bias_tiles",
    )(rel_bias)


def _diff_attn_kernel(q_ref, k_ref, v_ref, b_ref, dl_ref, sg_ref, o_ref,
                      qs_ref, m_ref, l_ref, acc_ref, *, t, nk, lam_init):
    ki = pl.program_id(3)

    @pl.when(ki == 0)
    def _():
        q = q_ref[...].astype(F32) * (DH_B ** -0.5)
        lane = lax.broadcasted_iota(jnp.int32, (t, LANES), 1)
        qs_ref[0:t, :] = jnp.where(lane < DH_B, q, 0.0).astype(BF16)
        qs_ref[t:2 * t, :] = jnp.where(lane >= DH_B, q, 0.0).astype(BF16)
        m_ref[...] = jnp.full_like(m_ref, -jnp.inf)
        l_ref[...] = jnp.zeros_like(l_ref)
        acc_ref[...] = jnp.zeros_like(acc_ref)

    bias = b_ref[...]
    s = _dot_nt(qs_ref[...], k_ref[...]) + jnp.concatenate([bias, bias], axis=0)
    m_prev = m_ref[...]
    m_new = jnp.maximum(m_prev, jnp.max(s, axis=-1, keepdims=True))
    alpha = jnp.exp(m_prev - m_new)
    p = jnp.exp(s - m_new)
    l_ref[...] = alpha * l_ref[...] + jnp.sum(p, axis=-1, keepdims=True)
    acc_ref[...] = alpha * acc_ref[...] + _dot(p.astype(BF16), v_ref[...])
    m_ref[...] = m_new

    @pl.when(ki == nk - 1)
    def _():
        dl = dl_ref[...]
        lam = (jnp.exp(jnp.sum(dl[0:1, :] * dl[1:2, :], axis=-1, keepdims=True))
               - jnp.exp(jnp.sum(dl[2:3, :] * dl[3:4, :], axis=-1, keepdims=True)) + lam_init)
        a = acc_ref[...] / l_ref[...]
        o = a[0:t, :] - lam * a[t:2 * t, :]
        o_ref[...] = (_rms(o, sg_ref[...]) * (1.0 - lam_init)).astype(o_ref.dtype)


def diff_attention(p, bias, diff_lambda, subln, lam_init, bsz, s):
    t = ATT_TILE
    nq = s // t
    qb0 = (2 * QK_A + 2 * V_A) // LANES
    kb0 = qb0 + QK_B // LANES
    vb0 = kb0 + QK_B // LANES
    return pl.pallas_call(
        functools.partial(_diff_attn_kernel, t=t, nk=nq, lam_init=lam_init),
        grid=(bsz, H_B, nq, nq),
        in_specs=[
            pl.BlockSpec((t, LANES), lambda b, h, qi, ki: (b * nq + qi, qb0 + h)),
            pl.BlockSpec((t, LANES), lambda b, h, qi, ki: (b * nq + ki, kb0 + h)),
            pl.BlockSpec((t, LANES), lambda b, h, qi, ki: (b * nq + ki, vb0 + h)),
            pl.BlockSpec((None, None, t, t), lambda b, h, qi, ki: (jnp.clip(ki - qi, -2, 2) + 2, h, 0, 0)),
            pl.BlockSpec((4, DH_B), lambda b, h, qi, ki: (0, 0)),
            pl.BlockSpec((1, DV_B), lambda b, h, qi, ki: (0, 0)),
        ],
        out_specs=pl.BlockSpec((t, DV_B), lambda b, h, qi, ki: (b * nq + qi, h)),
        out_shape=jax.ShapeDtypeStruct((bsz * s, V_B), BF16),
        scratch_shapes=[
            pltpu.VMEM((2 * t, LANES), BF16),
            pltpu.VMEM((2 * t, 1), F32),
            pltpu.VMEM((2 * t, 1), F32),
            pltpu.VMEM((2 * t, DV_B), F32),
        ],
        compiler_params=_params(("parallel", "parallel", "parallel", "arbitrary")),
        name="diff_attention",
    )(p, p, p, bias, diff_lambda, subln.reshape(1, DV_B))


def _rope_k_kernel(c_ref, cs_ref, o_ref):
    tmp = c_ref[...] * cs_ref[...]
    y = tmp + pltpu.roll(tmp, ROPE_C, axis=1)
    lane = lax.broadcasted_iota(jnp.int32, y.shape, 1)
    o_ref[...] = jnp.where(lane < ROPE_C, y, 0.0).astype(o_ref.dtype)


def rope_k(c, cs, bsz, s):
    tr = min(512, s)
    nb = s // tr
    cb = (Q_LORA + KV_LORA) // LANES
    return pl.pallas_call(
        _rope_k_kernel,
        grid=(bsz, nb),
        in_specs=[
            pl.BlockSpec((tr, LANES), lambda b, i: (b * nb + i, cb)),
            pl.BlockSpec((tr, LANES), lambda b, i: (i, 0)),
        ],
        out_specs=pl.BlockSpec((tr, LANES), lambda b, i: (b * nb + i, 0)),
        out_shape=jax.ShapeDtypeStruct((bsz * s, LANES), BF16),
        compiler_params=_params(("parallel", "parallel")),
        name="rope_k",
    )(c, cs)


def _mla_attn_kernel(qa_ref, qb_ref, cs_ref, kn_ref, kr_ref, v_ref, o_ref,
                     qs_ref, m_ref, l_ref, acc_ref, *, nk):
    ki = pl.program_id(3)

    @pl.when(ki == 0)
    def _():
        scale = (NOPE_C + ROPE_C) ** -0.5
        tmp = qb_ref[...].astype(F32) * cs_ref[...]
        qr = tmp + pltpu.roll(tmp, ROPE_C, axis=1)
        qs_ref[:, 0:LANES] = (qa_ref[...].astype(F32) * scale).astype(BF16)
        qs_ref[:, LANES:2 * LANES] = (qr * scale).astype(BF16)
        m_ref[...] = jnp.full_like(m_ref, -jnp.inf)
        l_ref[...] = jnp.zeros_like(l_ref)
        acc_ref[...] = jnp.zeros_like(acc_ref)

    kc = jnp.concatenate([kn_ref[...], kr_ref[...]], axis=-1)
    s = _dot_nt(qs_ref[...], kc)
    m_prev = m_ref[...]
    m_new = jnp.maximum(m_prev, jnp.max(s, axis=-1, keepdims=True))
    alpha = jnp.exp(m_prev - m_new)
    p = jnp.exp(s - m_new)
    l_ref[...] = alpha * l_ref[...] + jnp.sum(p, axis=-1, keepdims=True)
    acc_ref[...] = alpha * acc_ref[...] + _dot(p.astype(BF16), v_ref[...])
    m_ref[...] = m_new

    @pl.when(ki == nk - 1)
    def _():
        o_ref[...] = (acc_ref[...] / l_ref[...]).astype(o_ref.dtype)


def mla_attention(q, kv, krp, cs, bsz, s):
    t = ATT_TILE
    nq = s // t
    return pl.pallas_call(
        functools.partial(_mla_attn_kernel, nk=nq),
        grid=(bsz, H_C, nq, nq),
        in_specs=[
            pl.BlockSpec((t, LANES), lambda b, h, qi, ki: (b * nq + qi, 2 * h)),
            pl.BlockSpec((t, LANES), lambda b, h, qi, ki: (b * nq + qi, 2 * h + 1)),
            pl.BlockSpec((t, LANES), lambda b, h, qi, ki: (qi, 0)),
            pl.BlockSpec((t, LANES), lambda b, h, qi, ki: (b * nq + ki, 2 * h)),
            pl.BlockSpec((t, LANES), lambda b, h, qi, ki: (b * nq + ki, 0)),
            pl.BlockSpec((t, LANES), lambda b, h, qi, ki: (b * nq + ki, 2 * h + 1)),
        ],
        out_specs=pl.BlockSpec((t, DV_C), lambda b, h, qi, ki: (b * nq + qi, h)),
        out_shape=jax.ShapeDtypeStruct((bsz * s, H_C * DV_C), BF16),
        scratch_shapes=[
            pltpu.VMEM((t, 2 * LANES), BF16),
            pltpu.VMEM((t, 1), F32),
            pltpu.VMEM((t, 1), F32),
            pltpu.VMEM((t, DV_C), F32),
        ],
        compiler_params=_params(("parallel", "parallel", "parallel", "arbitrary")),
        name="mla_attention",
    )(q, q, cs, kv, krp, kv)


def _rotate_half_cols(w):
    half = w.shape[-1] // 2
    return jnp.concatenate([-w[..., half:], w[..., :half]], axis=-1)


def _prepare_weights(w_in_e, a_log, dt_bias, w_out_e, w_in_c, w_qb, w_out_c, w_mlp1, w_mlp2, w_kvb):
    g0 = 2 * QK_A + 2 * V_A
    prep = {}
    prep["w_e_main"] = jnp.concatenate([w_in_e[:, :, :g0], w_in_e[:, :, g0 + 4 * H_A:]], axis=-1).astype(BF16)
    prep["w_e_gate"] = jnp.pad(w_in_e[:, :, g0:g0 + 4 * H_A], ((0, 0), (0, 0), (0, LANES - 4 * H_A))).astype(BF16)
    n_even = a_log.shape[0]
    pad_row = lambda v: jnp.pad(v.reshape(n_even, 1, 2 * H_A).astype(F32),
                                ((0, 0), (0, 0), (2 * H_A, LANES - 4 * H_A)))
    prep["alog_row"] = pad_row(a_log)
    prep["dtb_row"] = pad_row(dt_bias)
    prep["w_out_e"] = w_out_e.astype(BF16)
    k_rope_w = w_in_c[:, :, Q_LORA + KV_LORA:]
    prep["w_c"] = jnp.concatenate([w_in_c, _rotate_half_cols(k_rope_w)], axis=-1).astype(BF16)
    n_odd = w_qb.shape[0]
    wq = w_qb.reshape(n_odd, Q_LORA, H_C, NOPE_C + ROPE_C)
    wq_rope = wq[..., NOPE_C:]
    prep["w_qb"] = jnp.concatenate([wq[..., :NOPE_C], wq_rope, _rotate_half_cols(wq_rope)], axis=-1).reshape(
        n_odd, Q_LORA, H_C * 2 * LANES).astype(BF16)
    prep["w_kvb"] = w_kvb.astype(BF16)
    prep["w_out_c"] = w_out_c.astype(BF16)
    prep["w_mlp1"] = w_mlp1.astype(BF16)
    prep["w_mlp2"] = w_mlp2.astype(BF16)
    return prep


def _rope_table(s):
    inv_freq = ROPE_THETA ** (-jnp.arange(0, ROPE_C, 2, dtype=F32) / ROPE_C)
    ang = jnp.arange(s, dtype=F32)[:, None] * inv_freq[None, :]
    cos, sin = jnp.cos(ang), jnp.sin(ang)
    return jnp.concatenate([cos, cos, sin, sin], axis=-1)


def _trunk(x, pw, bias, norm_mix, norm_mlp, norm_final, conv_w, gdn_norm, diff_lambda, subln, q_norm, kv_norm):
    bsz, s, d = x.shape
    x = x.reshape(bsz * s, d)
    cs = _rope_table(s)
    for layer in range(DEPTH):
        i = layer // 2
        if layer % 2 == 0:
            p = norm_matmul(x, 0, d, norm_mix[layer], pw["w_e_main"][i], BF16, 1024, 1024)
            gates = norm_matmul(x, 0, d, norm_mix[layer], pw["w_e_gate"][i], F32, 1024, LANES)
            prep = gdn_prep(p, gates, conv_w[i], pw["alog_row"][i], pw["dtb_row"][i], bsz, s)
            o_a = gdn_scan(prep, p, gdn_norm[i], bsz, s)
            lam_init = 0.8 - 0.6 * math.exp(-0.3 * layer)
            o_b = diff_attention(p, bias, diff_lambda[i], subln[i], lam_init, bsz, s)
            x = matmul_res([o_a, o_b], pw["w_out_e"][i], x, 1024, 1024)
        else:
            c = norm_matmul(x, 0, d, norm_mix[layer], pw["w_c"][i], F32, 1024, pw["w_c"].shape[-1])
            q = norm_matmul(c, 0, Q_LORA, q_norm[i], pw["w_qb"][i], BF16, 1024, 1024)
            kv = norm_matmul(c, 1, KV_LORA, kv_norm[i], pw["w_kvb"][i], BF16, 1024, 1024)
            krp = rope_k(c, cs, bsz, s)
            o_c = mla_attention(q, kv, krp, cs, bsz, s)
            x = matmul_res([o_c], pw["w_out_c"][i], x, 1024, 1024)
        x = mlp(x, norm_mlp[layer], pw["w_mlp1"][layer], pw["w_mlp2"][layer], norm_final,
                layer == DEPTH - 1, 1024, 512)
    return x.reshape(bsz, s, d)


def kernel(x_prompt, x_sample, norm_mix, norm_mlp, norm_final, rel_bias, w_in_e, conv_w, a_log, dt_bias, gdn_norm, diff_lambda, subln, w_out_e, w_in_c, q_norm, w_qb, kv_norm, w_kvb, w_out_c, w_mlp1, w_mlp2):
    pw = _prepare_weights(w_in_e, a_log, dt_bias, w_out_e, w_in_c, w_qb, w_out_c, w_mlp1, w_mlp2, w_kvb)
    bias = bias_tiles(rel_bias)
    run = lambda x: _trunk(x, pw, bias, norm_mix, norm_mlp, norm_final, conv_w, gdn_norm, diff_lambda,
                           subln, q_norm, kv_norm)
    return (run(x_prompt), run(x_sample))
```

```python
import functools
import math

import jax
import jax.numpy as jnp
from jax import lax
from jax.experimental import pallas as pl
from jax.experimental.pallas import tpu as pltpu

F32 = jnp.float32
BF16 = jnp.bfloat16

D_MODEL = 2048
DEPTH = 4
H_A = 8
DK_A = 128
DV_A = 128
QK_A = H_A * DK_A
V_A = H_A * DV_A
H_B = 8
DH_B = 64
DV_B = 128
QK_B = H_B * 2 * DH_B
V_B = H_B * DV_B
N_BUCKETS = 32
MAX_DIST = 128
H_C = 16
Q_LORA = 512
KV_LORA = 512
NOPE_C = 128
ROPE_C = 64
DV_C = 128
ROPE_THETA = 10000.0
D_FF = 4 * D_MODEL
EPS = 1e-6

LOG2E = math.log2(math.e)
LANES = 128
BF16_SUBLANES = 16
VMEM_LIMIT = 56 * 1024 * 1024

GDN_CHUNK = 256
DIFF_TILE = 512
MLA_TQ = 1024
MLA_TK = 512


def _params(sem):
    return pltpu.CompilerParams(dimension_semantics=sem, vmem_limit_bytes=VMEM_LIMIT)


def _dot(a, b):
    return jnp.dot(a, b, preferred_element_type=F32)


def _dot_nt(a, b):
    return lax.dot_general(a, b, (((1,), (1,)), ((), ())), preferred_element_type=F32)


def _rms(x, g):
    ms = jnp.mean(x * x, axis=-1, keepdims=True)
    return x * lax.rsqrt(ms + EPS) * g


def _norm_matmul_kernel(x_ref, g_ref, w_ref, o_ref, h_ref):
    @pl.when(pl.program_id(1) == 0)
    def _():
        h_ref[...] = _rms(x_ref[...].astype(F32), g_ref[...]).astype(BF16)

    o_ref[...] = _dot(h_ref[...], w_ref[...]).astype(o_ref.dtype)


def norm_matmul(x, kblock, k, g, w, out_dtype, tm, tn):
    t = x.shape[0]
    n = w.shape[1]
    tm = min(tm, t)
    tn = min(tn, n)
    return pl.pallas_call(
        _norm_matmul_kernel,
        grid=(t // tm, n // tn),
        in_specs=[
            pl.BlockSpec((tm, k), lambda i, j: (i, kblock)),
            pl.BlockSpec((1, k), lambda i, j: (0, 0)),
            pl.BlockSpec((k, tn), lambda i, j: (0, j)),
        ],
        out_specs=pl.BlockSpec((tm, tn), lambda i, j: (i, j)),
        out_shape=jax.ShapeDtypeStruct((t, n), out_dtype),
        scratch_shapes=[pltpu.VMEM((tm, k), BF16)],
        compiler_params=_params(("parallel", "arbitrary")),
        name="norm_matmul",
    )(x, g.reshape(1, k).astype(F32), w)


def _norm_matmul_t_kernel(x_ref, g_ref, wt_ref, o_ref, h_ref):
    @pl.when(pl.program_id(1) == 0)
    def _():
        h_ref[...] = _rms(x_ref[...].astype(F32), g_ref[...]).astype(BF16)

    o_ref[...] = _dot_nt(wt_ref[...], h_ref[...]).astype(o_ref.dtype)


def norm_matmul_t(x, kblock, k, g, wt, out_dtype, tm, tn):
    t = x.shape[0]
    n = wt.shape[0]
    tm = min(tm, t)
    tn = min(tn, n)
    return pl.pallas_call(
        _norm_matmul_t_kernel,
        grid=(t // tm, n // tn),
        in_specs=[
            pl.BlockSpec((tm, k), lambda i, j: (i, kblock)),
            pl.BlockSpec((1, k), lambda i, j: (0, 0)),
            pl.BlockSpec((tn, k), lambda i, j: (j, 0)),
        ],
        out_specs=pl.BlockSpec((tn, tm), lambda i, j: (j, i)),
        out_shape=jax.ShapeDtypeStruct((n, t), out_dtype),
        scratch_shapes=[pltpu.VMEM((tm, k), BF16)],
        compiler_params=_params(("parallel", "arbitrary")),
        name="norm_matmul_t",
    )(x, g.reshape(1, k).astype(F32), wt)


def _matmul_res_kernel(*refs, widths):
    n = len(widths)
    w_ref, r_ref, o_ref = refs[n:]
    acc = r_ref[...]
    off = 0
    for a_ref, wd in zip(refs[:n], widths):
        acc = acc + _dot(a_ref[...], w_ref[off:off + wd, :])
        off += wd
    o_ref[...] = acc


def matmul_res(parts, w, res, tm, tn):
    t = res.shape[0]
    n = w.shape[1]
    tm = min(tm, t)
    widths = tuple(p.shape[1] for p in parts)
    return pl.pallas_call(
        functools.partial(_matmul_res_kernel, widths=widths),
        grid=(t // tm, n // tn),
        in_specs=[pl.BlockSpec((tm, wd), lambda i, j: (i, 0)) for wd in widths]
        + [
            pl.BlockSpec((sum(widths), tn), lambda i, j: (0, j)),
            pl.BlockSpec((tm, tn), lambda i, j: (i, j)),
        ],
        out_specs=pl.BlockSpec((tm, tn), lambda i, j: (i, j)),
        out_shape=jax.ShapeDtypeStruct((t, n), F32),
        compiler_params=_params(("parallel", "parallel")),
        name="matmul_res",
    )(*parts, w, res)


def _mlp_kernel(x_ref, g_ref, w1_ref, w2_ref, gf_ref, o_ref, h_ref, *, nf, final_norm):
    f = pl.program_id(1)

    @pl.when(f == 0)
    def _():
        x = x_ref[...]
        h_ref[...] = _rms(x, g_ref[...]).astype(BF16)
        o_ref[...] = x

    a = _dot(h_ref[...], w1_ref[...])
    a = jnp.square(jnp.maximum(a, 0.0)).astype(BF16)
    o_ref[...] += _dot(a, w2_ref[...])

    if final_norm:
        @pl.when(f == nf - 1)
        def _():
            o_ref[...] = _rms(o_ref[...], gf_ref[...])


def mlp(x, g, w1, w2, g_final, final_norm, tm, tf):
    t, d = x.shape
    tm = min(tm, t)
    nf = D_FF // tf
    return pl.pallas_call(
        functools.partial(_mlp_kernel, nf=nf, final_norm=final_norm),
        grid=(t // tm, nf),
        in_specs=[
            pl.BlockSpec((tm, d), lambda i, f: (i, 0)),
            pl.BlockSpec((1, d), lambda i, f: (0, 0)),
            pl.BlockSpec((d, tf), lambda i, f: (0, f)),
            pl.BlockSpec((tf, d), lambda i, f: (f, 0)),
            pl.BlockSpec((1, d), lambda i, f: (0, 0)),
        ],
        out_specs=pl.BlockSpec((tm, d), lambda i, f: (i, 0)),
        out_shape=jax.ShapeDtypeStruct((t, d), F32),
        scratch_shapes=[pltpu.VMEM((tm, d), BF16)],
        compiler_params=_params(("parallel", "arbitrary")),
        name="mlp",
    )(x, g.reshape(1, d), w1, w2, g_final.reshape(1, d))


def _softplus(x):
    return jnp.maximum(x, 0.0) + jnp.log1p(jnp.exp(-jnp.abs(x)))


def _gdn_prep_kernel(q_ref, qp_ref, qn_ref, k_ref, kp_ref, kn_ref, v_ref, vp_ref, vn_ref,
                     cwq_ref, cwk_ref, cwv_ref, gt_ref, alog_ref, dtb_ref,
                     qd_ref, kdt_ref, w_ref, u_ref, in_ref, egl_ref, *, cc, n):
    h = pl.program_id(1)
    i = pl.program_id(2)
    row = lax.broadcasted_iota(jnp.int32, (cc, LANES), 0)
    lane = lax.broadcasted_iota(jnp.int32, (cc, LANES), 1)

    def conv_silu(x_ref, p_ref, n_ref, cw_ref):
        x = x_ref[...].astype(F32)
        prev = jnp.where(i == 0, 0.0, p_ref[BF16_SUBLANES - 1:BF16_SUBLANES, :].astype(F32))
        nxt = jnp.where(i == n - 1, 0.0, n_ref[0:1, :].astype(F32))
        xm = jnp.where(row == 0, prev, pltpu.roll(x, 1, axis=0))
        xp = jnp.where(row == cc - 1, nxt, pltpu.roll(x, cc - 1, axis=0))
        cw = cw_ref[...]
        y = xm * cw[0:1, :] + x * cw[1:2, :] + xp * cw[2:3, :]
        return y * jax.nn.sigmoid(y)

    q = conv_silu(q_ref, qp_ref, qn_ref, cwq_ref)
    k = conv_silu(k_ref, kp_ref, kn_ref, cwk_ref)
    v = conv_silu(v_ref, vp_ref, vn_ref, cwv_ref)
    q = q * lax.rsqrt(jnp.sum(q * q, axis=-1, keepdims=True) + EPS) * (DK_A ** -0.5)
    k = k * lax.rsqrt(jnp.sum(k * k, axis=-1, keepdims=True) + EPS)
    k16 = k.astype(BF16)
    qk = _dot_nt(q.astype(BF16), k16)
    kk = _dot_nt(k16, k16)

    gt = gt_ref[...]
    beta_all = jax.nn.sigmoid(gt)
    g_all = -jnp.exp(alog_ref[...]) * _softplus(gt + dtb_ref[...])
    pre = g_all
    s = 1
    while s < cc:
        pre = pre + jnp.where(row >= s, pltpu.roll(pre, s, axis=0), 0.0)
        s *= 2
    tot = pre[cc - 1:cc, :]
    suf = tot - pre + g_all
    pre_t = pre.T
    suf_t = suf.T
    sub_t = lax.broadcasted_iota(jnp.int32, (LANES, cc), 0)
    lane1 = lax.broadcasted_iota(jnp.int32, (1, LANES), 1)

    def col(x, j):
        return jnp.sum(jnp.where(lane == j, x, 0.0), axis=1, keepdims=True)

    def rowv(xt, j):
        return jnp.sum(jnp.where(sub_t == j, xt, 0.0), axis=0, keepdims=True)

    ii = lax.broadcasted_iota(jnp.int32, (cc, cc), 0)
    jj = lax.broadcasted_iota(jnp.int32, (cc, cc), 1)
    eye = (ii == jj).astype(F32)

    for d in range(2):
        jb = d * H_A + h
        jg = 2 * H_A + d * H_A + h
        if d == 0:
            incl, strict = ii >= jj, ii > jj
            gcol, grow = col(pre, jg), rowv(pre_t, jg)
        else:
            incl, strict = ii <= jj, ii < jj
            gcol, grow = col(suf, jg), rowv(suf_t, jg)
        beta = col(beta_all, jb)
        glast = jnp.sum(jnp.where(lane1 == jg, tot, 0.0), axis=1, keepdims=True)
        dec = jnp.where(incl, jnp.exp(jnp.where(incl, gcol - grow, 0.0)), 0.0)
        kb = k * beta
        vb = v * beta
        a = jnp.where(strict, kk * dec, 0.0) * beta
        x = eye - jnp.where((ii >> 1) == (jj >> 1), a, 0.0)
        lg = 1
        while (1 << lg) < cc:
            off = ((ii >> (lg + 1)) == (jj >> (lg + 1))) & ((ii >> lg) != (jj >> lg))
            x16 = x.astype(BF16)
            y = _dot(jnp.where(off, a, 0.0).astype(BF16), x16)
            x = x - _dot(x16, y.astype(BF16))
            lg += 1
        x16 = x.astype(BF16)
        eg = jnp.exp(gcol)
        u_ref[d] = _dot(x16, vb.astype(BF16))
        w_ref[d] = _dot(x16, (kb * eg).astype(BF16)).astype(BF16)
        in_ref[d] = (qk * dec).astype(BF16)
        qd_ref[d] = (q * eg).astype(BF16)
        kdt_ref[d] = (k * jnp.exp(glast - gcol)).T.astype(BF16)
        egl_ref[d] = jnp.broadcast_to(jnp.exp(glast), (8, LANES))


def gdn_prep(p, gates, conv_w, alog_row, dtb_row, bsz, s):
    cc = GDN_CHUNK
    n = s // cc
    t = bsz * s
    rpb = cc // BF16_SUBLANES
    last_blk = t // BF16_SUBLANES - 1

    def main(cb):
        return pl.BlockSpec((cc, LANES), lambda b, h, i: (b * n + i, cb + h))

    def prev(cb):
        return pl.BlockSpec((BF16_SUBLANES, LANES),
                            lambda b, h, i: (jnp.maximum((b * n + i) * rpb - 1, 0), cb + h))

    def nxt(cb):
        return pl.BlockSpec((BF16_SUBLANES, LANES),
                            lambda b, h, i: (jnp.minimum((b * n + i + 1) * rpb, last_blk), cb + h))

    def cw(cb):
        return pl.BlockSpec((3, LANES), lambda b, h, i: (0, cb + h))

    row_spec = pl.BlockSpec((1, LANES), lambda b, h, i: (0, 0))
    seq_out = lambda last: pl.BlockSpec((2, None, None, cc, last), lambda b, h, i: (0, b, h, i, 0))
    out_shape = [
        jax.ShapeDtypeStruct((2, bsz, H_A, s, DK_A), BF16),
        jax.ShapeDtypeStruct((2, bsz, H_A, DK_A, s), BF16),
        jax.ShapeDtypeStruct((2, bsz, H_A, s, DK_A), BF16),
        jax.ShapeDtypeStruct((2, bsz, H_A, s, DV_A), F32),
        jax.ShapeDtypeStruct((2, bsz, H_A, s, cc), BF16),
        jax.ShapeDtypeStruct((2, bsz, H_A, n, 8, LANES), F32),
    ]
    out_specs = [
        seq_out(DK_A),
        pl.BlockSpec((2, None, None, DK_A, cc), lambda b, h, i: (0, b, h, 0, i)),
        seq_out(DK_A),
        seq_out(DV_A),
        seq_out(cc),
        pl.BlockSpec((2, None, None, None, 8, LANES), lambda b, h, i: (0, b, h, i, 0, 0)),
    ]
    kb0, vb0 = QK_A // LANES, 2 * QK_A // LANES
    return pl.pallas_call(
        functools.partial(_gdn_prep_kernel, cc=cc, n=n),
        grid=(bsz, H_A, n),
        in_specs=[main(0), prev(0), nxt(0), main(kb0), prev(kb0), nxt(kb0), main(vb0), prev(vb0), nxt(vb0),
                  cw(0), cw(kb0), cw(vb0),
                  pl.BlockSpec((cc, LANES), lambda b, h, i: (b * n + i, 0)),
                  row_spec, row_spec],
        out_specs=out_specs,
        out_shape=out_shape,
        compiler_params=_params(("parallel", "parallel", "parallel")),
        name="gdn_prep",
    )(p, p, p, p, p, p, p, p, p, conv_w, conv_w, conv_w, gates, alog_row, dtb_row)


def _gdn_scan_kernel(qdf, kdf, wf, uf, inf, eglf, qdb, kdb, wb, ub, inb, eglb, z_ref, gn_ref,
                     o_ref, st_ref, acc_ref, *, cc, n):
    i = pl.program_id(2)

    @pl.when(i == 0)
    def _():
        st_ref[...] = jnp.zeros_like(st_ref)
        acc_ref[...] = jnp.zeros_like(acc_ref)

    def step(d, qd, kdt, w, u, intra, egl, chunk):
        st = st_ref[d]
        st16 = st.astype(BF16)
        v_new = u[...] - _dot(w[...], st16)
        v16 = v_new.astype(BF16)
        o = _dot(qd[...], st16) + _dot(intra[...], v16)
        st_ref[d] = st * egl[0:1, :] + _dot(kdt[...], v16)
        r0 = pl.multiple_of(chunk * cc, cc)
        acc_ref[pl.ds(r0, cc), :] += o

    step(0, qdf, kdf, wf, uf, inf, eglf, i)
    step(1, qdb, kdb, wb, ub, inb, eglb, n - 1 - i)

    @pl.when(i == n - 1)
    def _():
        z = z_ref[...].astype(F32)
        y = _rms(acc_ref[...], gn_ref[...]) * (z * jax.nn.sigmoid(z))
        o_ref[...] = y.astype(o_ref.dtype)


def gdn_scan(prep, p, gdn_norm, bsz, s):
    qd, kdt, w, u, intra, egl = prep
    cc = GDN_CHUNK
    n = s // cc

    def seq(d, last):
        if d == 0:
            return pl.BlockSpec((None, None, None, cc, last), lambda b, h, i: (0, b, h, i, 0))
        return pl.BlockSpec((None, None, None, cc, last), lambda b, h, i: (1, b, h, n - 1 - i, 0))

    def kdt_spec(d):
        if d == 0:
            return pl.BlockSpec((None, None, None, DK_A, cc), lambda b, h, i: (0, b, h, 0, i))
        return pl.BlockSpec((None, None, None, DK_A, cc), lambda b, h, i: (1, b, h, 0, n - 1 - i))

    def egl_spec(d):
        if d == 0:
            return pl.BlockSpec((None, None, None, None, 8, LANES), lambda b, h, i: (0, b, h, i, 0, 0))
        return pl.BlockSpec((None, None, None, None, 8, LANES), lambda b, h, i: (1, b, h, n - 1 - i, 0, 0))

    def dir_specs(d):
        return [seq(d, DK_A), kdt_spec(d), seq(d, DK_A), seq(d, DV_A), seq(d, cc), egl_spec(d)]

    zb0 = (2 * QK_A + V_A) // LANES
    return pl.pallas_call(
        functools.partial(_gdn_scan_kernel, cc=cc, n=n),
        grid=(bsz, H_A, n),
        in_specs=dir_specs(0) + dir_specs(1) + [
            pl.BlockSpec((s, LANES), lambda b, h, i: (b, zb0 + h)),
            pl.BlockSpec((1, DV_A), lambda b, h, i: (0, 0)),
        ],
        out_specs=pl.BlockSpec((s, DV_A), lambda b, h, i: (b, h)),
        out_shape=jax.ShapeDtypeStruct((bsz * s, V_A), BF16),
        scratch_shapes=[pltpu.VMEM((2, DK_A, DV_A), F32), pltpu.VMEM((s, DV_A), F32)],
        compiler_params=_params(("parallel", "parallel", "arbitrary")),
        name="gdn_scan",
    )(qd, kdt, w, u, intra, egl, qd, kdt, w, u, intra, egl, p, gdn_norm.reshape(1, DV_A))


def _bucket_thresholds():
    nb = N_BUCKETS // 2
    max_exact = nb // 2
    out = []
    prev = 0
    for dist in range(max_exact, MAX_DIST):
        val = int(math.log(dist / max_exact) / math.log(MAX_DIST / max_exact) * (nb - max_exact))
        val = min(val, nb - 1 - max_exact)
        if val > prev:
            out.extend([dist] * (val - prev))
            prev = val
    return tuple(out)


def _bias_kernel(rb_ref, o_ref, *, t, thresholds):
    d = pl.program_id(0) - 2
    h = pl.program_id(1)
    nb = N_BUCKETS // 2
    max_exact = nb // 2
    r = lax.broadcasted_iota(jnp.int32, (t, t), 0)
    c = lax.broadcasted_iota(jnp.int32, (t, t), 1)
    rel = d * t + r - c
    dist = jnp.abs(rel)
    large = jnp.full((t, t), max_exact, jnp.int32)
    for th in thresholds:
        large = large + (dist >= th).astype(jnp.int32)
    bucket = jnp.where(rel > 0, nb, 0) + jnp.where(dist < max_exact, dist, large)
    out = jnp.zeros((t, t), F32)
    for b in range(N_BUCKETS):
        out = jnp.where(bucket == b, rb_ref[b, h], out)
    o_ref[...] = out * LOG2E


def bias_tiles(rel_bias):
    t = DIFF_TILE
    return pl.pallas_call(
        functools.partial(_bias_kernel, t=t, thresholds=_bucket_thresholds()),
        grid=(5, H_B),
        in_specs=[pl.BlockSpec(memory_space=pltpu.SMEM)],
        out_specs=pl.BlockSpec((None, None, t, t), lambda d, h: (d, h, 0, 0)),
        out_shape=jax.ShapeDtypeStruct((5, H_B, t, t), F32),
        compiler_params=_params(("parallel", "parallel")),
        name="t5_bias_tiles",
    )(rel_bias)


def _diff_attn_kernel(q_ref, k_ref, vt_ref, b_ref, dl_ref, sg_ref, o_ref,
                      qs_ref, m_ref, l_ref, acc_ref, *, t, nk, lam_init):
    ki = pl.program_id(3)

    @pl.when(ki == 0)
    def _():
        q = q_ref[...].astype(F32) * (DH_B ** -0.5 * LOG2E)
        sub = lax.broadcasted_iota(jnp.int32, (LANES, t), 0)
        qs_ref[:, 0:t] = jnp.where(sub < DH_B, q, 0.0).astype(BF16)
        qs_ref[:, t:2 * t] = jnp.where(sub >= DH_B, q, 0.0).astype(BF16)
        m_ref[...] = jnp.full_like(m_ref, -jnp.inf)
        l_ref[...] = jnp.zeros_like(l_ref)
        acc_ref[...] = jnp.zeros_like(acc_ref)

    bias = b_ref[...]
    st = _dot(k_ref[...], qs_ref[...]) + jnp.concatenate([bias, bias], axis=1)
    m_prev = m_ref[...]
    m_new = jnp.maximum(m_prev, jnp.max(st, axis=0, keepdims=True))
    alpha = jnp.exp2(m_prev - m_new)
    pt = jnp.exp2(st - m_new)
    l_ref[...] = alpha * l_ref[...] + jnp.sum(pt, axis=0, keepdims=True)
    acc_ref[...] = alpha * acc_ref[...] + _dot(vt_ref[...], pt.astype(BF16))
    m_ref[...] = m_new

    @pl.when(ki == nk - 1)
    def _():
        dl = dl_ref[...]
        lam = (jnp.exp(jnp.sum(dl[0:1, :] * dl[1:2, :], axis=-1, keepdims=True))
               - jnp.exp(jnp.sum(dl[2:3, :] * dl[3:4, :], axis=-1, keepdims=True)) + lam_init)
        a = acc_ref[...] / l_ref[...]
        o = (a[:, 0:t] - lam * a[:, t:2 * t]).T
        o_ref[...] = (_rms(o, sg_ref[...]) * (1.0 - lam_init)).astype(o_ref.dtype)


def diff_attention(p, pt, bias, diff_lambda, subln, lam_init, bsz, s):
    t = min(DIFF_TILE, s)
    nq = s // t
    kb0 = (2 * QK_A + 2 * V_A) // LANES
    vb0 = QK_B // LANES
    return pl.pallas_call(
        functools.partial(_diff_attn_kernel, t=t, nk=nq, lam_init=lam_init),
        grid=(bsz, H_B, nq, nq),
        in_specs=[
            pl.BlockSpec((LANES, t), lambda b, h, qi, ki: (h, b * nq + qi)),
            pl.BlockSpec((t, LANES), lambda b, h, qi, ki: (b * nq + ki, kb0 + h)),
            pl.BlockSpec((DV_B, t), lambda b, h, qi, ki: (vb0 + h, b * nq + ki)),
            pl.BlockSpec((None, None, t, t), lambda b, h, qi, ki: (jnp.clip(ki - qi, -2, 2) + 2, h, 0, 0)),
            pl.BlockSpec((4, DH_B), lambda b, h, qi, ki: (0, 0)),
            pl.BlockSpec((1, DV_B), lambda b, h, qi, ki: (0, 0)),
        ],
        out_specs=pl.BlockSpec((t, DV_B), lambda b, h, qi, ki: (b * nq + qi, h)),
        out_shape=jax.ShapeDtypeStruct((bsz * s, V_B), BF16),
        scratch_shapes=[
            pltpu.VMEM((LANES, 2 * t), BF16),
            pltpu.VMEM((1, 2 * t), F32),
            pltpu.VMEM((1, 2 * t), F32),
            pltpu.VMEM((DV_B, 2 * t), F32),
        ],
        compiler_params=_params(("parallel", "parallel", "parallel", "arbitrary")),
        name="diff_attention",
    )(pt, p, pt, bias, diff_lambda, subln.reshape(1, DV_B))


def _rope_k_kernel(c_ref, cs_ref, o_ref):
    tmp = c_ref[...] * cs_ref[...]
    y = tmp + pltpu.roll(tmp, ROPE_C, axis=1)
    lane = lax.broadcasted_iota(jnp.int32, y.shape, 1)
    o_ref[...] = jnp.where(lane < ROPE_C, y, 0.0).astype(o_ref.dtype)


def rope_k(c, cs, bsz, s):
    tr = min(512, s)
    nb = s // tr
    cb = (Q_LORA + KV_LORA) // LANES
    return pl.pallas_call(
        _rope_k_kernel,
        grid=(bsz, nb),
        in_specs=[
            pl.BlockSpec((tr, LANES), lambda b, i: (b * nb + i, cb)),
            pl.BlockSpec((tr, LANES), lambda b, i: (i, 0)),
        ],
        out_specs=pl.BlockSpec((tr, LANES), lambda b, i: (b * nb + i, 0)),
        out_shape=jax.ShapeDtypeStruct((bsz * s, LANES), BF16),
        compiler_params=_params(("parallel", "parallel")),
        name="rope_k",
    )(c, cs)


def _mla_attn_kernel(qa_ref, qb_ref, cs_ref, kn_ref, kr_ref, vt_ref, o_ref,
                     qs_ref, m_ref, l_ref, acc_ref, *, nk):
    ki = pl.program_id(3)

    @pl.when(ki == 0)
    def _():
        scale = (NOPE_C + ROPE_C) ** -0.5 * LOG2E
        tmp = qb_ref[...].astype(F32) * cs_ref[...]
        qr = tmp + pltpu.roll(tmp, ROPE_C, axis=0)
        qs_ref[0:LANES, :] = (qa_ref[...].astype(F32) * scale).astype(BF16)
        qs_ref[LANES:2 * LANES, :] = (qr * scale).astype(BF16)
        m_ref[...] = jnp.full_like(m_ref, -jnp.inf)
        l_ref[...] = jnp.zeros_like(l_ref)
        acc_ref[...] = jnp.zeros_like(acc_ref)

    kc = jnp.concatenate([kn_ref[...], kr_ref[...]], axis=-1)
    st = _dot(kc, qs_ref[...])
    m_prev = m_ref[...]
    m_new = jnp.maximum(m_prev, jnp.max(st, axis=0, keepdims=True))
    alpha = jnp.exp2(m_prev - m_new)
    pt = jnp.exp2(st - m_new)
    l_ref[...] = alpha * l_ref[...] + jnp.sum(pt, axis=0, keepdims=True)
    acc_ref[...] = alpha * acc_ref[...] + _dot(vt_ref[...], pt.astype(BF16))
    m_ref[...] = m_new

    @pl.when(ki == nk - 1)
    def _():
        o_ref[...] = (acc_ref[...] / l_ref[...]).T.astype(o_ref.dtype)


def mla_attention(qt, kn, krp, vt, cst, bsz, s):
    tq = min(MLA_TQ, s)
    tk = min(MLA_TK, s)
    nq = s // tq
    nk = s // tk
    return pl.pallas_call(
        functools.partial(_mla_attn_kernel, nk=nk),
        grid=(bsz, H_C, nq, nk),
        in_specs=[
            pl.BlockSpec((LANES, tq), lambda b, h, qi, ki: (2 * h, b * nq + qi)),
            pl.BlockSpec((LANES, tq), lambda b, h, qi, ki: (2 * h + 1, b * nq + qi)),
            pl.BlockSpec((LANES, tq), lambda b, h, qi, ki: (0, qi)),
            pl.BlockSpec((tk, LANES), lambda b, h, qi, ki: (b * nk + ki, h)),
            pl.BlockSpec((tk, LANES), lambda b, h, qi, ki: (b * nk + ki, 0)),
            pl.BlockSpec((DV_C, tk), lambda b, h, qi, ki: (h, b * nk + ki)),
        ],
        out_specs=pl.BlockSpec((tq, DV_C), lambda b, h, qi, ki: (b * nq + qi, h)),
        out_shape=jax.ShapeDtypeStruct((bsz * s, H_C * DV_C), BF16),
        scratch_shapes=[
            pltpu.VMEM((2 * LANES, tq), BF16),
            pltpu.VMEM((1, tq), F32),
            pltpu.VMEM((1, tq), F32),
            pltpu.VMEM((DV_C, tq), F32),
        ],
        compiler_params=_params(("parallel", "parallel", "parallel", "arbitrary")),
        name="mla_attention",
    )(qt, qt, cst, kn, krp, vt)


def _rotate_half_cols(w):
    half = w.shape[-1] // 2
    return jnp.concatenate([-w[..., half:], w[..., :half]], axis=-1)


def _prepare_weights(w_in_e, a_log, dt_bias, w_out_e, w_in_c, w_qb, w_out_c, w_mlp1, w_mlp2, w_kvb):
    g0 = 2 * QK_A + 2 * V_A
    prep = {}
    b0 = g0 + 4 * H_A
    t_last = lambda w: jnp.swapaxes(w, -1, -2)
    prep["w_e_main"] = jnp.concatenate([w_in_e[:, :, :g0], w_in_e[:, :, b0 + QK_B:b0 + 2 * QK_B]],
                                       axis=-1).astype(BF16)
    prep["w_e_t"] = t_last(jnp.concatenate([w_in_e[:, :, b0:b0 + QK_B], w_in_e[:, :, b0 + 2 * QK_B:]],
                                           axis=-1)).astype(BF16)
    prep["w_e_gate"] = jnp.pad(w_in_e[:, :, g0:g0 + 4 * H_A], ((0, 0), (0, 0), (0, LANES - 4 * H_A))).astype(BF16)
    n_even = a_log.shape[0]
    pad_row = lambda v: jnp.pad(v.reshape(n_even, 1, 2 * H_A).astype(F32),
                                ((0, 0), (0, 0), (2 * H_A, LANES - 4 * H_A)))
    prep["alog_row"] = pad_row(a_log)
    prep["dtb_row"] = pad_row(dt_bias)
    prep["w_out_e"] = w_out_e.astype(BF16)
    k_rope_w = w_in_c[:, :, Q_LORA + KV_LORA:]
    prep["w_c"] = jnp.concatenate([w_in_c, _rotate_half_cols(k_rope_w)], axis=-1).astype(BF16)
    n_odd = w_qb.shape[0]
    wq = w_qb.reshape(n_odd, Q_LORA, H_C, NOPE_C + ROPE_C)
    wq_rope = wq[..., NOPE_C:]
    prep["w_qb_t"] = t_last(jnp.concatenate([wq[..., :NOPE_C], wq_rope, _rotate_half_cols(wq_rope)], axis=-1).reshape(
        n_odd, Q_LORA, H_C * 2 * LANES)).astype(BF16)
    wkv = w_kvb.reshape(n_odd, KV_LORA, H_C, NOPE_C + DV_C)
    prep["w_kn"] = wkv[..., :NOPE_C].reshape(n_odd, KV_LORA, H_C * NOPE_C).astype(BF16)
    prep["w_v_t"] = t_last(wkv[..., NOPE_C:].reshape(n_odd, KV_LORA, H_C * DV_C)).astype(BF16)
    prep["w_out_c"] = w_out_c.astype(BF16)
    prep["w_mlp1"] = w_mlp1.astype(BF16)
    prep["w_mlp2"] = w_mlp2.astype(BF16)
    return prep


def _rope_table(s):
    inv_freq = ROPE_THETA ** (-jnp.arange(0, ROPE_C, 2, dtype=F32) / ROPE_C)
    ang = jnp.arange(s, dtype=F32)[:, None] * inv_freq[None, :]
    cos, sin = jnp.cos(ang), jnp.sin(ang)
    return jnp.concatenate([cos, cos, sin, sin], axis=-1)


def _trunk(x, pw, bias, norm_mix, norm_mlp, norm_final, conv_w, gdn_norm, diff_lambda, subln, q_norm, kv_norm):
    bsz, s, d = x.shape
    x = x.reshape(bsz * s, d)
    cs = _rope_table(s)
    cst = cs.T
    for layer in range(DEPTH):
        i = layer // 2
        if layer % 2 == 0:
            p = norm_matmul(x, 0, d, norm_mix[layer], pw["w_e_main"][i], BF16, 1024, 1024)
            pt = norm_matmul_t(x, 0, d, norm_mix[layer], pw["w_e_t"][i], BF16, 1024, 1024)
            gates = norm_matmul(x, 0, d, norm_mix[layer], pw["w_e_gate"][i], F32, 1024, LANES)
            prep = gdn_prep(p, gates, conv_w[i], pw["alog_row"][i], pw["dtb_row"][i], bsz, s)
            o_a = gdn_scan(prep, p, gdn_norm[i], bsz, s)
            lam_init = 0.8 - 0.6 * math.exp(-0.3 * layer)
            o_b = diff_attention(p, pt, bias, diff_lambda[i], subln[i], lam_init, bsz, s)
            x = matmul_res([o_a, o_b], pw["w_out_e"][i], x, 1024, 1024)
        else:
            c = norm_matmul(x, 0, d, norm_mix[layer], pw["w_c"][i], F32, 1024, pw["w_c"].shape[-1])
            qt = norm_matmul_t(c, 0, Q_LORA, q_norm[i], pw["w_qb_t"][i], BF16, 1024, 1024)
            kn = norm_matmul(c, 1, KV_LORA, kv_norm[i], pw["w_kn"][i], BF16, 1024, 1024)
            vt = norm_matmul_t(c, 1, KV_LORA, kv_norm[i], pw["w_v_t"][i], BF16, 1024, 1024)
            krp = rope_k(c, cs, bsz, s)
            o_c = mla_attention(qt, kn, krp, vt, cst, bsz, s)
            x = matmul_res([o_c], pw["w_out_c"][i], x, 1024, 1024)
        x = mlp(x, norm_mlp[layer], pw["w_mlp1"][layer], pw["w_mlp2"][layer], norm_final,
                layer == DEPTH - 1, 1024, 512)
    return x.reshape(bsz, s, d)


def kernel(x_prompt, x_sample, norm_mix, norm_mlp, norm_final, rel_bias, w_in_e, conv_w, a_log, dt_bias, gdn_norm, diff_lambda, subln, w_out_e, w_in_c, q_norm, w_qb, kv_norm, w_kvb, w_out_c, w_mlp1, w_mlp2):
    pw = _prepare_weights(w_in_e, a_log, dt_bias, w_out_e, w_in_c, w_qb, w_out_c, w_mlp1, w_mlp2, w_kvb)
    bias = bias_tiles(rel_bias)
    run = lambda x: _trunk(x, pw, bias, norm_mix, norm_mlp, norm_final, conv_w, gdn_norm, diff_lambda,
                           subln, q_norm, kv_norm)
    return (run(x_prompt), run(x_sample))
```

```python
import functools
import math

import jax
import jax.numpy as jnp
from jax import lax
from jax.experimental import pallas as pl
from jax.experimental.pallas import tpu as pltpu

F32 = jnp.float32
BF16 = jnp.bfloat16

D_MODEL = 2048
DEPTH = 4
H_A = 8
DK_A = 128
DV_A = 128
QK_A = H_A * DK_A
V_A = H_A * DV_A
H_B = 8
DH_B = 64
DV_B = 128
QK_B = H_B * 2 * DH_B
V_B = H_B * DV_B
N_BUCKETS = 32
MAX_DIST = 128
H_C = 16
Q_LORA = 512
KV_LORA = 512
NOPE_C = 128
ROPE_C = 64
DV_C = 128
ROPE_THETA = 10000.0
D_FF = 4 * D_MODEL
EPS = 1e-6

LOG2E = math.log2(math.e)
LANES = 128
BF16_SUBLANES = 16
VMEM_LIMIT = 56 * 1024 * 1024

GDN_BLOCK = 256
GDN_CHUNK = 64
GDN_HEADS_PER_STEP = 2
DIFF_TILE = 512
MLA_TQ = 1024
MLA_TK = 512


def _params(sem):
    return pltpu.CompilerParams(dimension_semantics=sem, vmem_limit_bytes=VMEM_LIMIT)


def _dot(a, b):
    return jnp.dot(a, b, preferred_element_type=F32)


def _dot_nt(a, b):
    return lax.dot_general(a, b, (((1,), (1,)), ((), ())), preferred_element_type=F32)


def _rms(x, g):
    ms = jnp.mean(x * x, axis=-1, keepdims=True)
    return x * lax.rsqrt(ms + EPS) * g


def _norm_matmul_kernel(x_ref, g_ref, w_ref, o_ref, h_ref):
    @pl.when(pl.program_id(1) == 0)
    def _():
        h_ref[...] = _rms(x_ref[...].astype(F32), g_ref[...]).astype(BF16)

    o_ref[...] = _dot(h_ref[...], w_ref[...]).astype(o_ref.dtype)


def norm_matmul(x, kblock, k, g, w, out_dtype, tm, tn):
    t = x.shape[0]
    n = w.shape[1]
    tm = min(tm, t)
    tn = min(tn, n)
    return pl.pallas_call(
        _norm_matmul_kernel,
        grid=(t // tm, n // tn),
        in_specs=[
            pl.BlockSpec((tm, k), lambda i, j: (i, kblock)),
            pl.BlockSpec((1, k), lambda i, j: (0, 0)),
            pl.BlockSpec((k, tn), lambda i, j: (0, j)),
        ],
        out_specs=pl.BlockSpec((tm, tn), lambda i, j: (i, j)),
        out_shape=jax.ShapeDtypeStruct((t, n), out_dtype),
        scratch_shapes=[pltpu.VMEM((tm, k), BF16)],
        compiler_params=_params(("parallel", "arbitrary")),
        name="norm_matmul",
    )(x, g.reshape(1, k).astype(F32), w)


def _norm_matmul_t_kernel(x_ref, g_ref, wt_ref, o_ref, h_ref):
    @pl.when(pl.program_id(1) == 0)
    def _():
        h_ref[...] = _rms(x_ref[...].astype(F32), g_ref[...]).astype(BF16)

    o_ref[...] = _dot_nt(wt_ref[...], h_ref[...]).astype(o_ref.dtype)


def norm_matmul_t(x, kblock, k, g, wt, out_dtype, tm, tn):
    t = x.shape[0]
    n = wt.shape[0]
    tm = min(tm, t)
    tn = min(tn, n)
    return pl.pallas_call(
        _norm_matmul_t_kernel,
        grid=(t // tm, n // tn),
        in_specs=[
            pl.BlockSpec((tm, k), lambda i, j: (i, kblock)),
            pl.BlockSpec((1, k), lambda i, j: (0, 0)),
            pl.BlockSpec((tn, k), lambda i, j: (j, 0)),
        ],
        out_specs=pl.BlockSpec((tn, tm), lambda i, j: (j, i)),
        out_shape=jax.ShapeDtypeStruct((n, t), out_dtype),
        scratch_shapes=[pltpu.VMEM((tm, k), BF16)],
        compiler_params=_params(("parallel", "arbitrary")),
        name="norm_matmul_t",
    )(x, g.reshape(1, k).astype(F32), wt)


def _matmul_res_kernel(*refs, widths):
    n = len(widths)
    w_ref, r_ref, o_ref = refs[n:]
    acc = r_ref[...]
    off = 0
    for a_ref, wd in zip(refs[:n], widths):
        acc = acc + _dot(a_ref[...], w_ref[off:off + wd, :])
        off += wd
    o_ref[...] = acc


def matmul_res(parts, w, res, tm, tn):
    t = res.shape[0]
    n = w.shape[1]
    tm = min(tm, t)
    widths = tuple(p.shape[1] for p in parts)
    return pl.pallas_call(
        functools.partial(_matmul_res_kernel, widths=widths),
        grid=(t // tm, n // tn),
        in_specs=[pl.BlockSpec((tm, wd), lambda i, j: (i, 0)) for wd in widths]
        + [
            pl.BlockSpec((sum(widths), tn), lambda i, j: (0, j)),
            pl.BlockSpec((tm, tn), lambda i, j: (i, j)),
        ],
        out_specs=pl.BlockSpec((tm, tn), lambda i, j: (i, j)),
        out_shape=jax.ShapeDtypeStruct((t, n), F32),
        compiler_params=_params(("parallel", "parallel")),
        name="matmul_res",
    )(*parts, w, res)


def _mlp_kernel(x_ref, g_ref, w1_ref, w2_ref, gf_ref, o_ref, h_ref, *, nf, final_norm):
    f = pl.program_id(1)

    @pl.when(f == 0)
    def _():
        x = x_ref[...]
        h_ref[...] = _rms(x, g_ref[...]).astype(BF16)
        o_ref[...] = x

    a = _dot(h_ref[...], w1_ref[...])
    a = jnp.square(jnp.maximum(a, 0.0)).astype(BF16)
    o_ref[...] += _dot(a, w2_ref[...])

    if final_norm:
        @pl.when(f == nf - 1)
        def _():
            o_ref[...] = _rms(o_ref[...], gf_ref[...])


def mlp(x, g, w1, w2, g_final, final_norm, tm, tf):
    t, d = x.shape
    tm = min(tm, t)
    nf = D_FF // tf
    return pl.pallas_call(
        functools.partial(_mlp_kernel, nf=nf, final_norm=final_norm),
        grid=(t // tm, nf),
        in_specs=[
            pl.BlockSpec((tm, d), lambda i, f: (i, 0)),
            pl.BlockSpec((1, d), lambda i, f: (0, 0)),
            pl.BlockSpec((d, tf), lambda i, f: (0, f)),
            pl.BlockSpec((tf, d), lambda i, f: (f, 0)),
            pl.BlockSpec((1, d), lambda i, f: (0, 0)),
        ],
        out_specs=pl.BlockSpec((tm, d), lambda i, f: (i, 0)),
        out_shape=jax.ShapeDtypeStruct((t, d), F32),
        scratch_shapes=[pltpu.VMEM((tm, d), BF16)],
        compiler_params=_params(("parallel", "arbitrary")),
        name="mlp",
    )(x, g.reshape(1, d), w1, w2, g_final.reshape(1, d))


def _softplus(x):
    return jnp.maximum(x, 0.0) + jnp.log1p(jnp.exp(-jnp.abs(x)))


def _gdn_prep_kernel(q_ref, qp_ref, qn_ref, k_ref, kp_ref, kn_ref, v_ref, vp_ref, vn_ref,
                     cwq_ref, cwk_ref, cwv_ref, gt_ref, alog_ref, dtb_ref,
                     qd_ref, kd_ref, w_ref, u_ref, in_ref, egl_ref, *, tb, cc, n):
    hp = pl.program_id(1)
    i = pl.program_id(2)
    width = GDN_HEADS_PER_STEP * LANES
    roww = lax.broadcasted_iota(jnp.int32, (tb, width), 0)

    def conv_silu(x_ref, p_ref, n_ref, cw_ref):
        x = x_ref[...].astype(F32)
        prev = jnp.where(i == 0, 0.0, p_ref[BF16_SUBLANES - 1:BF16_SUBLANES, :].astype(F32))
        nxt = jnp.where(i == n - 1, 0.0, n_ref[0:1, :].astype(F32))
        xm = jnp.where(roww == 0, prev, pltpu.roll(x, 1, axis=0))
        xp = jnp.where(roww == tb - 1, nxt, pltpu.roll(x, tb - 1, axis=0))
        cw = cw_ref[...]
        y = xm * cw[0:1, :] + x * cw[1:2, :] + xp * cw[2:3, :]
        return y * jax.nn.sigmoid(y)

    q_all = conv_silu(q_ref, qp_ref, qn_ref, cwq_ref)
    k_all = conv_silu(k_ref, kp_ref, kn_ref, cwk_ref)
    v_all = conv_silu(v_ref, vp_ref, vn_ref, cwv_ref)

    row = lax.broadcasted_iota(jnp.int32, (tb, LANES), 0)
    lane = lax.broadcasted_iota(jnp.int32, (tb, LANES), 1)
    rin = row & (cc - 1)
    gt = gt_ref[...]
    beta_all = jax.nn.sigmoid(gt)
    g_all = -jnp.exp(alog_ref[...]) * _softplus(gt + dtb_ref[...])
    pre = g_all
    suf = g_all
    s = 1
    while s < cc:
        pre = pre + jnp.where(rin >= s, pltpu.roll(pre, s, axis=0), 0.0)
        suf = suf + jnp.where(rin < cc - s, pltpu.roll(suf, tb - s, axis=0), 0.0)
        s *= 2
    tot = pre + suf - g_all
    pre_t = pre.T
    suf_t = suf.T
    sub_t = lax.broadcasted_iota(jnp.int32, (LANES, tb), 0)

    def col(x, j):
        return jnp.sum(jnp.where(lane == j, x, 0.0), axis=1, keepdims=True)

    def rowv(xt, j):
        return jnp.sum(jnp.where(sub_t == j, xt, 0.0), axis=0, keepdims=True)

    lgc = cc.bit_length() - 1
    ii = lax.broadcasted_iota(jnp.int32, (tb, tb), 0)
    jj = lax.broadcasted_iota(jnp.int32, (tb, tb), 1)
    same = (ii >> lgc) == (jj >> lgc)
    incl = (same & (ii >= jj), same & (ii <= jj))
    offdiag = ii != jj
    eye = (ii == jj).astype(F32)
    pair = (ii >> 1) == (jj >> 1)
    levels = [((ii >> (lg + 1)) == (jj >> (lg + 1))) & ((ii >> lg) != (jj >> lg)) for lg in range(1, lgc)]

    chains = []
    for e in range(GDN_HEADS_PER_STEP):
        h = hp * GDN_HEADS_PER_STEP + e
        cols = slice(e * LANES, (e + 1) * LANES)
        q, k, v = q_all[:, cols], k_all[:, cols], v_all[:, cols]
        q = q * lax.rsqrt(jnp.sum(q * q, axis=-1, keepdims=True) + EPS) * (DK_A ** -0.5)
        k = k * lax.rsqrt(jnp.sum(k * k, axis=-1, keepdims=True) + EPS)
        k16 = k.astype(BF16)
        qk = _dot_nt(q.astype(BF16), k16)
        kk = _dot_nt(k16, k16)
        for d in range(2):
            jb = d * H_A + h
            jg = 2 * H_A + d * H_A + h
            gcol, grow = (col(pre, jg), rowv(pre_t, jg)) if d == 0 else (col(suf, jg), rowv(suf_t, jg))
            beta = col(beta_all, jb)
            tcol = col(tot, jg)
            dec = jnp.where(incl[d], jnp.exp(jnp.where(incl[d], gcol - grow, 0.0)), 0.0)
            eg = jnp.exp(gcol)
            a = jnp.where(offdiag, kk * dec, 0.0) * beta
            m = qk * dec
            m = m[:, 0:LANES] + m[:, LANES:2 * LANES]
            m = m + pltpu.roll(m, cc, axis=1)
            in_ref[d, e] = m[:, 0:cc].astype(BF16)
            qd_ref[d, e] = (q * eg).astype(BF16)
            kd_ref[d, e] = (k * jnp.exp(tcol - gcol)).astype(BF16)
            eb = jnp.broadcast_to(jnp.exp(tcol), (tb, LANES))
            for c in range(tb // cc):
                egl_ref[d, e, c] = jnp.broadcast_to(eb[c * cc:c * cc + 1, :], (8, LANES))
            rhs = jnp.concatenate([v * beta, k * (beta * eg)], axis=1).astype(BF16)
            chains.append((d, e, a, rhs))

    xs = [eye - jnp.where(pair, a, 0.0) for (_, _, a, _) in chains]
    for off in levels:
        x16s = [x.astype(BF16) for x in xs]
        ys = [_dot(jnp.where(off, a, 0.0).astype(BF16), x16) for (_, _, a, _), x16 in zip(chains, x16s)]
        xs = [x - _dot(x16, y.astype(BF16)) for x, x16, y in zip(xs, x16s, ys)]
    for (d, e, _, rhs), x in zip(chains, xs):
        uw = _dot(x.astype(BF16), rhs)
        u_ref[d, e] = uw[:, 0:DV_A]
        w_ref[d, e] = uw[:, DV_A:].astype(BF16)


def gdn_prep(p, gates, conv_w, alog_row, dtb_row, bsz, s):
    tb, cc, hps = GDN_BLOCK, GDN_CHUNK, GDN_HEADS_PER_STEP
    assert tb == 2 * LANES and cc == LANES // 2
    n = s // tb
    t = bsz * s
    width = hps * LANES
    rpb = tb // BF16_SUBLANES
    last_blk = t // BF16_SUBLANES - 1

    def main(cb):
        return pl.BlockSpec((tb, width), lambda b, hp, i: (b * n + i, cb // hps + hp))

    def prev(cb):
        return pl.BlockSpec((BF16_SUBLANES, width),
                            lambda b, hp, i: (jnp.maximum((b * n + i) * rpb - 1, 0), cb // hps + hp))

    def nxt(cb):
        return pl.BlockSpec((BF16_SUBLANES, width),
                            lambda b, hp, i: (jnp.minimum((b * n + i + 1) * rpb, last_blk), cb // hps + hp))

    def cw(cb):
        return pl.BlockSpec((3, width), lambda b, hp, i: (0, cb // hps + hp))

    row_spec = pl.BlockSpec((1, LANES), lambda b, hp, i: (0, 0))
    seq_out = lambda last: pl.BlockSpec((2, None, hps, tb, last), lambda b, hp, i: (0, b, hp, i, 0))
    out_shape = [
        jax.ShapeDtypeStruct((2, bsz, H_A, s, DK_A), BF16),
        jax.ShapeDtypeStruct((2, bsz, H_A, s, DK_A), BF16),
        jax.ShapeDtypeStruct((2, bsz, H_A, s, DK_A), BF16),
        jax.ShapeDtypeStruct((2, bsz, H_A, s, DV_A), F32),
        jax.ShapeDtypeStruct((2, bsz, H_A, s, cc), BF16),
        jax.ShapeDtypeStruct((2, bsz, H_A, s // cc, 8, LANES), F32),
    ]
    out_specs = [
        seq_out(DK_A), seq_out(DK_A), seq_out(DK_A), seq_out(DV_A), seq_out(cc),
        pl.BlockSpec((2, None, hps, tb // cc, 8, LANES), lambda b, hp, i: (0, b, hp, i, 0, 0)),
    ]
    kb0, vb0 = QK_A // LANES, 2 * QK_A // LANES
    return pl.pallas_call(
        functools.partial(_gdn_prep_kernel, tb=tb, cc=cc, n=n),
        grid=(bsz, H_A // hps, n),
        in_specs=[main(0), prev(0), nxt(0), main(kb0), prev(kb0), nxt(kb0), main(vb0), prev(vb0), nxt(vb0),
                  cw(0), cw(kb0), cw(vb0),
                  pl.BlockSpec((tb, LANES), lambda b, hp, i: (b * n + i, 0)),
                  row_spec, row_spec],
        out_specs=out_specs,
        out_shape=out_shape,
        compiler_params=_params(("parallel", "parallel", "parallel")),
        name="gdn_prep",
    )(p, p, p, p, p, p, p, p, p, conv_w, conv_w, conv_w, gates, alog_row, dtb_row)


def _gdn_scan_kernel(qdf, kdf, wf, uf, inf, eglf, qdb, kdb, wb, ub, inb, eglb, z_ref, gn_ref,
                     o_ref, st_ref, acc_ref, *, tb, cc, n):
    i = pl.program_id(2)
    nc = tb // cc

    @pl.when(i == 0)
    def _():
        st_ref[...] = jnp.zeros_like(st_ref)
        acc_ref[...] = jnp.zeros_like(acc_ref)

    def chunk_step(st, qd, kd, w, u, intra, egl, c, row0):
        rows = slice(c * cc, (c + 1) * cc)
        st16 = st.astype(BF16)
        v_new = u[rows, :] - _dot(w[rows, :], st16)
        v16 = v_new.astype(BF16)
        o = _dot(qd[rows, :], st16) + _dot(intra[rows, :], v16)
        r0 = pl.multiple_of(row0 + c * cc, cc)
        acc_ref[pl.ds(r0, cc), :] += o
        kv = lax.dot_general(kd[rows, :], v16, (((0,), (0,)), ((), ())), preferred_element_type=F32)
        return st * egl[c, 0:1, :] + kv

    st_f = st_ref[0]
    st_b = st_ref[1]
    for c in range(nc):
        st_f = chunk_step(st_f, qdf, kdf, wf, uf, inf, eglf, c, i * tb)
        st_b = chunk_step(st_b, qdb, kdb, wb, ub, inb, eglb, nc - 1 - c, (n - 1 - i) * tb)
    st_ref[0] = st_f
    st_ref[1] = st_b

    @pl.when(i == n - 1)
    def _():
        z = z_ref[...].astype(F32)
        y = _rms(acc_ref[...], gn_ref[...]) * (z * jax.nn.sigmoid(z))
        o_ref[...] = y.astype(o_ref.dtype)


def gdn_scan(prep, p, gdn_norm, bsz, s):
    qd, kd, w, u, intra, egl = prep
    tb, cc = GDN_BLOCK, GDN_CHUNK
    n = s // tb

    def seq(d, last):
        if d == 0:
            return pl.BlockSpec((None, None, None, tb, last), lambda b, h, i: (0, b, h, i, 0))
        return pl.BlockSpec((None, None, None, tb, last), lambda b, h, i: (1, b, h, n - 1 - i, 0))

    def egl_spec(d):
        blk = (None, None, None, tb // cc, 8, LANES)
        if d == 0:
            return pl.BlockSpec(blk, lambda b, h, i: (0, b, h, i, 0, 0))
        return pl.BlockSpec(blk, lambda b, h, i: (1, b, h, n - 1 - i, 0, 0))

    def dir_specs(d):
        return [seq(d, DK_A), seq(d, DK_A), seq(d, DK_A), seq(d, DV_A), seq(d, cc), egl_spec(d)]

    zb0 = (2 * QK_A + V_A) // LANES
    return pl.pallas_call(
        functools.partial(_gdn_scan_kernel, tb=tb, cc=cc, n=n),
        grid=(bsz, H_A, n),
        in_specs=dir_specs(0) + dir_specs(1) + [
            pl.BlockSpec((s, LANES), lambda b, h, i: (b, zb0 + h)),
            pl.BlockSpec((1, DV_A), lambda b, h, i: (0, 0)),
        ],
        out_specs=pl.BlockSpec((s, DV_A), lambda b, h, i: (b, h)),
        out_shape=jax.ShapeDtypeStruct((bsz * s, V_A), BF16),
        scratch_shapes=[pltpu.VMEM((2, DK_A, DV_A), F32), pltpu.VMEM((s, DV_A), F32)],
        compiler_params=_params(("parallel", "parallel", "arbitrary")),
        name="gdn_scan",
    )(qd, kd, w, u, intra, egl, qd, kd, w, u, intra, egl, p, gdn_norm.reshape(1, DV_A))


def _bucket_thresholds():
    nb = N_BUCKETS // 2
    max_exact = nb // 2
    out = []
    prev = 0
    for dist in range(max_exact, MAX_DIST):
        val = int(math.log(dist / max_exact) / math.log(MAX_DIST / max_exact) * (nb - max_exact))
        val = min(val, nb - 1 - max_exact)
        if val > prev:
            out.extend([dist] * (val - prev))
            prev = val
    return tuple(out)


def _bias_kernel(rb_ref, o_ref, *, t, thresholds):
    d = pl.program_id(0) - 2
    h = pl.program_id(1)
    nb = N_BUCKETS // 2
    max_exact = nb // 2
    r = lax.broadcasted_iota(jnp.int32, (t, t), 0)
    c = lax.broadcasted_iota(jnp.int32, (t, t), 1)
    rel = d * t + r - c
    dist = jnp.abs(rel)
    large = jnp.full((t, t), max_exact, jnp.int32)
    for th in thresholds:
        large = large + (dist >= th).astype(jnp.int32)
    bucket = jnp.where(rel > 0, nb, 0) + jnp.where(dist < max_exact, dist, large)
    out = jnp.zeros((t, t), F32)
    for b in range(N_BUCKETS):
        out = jnp.where(bucket == b, rb_ref[b, h], out)
    o_ref[...] = out * LOG2E


def bias_tiles(rel_bias):
    t = DIFF_TILE
    return pl.pallas_call(
        functools.partial(_bias_kernel, t=t, thresholds=_bucket_thresholds()),
        grid=(5, H_B),
        in_specs=[pl.BlockSpec(memory_space=pltpu.SMEM)],
        out_specs=pl.BlockSpec((None, None, t, t), lambda d, h: (d, h, 0, 0)),
        out_shape=jax.ShapeDtypeStruct((5, H_B, t, t), F32),
        compiler_params=_params(("parallel", "parallel")),
        name="t5_bias_tiles",
    )(rel_bias)


def _diff_attn_kernel(q_ref, k_ref, vt_ref, b_ref, dl_ref, sg_ref, o_ref,
                      qs_ref, m_ref, l_ref, acc_ref, *, t, nk, lam_init):
    ki = pl.program_id(3)

    @pl.when(ki == 0)
    def _():
        q = q_ref[...].astype(F32) * (DH_B ** -0.5 * LOG2E)
        sub = lax.broadcasted_iota(jnp.int32, (LANES, t), 0)
        qs_ref[:, 0:t] = jnp.where(sub < DH_B, q, 0.0).astype(BF16)
        qs_ref[:, t:2 * t] = jnp.where(sub >= DH_B, q, 0.0).astype(BF16)
        m_ref[...] = jnp.full_like(m_ref, -jnp.inf)
        l_ref[...] = jnp.zeros_like(l_ref)
        acc_ref[...] = jnp.zeros_like(acc_ref)

    bias = b_ref[...]
    st = _dot(k_ref[...], qs_ref[...]) + jnp.concatenate([bias, bias], axis=1)
    m_prev = m_ref[...]
    m_new = jnp.maximum(m_prev, jnp.max(st, axis=0, keepdims=True))
    alpha = jnp.exp2(m_prev - m_new)
    pt = jnp.exp2(st - m_new)
    l_ref[...] = alpha * l_ref[...] + jnp.sum(pt, axis=0, keepdims=True)
    acc_ref[...] = alpha * acc_ref[...] + _dot(vt_ref[...], pt.astype(BF16))
    m_ref[...] = m_new

    @pl.when(ki == nk - 1)
    def _():
        dl = dl_ref[...]
        lam = (jnp.exp(jnp.sum(dl[0:1, :] * dl[1:2, :], axis=-1, keepdims=True))
               - jnp.exp(jnp.sum(dl[2:3, :] * dl[3:4, :], axis=-1, keepdims=True)) + lam_init)
        a = acc_ref[...] / l_ref[...]
        o = (a[:, 0:t] - lam * a[:, t:2 * t]).T
        o_ref[...] = (_rms(o, sg_ref[...]) * (1.0 - lam_init)).astype(o_ref.dtype)


def diff_attention(p, pt, bias, diff_lambda, subln, lam_init, bsz, s):
    t = min(DIFF_TILE, s)
    nq = s // t
    kb0 = (2 * QK_A + 2 * V_A) // LANES
    vb0 = QK_B // LANES
    return pl.pallas_call(
        functools.partial(_diff_attn_kernel, t=t, nk=nq, lam_init=lam_init),
        grid=(bsz, H_B, nq, nq),
        in_specs=[
            pl.BlockSpec((LANES, t), lambda b, h, qi, ki: (h, b * nq + qi)),
            pl.BlockSpec((t, LANES), lambda b, h, qi, ki: (b * nq + ki, kb0 + h)),
            pl.BlockSpec((DV_B, t), lambda b, h, qi, ki: (vb0 + h, b * nq + ki)),
            pl.BlockSpec((None, None, t, t), lambda b, h, qi, ki: (jnp.clip(ki - qi, -2, 2) + 2, h, 0, 0)),
            pl.BlockSpec((4, DH_B), lambda b, h, qi, ki: (0, 0)),
            pl.BlockSpec((1, DV_B), lambda b, h, qi, ki: (0, 0)),
        ],
        out_specs=pl.BlockSpec((t, DV_B), lambda b, h, qi, ki: (b * nq + qi, h)),
        out_shape=jax.ShapeDtypeStruct((bsz * s, V_B), BF16),
        scratch_shapes=[
            pltpu.VMEM((LANES, 2 * t), BF16),
            pltpu.VMEM((1, 2 * t), F32),
            pltpu.VMEM((1, 2 * t), F32),
            pltpu.VMEM((DV_B, 2 * t), F32),
        ],
        compiler_params=_params(("parallel", "parallel", "parallel", "arbitrary")),
        name="diff_attention",
    )(pt, p, pt, bias, diff_lambda, subln.reshape(1, DV_B))


def _rope_k_kernel(c_ref, cs_ref, o_ref):
    tmp = c_ref[...] * cs_ref[...]
    y = tmp + pltpu.roll(tmp, ROPE_C, axis=1)
    lane = lax.broadcasted_iota(jnp.int32, y.shape, 1)
    o_ref[...] = jnp.where(lane < ROPE_C, y, 0.0).astype(o_ref.dtype)


def rope_k(c, cs, bsz, s):
    tr = min(512, s)
    nb = s // tr
    cb = (Q_LORA + KV_LORA) // LANES
    return pl.pallas_call(
        _rope_k_kernel,
        grid=(bsz, nb),
        in_specs=[
            pl.BlockSpec((tr, LANES), lambda b, i: (b * nb + i, cb)),
            pl.BlockSpec((tr, LANES), lambda b, i: (i, 0)),
        ],
        out_specs=pl.BlockSpec((tr, LANES), lambda b, i: (b * nb + i, 0)),
        out_shape=jax.ShapeDtypeStruct((bsz * s, LANES), BF16),
        compiler_params=_params(("parallel", "parallel")),
        name="rope_k",
    )(c, cs)


def _mla_attn_kernel(qa_ref, qb_ref, cs_ref, kn_ref, kr_ref, vt_ref, o_ref,
                     qs_ref, m_ref, l_ref, acc_ref, *, nk):
    ki = pl.program_id(3)

    @pl.when(ki == 0)
    def _():
        scale = (NOPE_C + ROPE_C) ** -0.5 * LOG2E
        tmp = qb_ref[...].astype(F32) * cs_ref[...]
        qr = tmp + pltpu.roll(tmp, ROPE_C, axis=0)
        qs_ref[0:LANES, :] = (qa_ref[...].astype(F32) * scale).astype(BF16)
        qs_ref[LANES:2 * LANES, :] = (qr * scale).astype(BF16)
        m_ref[...] = jnp.full_like(m_ref, -jnp.inf)
        l_ref[...] = jnp.zeros_like(l_ref)
        acc_ref[...] = jnp.zeros_like(acc_ref)

    kc = jnp.concatenate([kn_ref[...], kr_ref[...]], axis=-1)
    st = _dot(kc, qs_ref[...])
    m_prev = m_ref[...]
    m_new = jnp.maximum(m_prev, jnp.max(st, axis=0, keepdims=True))
    alpha = jnp.exp2(m_prev - m_new)
    pt = jnp.exp2(st - m_new)
    l_ref[...] = alpha * l_ref[...] + jnp.sum(pt, axis=0, keepdims=True)
    acc_ref[...] = alpha * acc_ref[...] + _dot(vt_ref[...], pt.astype(BF16))
    m_ref[...] = m_new

    @pl.when(ki == nk - 1)
    def _():
        o_ref[...] = (acc_ref[...] / l_ref[...]).T.astype(o_ref.dtype)


def mla_attention(qt, kn, krp, vt, cst, bsz, s):
    tq = min(MLA_TQ, s)
    tk = min(MLA_TK, s)
    nq = s // tq
    nk = s // tk
    return pl.pallas_call(
        functools.partial(_mla_attn_kernel, nk=nk),
        grid=(bsz, H_C, nq, nk),
        in_specs=[
            pl.BlockSpec((LANES, tq), lambda b, h, qi, ki: (2 * h, b * nq + qi)),
            pl.BlockSpec((LANES, tq), lambda b, h, qi, ki: (2 * h + 1, b * nq + qi)),
            pl.BlockSpec((LANES, tq), lambda b, h, qi, ki: (0, qi)),
            pl.BlockSpec((tk, LANES), lambda b, h, qi, ki: (b * nk + ki, h)),
            pl.BlockSpec((tk, LANES), lambda b, h, qi, ki: (b * nk + ki, 0)),
            pl.BlockSpec((DV_C, tk), lambda b, h, qi, ki: (h, b * nk + ki)),
        ],
        out_specs=pl.BlockSpec((tq, DV_C), lambda b, h, qi, ki: (b * nq + qi, h)),
        out_shape=jax.ShapeDtypeStruct((bsz * s, H_C * DV_C), BF16),
        scratch_shapes=[
            pltpu.VMEM((2 * LANES, tq), BF16),
            pltpu.VMEM((1, tq), F32),
            pltpu.VMEM((1, tq), F32),
            pltpu.VMEM((DV_C, tq), F32),
        ],
        compiler_params=_params(("parallel", "parallel", "parallel", "arbitrary")),
        name="mla_attention",
    )(qt, qt, cst, kn, krp, vt)


def _rotate_half_cols(w):
    half = w.shape[-1] // 2
    return jnp.concatenate([-w[..., half:], w[..., :half]], axis=-1)


def _prepare_weights(w_in_e, a_log, dt_bias, w_out_e, w_in_c, w_qb, w_out_c, w_mlp1, w_mlp2, w_kvb):
    g0 = 2 * QK_A + 2 * V_A
    prep = {}
    b0 = g0 + 4 * H_A
    t_last = lambda w: jnp.swapaxes(w, -1, -2)
    prep["w_e_main"] = jnp.concatenate([w_in_e[:, :, :g0], w_in_e[:, :, b0 + QK_B:b0 + 2 * QK_B]],
                                       axis=-1).astype(BF16)
    prep["w_e_t"] = t_last(jnp.concatenate([w_in_e[:, :, b0:b0 + QK_B], w_in_e[:, :, b0 + 2 * QK_B:]],
                                           axis=-1)).astype(BF16)
    prep["w_e_gate"] = jnp.pad(w_in_e[:, :, g0:g0 + 4 * H_A], ((0, 0), (0, 0), (0, LANES - 4 * H_A))).astype(BF16)
    n_even = a_log.shape[0]
    pad_row = lambda v: jnp.pad(v.reshape(n_even, 1, 2 * H_A).astype(F32),
                                ((0, 0), (0, 0), (2 * H_A, LANES - 4 * H_A)))
    prep["alog_row"] = pad_row(a_log)
    prep["dtb_row"] = pad_row(dt_bias)
    prep["w_out_e"] = w_out_e.astype(BF16)
    k_rope_w = w_in_c[:, :, Q_LORA + KV_LORA:]
    prep["w_c"] = jnp.concatenate([w_in_c, _rotate_half_cols(k_rope_w)], axis=-1).astype(BF16)
    n_odd = w_qb.shape[0]
    wq = w_qb.reshape(n_odd, Q_LORA, H_C, NOPE_C + ROPE_C)
    wq_rope = wq[..., NOPE_C:]
    prep["w_qb_t"] = t_last(jnp.concatenate([wq[..., :NOPE_C], wq_rope, _rotate_half_cols(wq_rope)], axis=-1).reshape(
        n_odd, Q_LORA, H_C * 2 * LANES)).astype(BF16)
    wkv = w_kvb.reshape(n_odd, KV_LORA, H_C, NOPE_C + DV_C)
    prep["w_kn"] = wkv[..., :NOPE_C].reshape(n_odd, KV_LORA, H_C * NOPE_C).astype(BF16)
    prep["w_v_t"] = t_last(wkv[..., NOPE_C:].reshape(n_odd, KV_LORA, H_C * DV_C)).astype(BF16)
    prep["w_out_c"] = w_out_c.astype(BF16)
    prep["w_mlp1"] = w_mlp1.astype(BF16)
    prep["w_mlp2"] = w_mlp2.astype(BF16)
    return prep


def _rope_table(s):
    inv_freq = ROPE_THETA ** (-jnp.arange(0, ROPE_C, 2, dtype=F32) / ROPE_C)
    ang = jnp.arange(s, dtype=F32)[:, None] * inv_freq[None, :]
    cos, sin = jnp.cos(ang), jnp.sin(ang)
    return jnp.concatenate([cos, cos, sin, sin], axis=-1)


def _trunk(x, pw, bias, norm_mix, norm_mlp, norm_final, conv_w, gdn_norm, diff_lambda, subln, q_norm, kv_norm):
    bsz, s, d = x.shape
    x = x.reshape(bsz * s, d)
    cs = _rope_table(s)
    cst = cs.T
    for layer in range(DEPTH):
        i = layer // 2
        if layer % 2 == 0:
            p = norm_matmul(x, 0, d, norm_mix[layer], pw["w_e_main"][i], BF16, 1024, 1024)
            pt = norm_matmul_t(x, 0, d, norm_mix[layer], pw["w_e_t"][i], BF16, 1024, 1024)
            gates = norm_matmul(x, 0, d, norm_mix[layer], pw["w_e_gate"][i], F32, 1024, LANES)
            prep = gdn_prep(p, gates, conv_w[i], pw["alog_row"][i], pw["dtb_row"][i], bsz, s)
            o_a = gdn_scan(prep, p, gdn_norm[i], bsz, s)
            lam_init = 0.8 - 0.6 * math.exp(-0.3 * layer)
            o_b = diff_attention(p, pt, bias, diff_lambda[i], subln[i], lam_init, bsz, s)
            x = matmul_res([o_a, o_b], pw["w_out_e"][i], x, 1024, 1024)
        else:
            c = norm_matmul(x, 0, d, norm_mix[layer], pw["w_c"][i], F32, 1024, pw["w_c"].shape[-1])
            qt = norm_matmul_t(c, 0, Q_LORA, q_norm[i], pw["w_qb_t"][i], BF16, 1024, 1024)
            kn = norm_matmul(c, 1, KV_LORA, kv_norm[i], pw["w_kn"][i], BF16, 1024, 1024)
            vt = norm_matmul_t(c, 1, KV_LORA, kv_norm[i], pw["w_v_t"][i], BF16, 1024, 1024)
            krp = rope_k(c, cs, bsz, s)
            o_c = mla_attention(qt, kn, krp, vt, cst, bsz, s)
            x = matmul_res([o_c], pw["w_out_c"][i], x, 1024, 1024)
        x = mlp(x, norm_mlp[layer], pw["w_mlp1"][layer], pw["w_mlp2"][layer], norm_final,
                layer == DEPTH - 1, 1024, 512)
    return x.reshape(bsz, s, d)


def kernel(x_prompt, x_sample, norm_mix, norm_mlp, norm_final, rel_bias, w_in_e, conv_w, a_log, dt_bias, gdn_norm, diff_lambda, subln, w_out_e, w_in_c, q_norm, w_qb, kv_norm, w_kvb, w_out_c, w_mlp1, w_mlp2):
    pw = _prepare_weights(w_in_e, a_log, dt_bias, w_out_e, w_in_c, w_qb, w_out_c, w_mlp1, w_mlp2, w_kvb)
    bias = bias_tiles(rel_bias)
    run = lambda x: _trunk(x, pw, bias, norm_mix, norm_mlp, norm_final, conv_w, gdn_norm, diff_lambda,
                           subln, q_norm, kv_norm)
    return (run(x_prompt), run(x_sample))
```

```python
import functools
import math

import jax
import jax.numpy as jnp
from jax import lax
from jax.experimental import pallas as pl
from jax.experimental.pallas import tpu as pltpu

F32 = jnp.float32
BF16 = jnp.bfloat16

D_MODEL = 2048
DEPTH = 4
H_A = 8
DK_A = 128
DV_A = 128
QK_A = H_A * DK_A
V_A = H_A * DV_A
H_B = 8
DH_B = 64
DV_B = 128
QK_B = H_B * 2 * DH_B
V_B = H_B * DV_B
N_BUCKETS = 32
MAX_DIST = 128
H_C = 16
Q_LORA = 512
KV_LORA = 512
NOPE_C = 128
ROPE_C = 64
DV_C = 128
ROPE_THETA = 10000.0
D_FF = 4 * D_MODEL
EPS = 1e-6

LOG2E = math.log2(math.e)
LANES = 128
BF16_SUBLANES = 16
VMEM_LIMIT = 56 * 1024 * 1024

GDN_BLOCK = 256
GDN_CHUNK = 64
GDN_HEADS_PER_STEP = 2
DIFF_TILE = 512
MLA_TQ = 1024
MLA_TK = 512
ATT_PART = 256


def _params(sem):
    return pltpu.CompilerParams(dimension_semantics=sem, vmem_limit_bytes=VMEM_LIMIT)


def _dot(a, b):
    return jnp.dot(a, b, preferred_element_type=F32)


def _dot_nt(a, b):
    return lax.dot_general(a, b, (((1,), (1,)), ((), ())), preferred_element_type=F32)


def _rms(x, g):
    ms = jnp.mean(x * x, axis=-1, keepdims=True)
    return x * lax.rsqrt(ms + EPS) * g


def _norm_matmul_kernel(x_ref, g_ref, w_ref, o_ref, h_ref):
    @pl.when(pl.program_id(1) == 0)
    def _():
        h_ref[...] = _rms(x_ref[...].astype(F32), g_ref[...]).astype(BF16)

    o_ref[...] = _dot(h_ref[...], w_ref[...]).astype(o_ref.dtype)


def norm_matmul(x, kblock, k, g, w, out_dtype, tm, tn):
    t = x.shape[0]
    n = w.shape[1]
    tm = min(tm, t)
    tn = min(tn, n)
    return pl.pallas_call(
        _norm_matmul_kernel,
        grid=(t // tm, n // tn),
        in_specs=[
            pl.BlockSpec((tm, k), lambda i, j: (i, kblock)),
            pl.BlockSpec((1, k), lambda i, j: (0, 0)),
            pl.BlockSpec((k, tn), lambda i, j: (0, j)),
        ],
        out_specs=pl.BlockSpec((tm, tn), lambda i, j: (i, j)),
        out_shape=jax.ShapeDtypeStruct((t, n), out_dtype),
        scratch_shapes=[pltpu.VMEM((tm, k), BF16)],
        compiler_params=_params(("parallel", "arbitrary")),
        name="norm_matmul",
    )(x, g.reshape(1, k).astype(F32), w)


def _norm_matmul_t_kernel(x_ref, g_ref, wt_ref, o_ref, h_ref):
    @pl.when(pl.program_id(1) == 0)
    def _():
        h_ref[...] = _rms(x_ref[...].astype(F32), g_ref[...]).astype(BF16)

    o_ref[...] = _dot_nt(wt_ref[...], h_ref[...]).astype(o_ref.dtype)


def norm_matmul_t(x, kblock, k, g, wt, out_dtype, tm, tn):
    t = x.shape[0]
    n = wt.shape[0]
    tm = min(tm, t)
    tn = min(tn, n)
    return pl.pallas_call(
        _norm_matmul_t_kernel,
        grid=(t // tm, n // tn),
        in_specs=[
            pl.BlockSpec((tm, k), lambda i, j: (i, kblock)),
            pl.BlockSpec((1, k), lambda i, j: (0, 0)),
            pl.BlockSpec((tn, k), lambda i, j: (j, 0)),
        ],
        out_specs=pl.BlockSpec((tn, tm), lambda i, j: (j, i)),
        out_shape=jax.ShapeDtypeStruct((n, t), out_dtype),
        scratch_shapes=[pltpu.VMEM((tm, k), BF16)],
        compiler_params=_params(("parallel", "arbitrary")),
        name="norm_matmul_t",
    )(x, g.reshape(1, k).astype(F32), wt)


def _matmul_res_kernel(*refs, widths):
    n = len(widths)
    w_ref, r_ref, o_ref = refs[n:]
    acc = r_ref[...]
    off = 0
    for a_ref, wd in zip(refs[:n], widths):
        acc = acc + _dot(a_ref[...], w_ref[off:off + wd, :])
        off += wd
    o_ref[...] = acc


def matmul_res(parts, w, res, tm, tn):
    t = res.shape[0]
    n = w.shape[1]
    tm = min(tm, t)
    widths = tuple(p.shape[1] for p in parts)
    return pl.pallas_call(
        functools.partial(_matmul_res_kernel, widths=widths),
        grid=(t // tm, n // tn),
        in_specs=[pl.BlockSpec((tm, wd), lambda i, j: (i, 0)) for wd in widths]
        + [
            pl.BlockSpec((sum(widths), tn), lambda i, j: (0, j)),
            pl.BlockSpec((tm, tn), lambda i, j: (i, j)),
        ],
        out_specs=pl.BlockSpec((tm, tn), lambda i, j: (i, j)),
        out_shape=jax.ShapeDtypeStruct((t, n), F32),
        compiler_params=_params(("parallel", "parallel")),
        name="matmul_res",
    )(*parts, w, res)


def _mlp_kernel(x_ref, g_ref, w1_ref, w2_ref, gf_ref, o_ref, h_ref, *, nf, final_norm):
    f = pl.program_id(1)

    @pl.when(f == 0)
    def _():
        x = x_ref[...]
        h_ref[...] = _rms(x, g_ref[...]).astype(BF16)
        o_ref[...] = x

    a = _dot(h_ref[...], w1_ref[...])
    a = jnp.square(jnp.maximum(a, 0.0)).astype(BF16)
    o_ref[...] += _dot(a, w2_ref[...])

    if final_norm:
        @pl.when(f == nf - 1)
        def _():
            o_ref[...] = _rms(o_ref[...], gf_ref[...])


def mlp(x, g, w1, w2, g_final, final_norm, tm, tf):
    t, d = x.shape
    tm = min(tm, t)
    nf = D_FF // tf
    return pl.pallas_call(
        functools.partial(_mlp_kernel, nf=nf, final_norm=final_norm),
        grid=(t // tm, nf),
        in_specs=[
            pl.BlockSpec((tm, d), lambda i, f: (i, 0)),
            pl.BlockSpec((1, d), lambda i, f: (0, 0)),
            pl.BlockSpec((d, tf), lambda i, f: (0, f)),
            pl.BlockSpec((tf, d), lambda i, f: (f, 0)),
            pl.BlockSpec((1, d), lambda i, f: (0, 0)),
        ],
        out_specs=pl.BlockSpec((tm, d), lambda i, f: (i, 0)),
        out_shape=jax.ShapeDtypeStruct((t, d), F32),
        scratch_shapes=[pltpu.VMEM((tm, d), BF16)],
        compiler_params=_params(("parallel", "arbitrary")),
        name="mlp",
    )(x, g.reshape(1, d), w1, w2, g_final.reshape(1, d))


def _softplus(x):
    return jnp.maximum(x, 0.0) + jnp.log1p(jnp.exp(-jnp.abs(x)))


def _gdn_prep_kernel(q_ref, qp_ref, qn_ref, k_ref, kp_ref, kn_ref, v_ref, vp_ref, vn_ref,
                     cwq_ref, cwk_ref, cwv_ref, gt_ref, alog_ref, dtb_ref,
                     qd_ref, kd_ref, w_ref, u_ref, in_ref, egl_ref, *, tb, cc, n):
    hp = pl.program_id(1)
    i = pl.program_id(2)
    width = GDN_HEADS_PER_STEP * LANES
    roww = lax.broadcasted_iota(jnp.int32, (tb, width), 0)

    def conv_silu(x_ref, p_ref, n_ref, cw_ref):
        x = x_ref[...].astype(F32)
        prev = jnp.where(i == 0, 0.0, p_ref[BF16_SUBLANES - 1:BF16_SUBLANES, :].astype(F32))
        nxt = jnp.where(i == n - 1, 0.0, n_ref[0:1, :].astype(F32))
        xm = jnp.where(roww == 0, prev, pltpu.roll(x, 1, axis=0))
        xp = jnp.where(roww == tb - 1, nxt, pltpu.roll(x, tb - 1, axis=0))
        cw = cw_ref[...]
        y = xm * cw[0:1, :] + x * cw[1:2, :] + xp * cw[2:3, :]
        return y * jax.nn.sigmoid(y)

    q_all = conv_silu(q_ref, qp_ref, qn_ref, cwq_ref)
    k_all = conv_silu(k_ref, kp_ref, kn_ref, cwk_ref)
    v_all = conv_silu(v_ref, vp_ref, vn_ref, cwv_ref)

    row = lax.broadcasted_iota(jnp.int32, (tb, LANES), 0)
    lane = lax.broadcasted_iota(jnp.int32, (tb, LANES), 1)
    rin = row & (cc - 1)
    gt = gt_ref[...]
    beta_all = jax.nn.sigmoid(gt)
    g_all = -jnp.exp(alog_ref[...]) * _softplus(gt + dtb_ref[...])
    pre = g_all
    suf = g_all
    s = 1
    while s < cc:
        pre = pre + jnp.where(rin >= s, pltpu.roll(pre, s, axis=0), 0.0)
        suf = suf + jnp.where(rin < cc - s, pltpu.roll(suf, tb - s, axis=0), 0.0)
        s *= 2
    tot = pre + suf - g_all
    pre_t = pre.T
    suf_t = suf.T
    sub_t = lax.broadcasted_iota(jnp.int32, (LANES, tb), 0)

    def col(x, j):
        return jnp.sum(jnp.where(lane == j, x, 0.0), axis=1, keepdims=True)

    def rowv(xt, j):
        return jnp.sum(jnp.where(sub_t == j, xt, 0.0), axis=0, keepdims=True)

    lgc = cc.bit_length() - 1
    ii = lax.broadcasted_iota(jnp.int32, (tb, tb), 0)
    jj = lax.broadcasted_iota(jnp.int32, (tb, tb), 1)
    same = (ii >> lgc) == (jj >> lgc)
    incl = (same & (ii >= jj), same & (ii <= jj))
    offdiag = ii != jj
    eye = (ii == jj).astype(F32)
    pair = (ii >> 1) == (jj >> 1)
    levels = [((ii >> (lg + 1)) == (jj >> (lg + 1))) & ((ii >> lg) != (jj >> lg)) for lg in range(1, lgc)]

    chains = []
    for e in range(GDN_HEADS_PER_STEP):
        h = hp * GDN_HEADS_PER_STEP + e
        cols = slice(e * LANES, (e + 1) * LANES)
        q, k, v = q_all[:, cols], k_all[:, cols], v_all[:, cols]
        q = q * lax.rsqrt(jnp.sum(q * q, axis=-1, keepdims=True) + EPS) * (DK_A ** -0.5)
        k = k * lax.rsqrt(jnp.sum(k * k, axis=-1, keepdims=True) + EPS)
        k16 = k.astype(BF16)
        qk = _dot_nt(q.astype(BF16), k16)
        kk = _dot_nt(k16, k16)
        for d in range(2):
            jb = d * H_A + h
            jg = 2 * H_A + d * H_A + h
            gcol, grow = (col(pre, jg), rowv(pre_t, jg)) if d == 0 else (col(suf, jg), rowv(suf_t, jg))
            beta = col(beta_all, jb)
            tcol = col(tot, jg)
            dec = jnp.where(incl[d], jnp.exp(jnp.where(incl[d], gcol - grow, 0.0)), 0.0)
            eg = jnp.exp(gcol)
            a = jnp.where(offdiag, kk * dec, 0.0) * beta
            m = qk * dec
            m = m[:, 0:LANES] + m[:, LANES:2 * LANES]
            m = m + pltpu.roll(m, cc, axis=1)
            in_ref[d, e] = m[:, 0:cc].astype(BF16)
            qd_ref[d, e] = (q * eg).astype(BF16)
            kd_ref[d, e] = (k * jnp.exp(tcol - gcol)).astype(BF16)
            eb = jnp.broadcast_to(jnp.exp(tcol), (tb, LANES))
            for c in range(tb // cc):
                egl_ref[d, e, c] = jnp.broadcast_to(eb[c * cc:c * cc + 1, :], (8, LANES))
            rhs = jnp.concatenate([v * beta, k * (beta * eg)], axis=1).astype(BF16)
            chains.append((d, e, a, rhs))

    xs = [eye - jnp.where(pair, a, 0.0) for (_, _, a, _) in chains]
    for off in levels:
        x16s = [x.astype(BF16) for x in xs]
        ys = [_dot(jnp.where(off, a, 0.0).astype(BF16), x16) for (_, _, a, _), x16 in zip(chains, x16s)]
        xs = [x - _dot(x16, y.astype(BF16)) for x, x16, y in zip(xs, x16s, ys)]
    for (d, e, _, rhs), x in zip(chains, xs):
        uw = _dot(x.astype(BF16), rhs)
        u_ref[d, e] = uw[:, 0:DV_A]
        w_ref[d, e] = uw[:, DV_A:].astype(BF16)


def gdn_prep(p, gates, conv_w, alog_row, dtb_row, bsz, s):
    tb, cc, hps = GDN_BLOCK, GDN_CHUNK, GDN_HEADS_PER_STEP
    assert tb == 2 * LANES and cc == LANES // 2
    n = s // tb
    t = bsz * s
    width = hps * LANES
    rpb = tb // BF16_SUBLANES
    last_blk = t // BF16_SUBLANES - 1

    def main(cb):
        return pl.BlockSpec((tb, width), lambda b, hp, i: (b * n + i, cb // hps + hp))

    def prev(cb):
        return pl.BlockSpec((BF16_SUBLANES, width),
                            lambda b, hp, i: (jnp.maximum((b * n + i) * rpb - 1, 0), cb // hps + hp))

    def nxt(cb):
        return pl.BlockSpec((BF16_SUBLANES, width),
                            lambda b, hp, i: (jnp.minimum((b * n + i + 1) * rpb, last_blk), cb // hps + hp))

    def cw(cb):
        return pl.BlockSpec((3, width), lambda b, hp, i: (0, cb // hps + hp))

    row_spec = pl.BlockSpec((1, LANES), lambda b, hp, i: (0, 0))
    seq_out = lambda last: pl.BlockSpec((2, None, hps, tb, last), lambda b, hp, i: (0, b, hp, i, 0))
    out_shape = [
        jax.ShapeDtypeStruct((2, bsz, H_A, s, DK_A), BF16),
        jax.ShapeDtypeStruct((2, bsz, H_A, s, DK_A), BF16),
        jax.ShapeDtypeStruct((2, bsz, H_A, s, DK_A), BF16),
        jax.ShapeDtypeStruct((2, bsz, H_A, s, DV_A), F32),
        jax.ShapeDtypeStruct((2, bsz, H_A, s, cc), BF16),
        jax.ShapeDtypeStruct((2, bsz, H_A, s // cc, 8, LANES), F32),
    ]
    out_specs = [
        seq_out(DK_A), seq_out(DK_A), seq_out(DK_A), seq_out(DV_A), seq_out(cc),
        pl.BlockSpec((2, None, hps, tb // cc, 8, LANES), lambda b, hp, i: (0, b, hp, i, 0, 0)),
    ]
    kb0, vb0 = QK_A // LANES, 2 * QK_A // LANES
    return pl.pallas_call(
        functools.partial(_gdn_prep_kernel, tb=tb, cc=cc, n=n),
        grid=(bsz, H_A // hps, n),
        in_specs=[main(0), prev(0), nxt(0), main(kb0), prev(kb0), nxt(kb0), main(vb0), prev(vb0), nxt(vb0),
                  cw(0), cw(kb0), cw(vb0),
                  pl.BlockSpec((tb, LANES), lambda b, hp, i: (b * n + i, 0)),
                  row_spec, row_spec],
        out_specs=out_specs,
        out_shape=out_shape,
        compiler_params=_params(("parallel", "parallel", "parallel")),
        name="gdn_prep",
    )(p, p, p, p, p, p, p, p, p, conv_w, conv_w, conv_w, gates, alog_row, dtb_row)


def _gdn_scan_kernel(qdf, kdf, wf, uf, inf, eglf, qdb, kdb, wb, ub, inb, eglb, z_ref, gn_ref,
                     o_ref, st_ref, acc_ref, *, tb, cc, n):
    i = pl.program_id(2)
    nc = tb // cc

    @pl.when(i == 0)
    def _():
        st_ref[...] = jnp.zeros_like(st_ref)
        acc_ref[...] = jnp.zeros_like(acc_ref)

    def chunk_step(st, qd, kd, w, u, intra, egl, c, row0):
        rows = slice(c * cc, (c + 1) * cc)
        st16 = st.astype(BF16)
        v_new = u[rows, :] - _dot(w[rows, :], st16)
        v16 = v_new.astype(BF16)
        o = _dot(qd[rows, :], st16) + _dot(intra[rows, :], v16)
        r0 = pl.multiple_of(row0 + c * cc, cc)
        acc_ref[pl.ds(r0, cc), :] += o
        kv = lax.dot_general(kd[rows, :], v16, (((0,), (0,)), ((), ())), preferred_element_type=F32)
        return st * egl[c, 0:1, :] + kv

    st_f = st_ref[0]
    st_b = st_ref[1]
    for c in range(nc):
        st_f = chunk_step(st_f, qdf, kdf, wf, uf, inf, eglf, c, i * tb)
        st_b = chunk_step(st_b, qdb, kdb, wb, ub, inb, eglb, nc - 1 - c, (n - 1 - i) * tb)
    st_ref[0] = st_f
    st_ref[1] = st_b

    @pl.when(i == n - 1)
    def _():
        z = z_ref[...].astype(F32)
        y = _rms(acc_ref[...], gn_ref[...]) * (z * jax.nn.sigmoid(z))
        o_ref[...] = y.astype(o_ref.dtype)


def gdn_scan(prep, p, gdn_norm, bsz, s):
    qd, kd, w, u, intra, egl = prep
    tb, cc = GDN_BLOCK, GDN_CHUNK
    n = s // tb

    def seq(d, last):
        if d == 0:
            return pl.BlockSpec((None, None, None, tb, last), lambda b, h, i: (0, b, h, i, 0))
        return pl.BlockSpec((None, None, None, tb, last), lambda b, h, i: (1, b, h, n - 1 - i, 0))

    def egl_spec(d):
        blk = (None, None, None, tb // cc, 8, LANES)
        if d == 0:
            return pl.BlockSpec(blk, lambda b, h, i: (0, b, h, i, 0, 0))
        return pl.BlockSpec(blk, lambda b, h, i: (1, b, h, n - 1 - i, 0, 0))

    def dir_specs(d):
        return [seq(d, DK_A), seq(d, DK_A), seq(d, DK_A), seq(d, DV_A), seq(d, cc), egl_spec(d)]

    zb0 = (2 * QK_A + V_A) // LANES
    return pl.pallas_call(
        functools.partial(_gdn_scan_kernel, tb=tb, cc=cc, n=n),
        grid=(bsz, H_A, n),
        in_specs=dir_specs(0) + dir_specs(1) + [
            pl.BlockSpec((s, LANES), lambda b, h, i: (b, zb0 + h)),
            pl.BlockSpec((1, DV_A), lambda b, h, i: (0, 0)),
        ],
        out_specs=pl.BlockSpec((s, DV_A), lambda b, h, i: (b, h)),
        out_shape=jax.ShapeDtypeStruct((bsz * s, V_A), BF16),
        scratch_shapes=[pltpu.VMEM((2, DK_A, DV_A), F32), pltpu.VMEM((s, DV_A), F32)],
        compiler_params=_params(("parallel", "parallel", "arbitrary")),
        name="gdn_scan",
    )(qd, kd, w, u, intra, egl, qd, kd, w, u, intra, egl, p, gdn_norm.reshape(1, DV_A))


def _bucket_thresholds():
    nb = N_BUCKETS // 2
    max_exact = nb // 2
    out = []
    prev = 0
    for dist in range(max_exact, MAX_DIST):
        val = int(math.log(dist / max_exact) / math.log(MAX_DIST / max_exact) * (nb - max_exact))
        val = min(val, nb - 1 - max_exact)
        if val > prev:
            out.extend([dist] * (val - prev))
            prev = val
    return tuple(out)


def _bias_kernel(rb_ref, o_ref, *, t, thresholds):
    d = pl.program_id(0) - 2
    h = pl.program_id(1)
    nb = N_BUCKETS // 2
    max_exact = nb // 2
    r = lax.broadcasted_iota(jnp.int32, (t, t), 0)
    c = lax.broadcasted_iota(jnp.int32, (t, t), 1)
    rel = d * t + r - c
    dist = jnp.abs(rel)
    large = jnp.full((t, t), max_exact, jnp.int32)
    for th in thresholds:
        large = large + (dist >= th).astype(jnp.int32)
    bucket = jnp.where(rel > 0, nb, 0) + jnp.where(dist < max_exact, dist, large)
    out = jnp.zeros((t, t), F32)
    for b in range(N_BUCKETS):
        out = jnp.where(bucket == b, rb_ref[b, h], out)
    o_ref[...] = out * LOG2E


def bias_tiles(rel_bias):
    t = DIFF_TILE
    return pl.pallas_call(
        functools.partial(_bias_kernel, t=t, thresholds=_bucket_thresholds()),
        grid=(5, H_B),
        in_specs=[pl.BlockSpec(memory_space=pltpu.SMEM)],
        out_specs=pl.BlockSpec((None, None, t, t), lambda d, h: (d, h, 0, 0)),
        out_shape=jax.ShapeDtypeStruct((5, H_B, t, t), F32),
        compiler_params=_params(("parallel", "parallel")),
        name="t5_bias_tiles",
    )(rel_bias)


def _online_softmax(n_tiles, scores, vt, acc_ref, nq):
    part = min(ATT_PART, nq)
    parts = [slice(c, c + part) for c in range(0, nq, part)]
    m = [None] * len(parts)
    l = [None] * len(parts)
    sts = [scores(0, cols) for cols in parts]
    for j in range(n_tiles):
        nxt = [scores(j + 1, cols) for cols in parts] if j + 1 < n_tiles else None
        pts = []
        for i, st in enumerate(sts):
            mx = jnp.max(st, axis=0, keepdims=True)
            if j == 0:
                m[i], alpha = mx, None
            else:
                m_new = jnp.maximum(m[i], mx)
                alpha = jnp.exp2(m[i] - m_new)
                m[i] = m_new
            pt = jnp.exp2(st - m[i])
            ps = jnp.sum(pt, axis=0, keepdims=True)
            l[i] = ps if j == 0 else alpha * l[i] + ps
            pts.append((alpha, pt.astype(BF16)))
        v = vt(j)
        for cols, (alpha, pt16) in zip(parts, pts):
            pv = _dot(v, pt16)
            acc_ref[:, cols] = pv if j == 0 else alpha * acc_ref[:, cols] + pv
        sts = nxt
    return jnp.concatenate(l, axis=1)


def _diff_attn_kernel(q_ref, k_ref, vt_ref, b_ref, dl_ref, sg_ref, o_ref, qs_ref, acc_ref, *, t, n_tiles, lam_init):
    qi = pl.program_id(2)
    q = q_ref[...].astype(F32) * (DH_B ** -0.5 * LOG2E)
    sub = lax.broadcasted_iota(jnp.int32, (LANES, t), 0)
    qs_ref[:, 0:t] = jnp.where(sub < DH_B, q, 0.0).astype(BF16)
    qs_ref[:, t:2 * t] = jnp.where(sub >= DH_B, q, 0.0).astype(BF16)

    def scores(j, cols):
        b0 = cols.start % t
        width = cols.stop - cols.start
        tile = jnp.clip(j - qi, -2, 2) + 2
        return _dot(k_ref[j * t:(j + 1) * t, :], qs_ref[:, cols]) + b_ref[tile, :, b0:b0 + width]

    l = _online_softmax(n_tiles, scores, lambda j: vt_ref[:, j * t:(j + 1) * t], acc_ref, 2 * t)

    dl = dl_ref[...]
    lam = (jnp.exp(jnp.sum(dl[0:1, :] * dl[1:2, :], axis=-1, keepdims=True))
           - jnp.exp(jnp.sum(dl[2:3, :] * dl[3:4, :], axis=-1, keepdims=True)) + lam_init)
    a = acc_ref[...] / l
    o = (a[:, 0:t] - lam * a[:, t:2 * t]).T
    o_ref[...] = (_rms(o, sg_ref[...]) * (1.0 - lam_init)).astype(o_ref.dtype)


def diff_attention(p, pt, bias, diff_lambda, subln, lam_init, bsz, s):
    t = min(DIFF_TILE, s)
    nq = s // t
    kb0 = (2 * QK_A + 2 * V_A) // LANES
    vb0 = QK_B // LANES
    return pl.pallas_call(
        functools.partial(_diff_attn_kernel, t=t, n_tiles=nq, lam_init=lam_init),
        grid=(bsz, H_B, nq),
        in_specs=[
            pl.BlockSpec((LANES, t), lambda b, h, qi: (h, b * nq + qi)),
            pl.BlockSpec((s, LANES), lambda b, h, qi: (b, kb0 + h)),
            pl.BlockSpec((DV_B, s), lambda b, h, qi: (vb0 + h, b)),
            pl.BlockSpec((5, None, t, t), lambda b, h, qi: (0, h, 0, 0)),
            pl.BlockSpec((4, DH_B), lambda b, h, qi: (0, 0)),
            pl.BlockSpec((1, DV_B), lambda b, h, qi: (0, 0)),
        ],
        out_specs=pl.BlockSpec((t, DV_B), lambda b, h, qi: (b * nq + qi, h)),
        out_shape=jax.ShapeDtypeStruct((bsz * s, V_B), BF16),
        scratch_shapes=[
            pltpu.VMEM((LANES, 2 * t), BF16),
            pltpu.VMEM((DV_B, 2 * t), F32),
        ],
        compiler_params=_params(("parallel", "parallel", "parallel")),
        name="diff_attention",
    )(pt, p, pt, bias, diff_lambda, subln.reshape(1, DV_B))


def _rope_k_kernel(c_ref, cs_ref, o_ref):
    tmp = c_ref[...] * cs_ref[...]
    y = tmp + pltpu.roll(tmp, ROPE_C, axis=1)
    lane = lax.broadcasted_iota(jnp.int32, y.shape, 1)
    o_ref[...] = jnp.where(lane < ROPE_C, y, 0.0).astype(o_ref.dtype)


def rope_k(c, cs, bsz, s):
    tr = min(512, s)
    nb = s // tr
    cb = (Q_LORA + KV_LORA) // LANES
    return pl.pallas_call(
        _rope_k_kernel,
        grid=(bsz, nb),
        in_specs=[
            pl.BlockSpec((tr, LANES), lambda b, i: (b * nb + i, cb)),
            pl.BlockSpec((tr, LANES), lambda b, i: (i, 0)),
        ],
        out_specs=pl.BlockSpec((tr, LANES), lambda b, i: (b * nb + i, 0)),
        out_shape=jax.ShapeDtypeStruct((bsz * s, LANES), BF16),
        compiler_params=_params(("parallel", "parallel")),
        name="rope_k",
    )(c, cs)


def _mla_attn_kernel(qa_ref, qb_ref, cs_ref, kn_ref, kr_ref, vt_ref, o_ref, qs_ref, acc_ref, *, tk, n_tiles):
    scale = (NOPE_C + ROPE_C) ** -0.5 * LOG2E
    tmp = qb_ref[...].astype(F32) * cs_ref[...]
    qr = tmp + pltpu.roll(tmp, ROPE_C, axis=0)
    qs_ref[0:LANES, :] = (qa_ref[...].astype(F32) * scale).astype(BF16)
    qs_ref[LANES:2 * LANES, :] = (qr * scale).astype(BF16)

    kcs = {}

    def scores(j, cols):
        if j not in kcs:
            rows = slice(j * tk, (j + 1) * tk)
            kcs[j] = jnp.concatenate([kn_ref[rows, :], kr_ref[rows, :]], axis=-1)
        return _dot(kcs[j], qs_ref[:, cols])

    l = _online_softmax(n_tiles, scores, lambda j: vt_ref[:, j * tk:(j + 1) * tk], acc_ref, qs_ref.shape[1])
    o_ref[...] = (acc_ref[...] / l).T.astype(o_ref.dtype)


def mla_attention(qt, kn, krp, vt, cst, bsz, s):
    tq = min(MLA_TQ, s)
    tk = min(MLA_TK, s)
    nq = s // tq
    return pl.pallas_call(
        functools.partial(_mla_attn_kernel, tk=tk, n_tiles=s // tk),
        grid=(bsz, H_C, nq),
        in_specs=[
            pl.BlockSpec((LANES, tq), lambda b, h, qi: (2 * h, b * nq + qi)),
            pl.BlockSpec((LANES, tq), lambda b, h, qi: (2 * h + 1, b * nq + qi)),
            pl.BlockSpec((LANES, tq), lambda b, h, qi: (0, qi)),
            pl.BlockSpec((s, LANES), lambda b, h, qi: (b, h)),
            pl.BlockSpec((s, LANES), lambda b, h, qi: (b, 0)),
            pl.BlockSpec((DV_C, s), lambda b, h, qi: (h, b)),
        ],
        out_specs=pl.BlockSpec((tq, DV_C), lambda b, h, qi: (b * nq + qi, h)),
        out_shape=jax.ShapeDtypeStruct((bsz * s, H_C * DV_C), BF16),
        scratch_shapes=[
            pltpu.VMEM((2 * LANES, tq), BF16),
            pltpu.VMEM((DV_C, tq), F32),
        ],
        compiler_params=_params(("parallel", "parallel", "parallel")),
        name="mla_attention",
    )(qt, qt, cst, kn, krp, vt)


def _rotate_half_cols(w):
    half = w.shape[-1] // 2
    return jnp.concatenate([-w[..., half:], w[..., :half]], axis=-1)


def _prepare_weights(w_in_e, a_log, dt_bias, w_out_e, w_in_c, w_qb, w_out_c, w_mlp1, w_mlp2, w_kvb):
    g0 = 2 * QK_A + 2 * V_A
    prep = {}
    b0 = g0 + 4 * H_A
    t_last = lambda w: jnp.swapaxes(w, -1, -2)
    prep["w_e_main"] = jnp.concatenate([w_in_e[:, :, :g0], w_in_e[:, :, b0 + QK_B:b0 + 2 * QK_B]],
                                       axis=-1).astype(BF16)
    prep["w_e_t"] = t_last(jnp.concatenate([w_in_e[:, :, b0:b0 + QK_B], w_in_e[:, :, b0 + 2 * QK_B:]],
                                           axis=-1)).astype(BF16)
    prep["w_e_gate"] = jnp.pad(w_in_e[:, :, g0:g0 + 4 * H_A], ((0, 0), (0, 0), (0, LANES - 4 * H_A))).astype(BF16)
    n_even = a_log.shape[0]
    pad_row = lambda v: jnp.pad(v.reshape(n_even, 1, 2 * H_A).astype(F32),
                                ((0, 0), (0, 0), (2 * H_A, LANES - 4 * H_A)))
    prep["alog_row"] = pad_row(a_log)
    prep["dtb_row"] = pad_row(dt_bias)
    prep["w_out_e"] = w_out_e.astype(BF16)
    k_rope_w = w_in_c[:, :, Q_LORA + KV_LORA:]
    prep["w_c"] = jnp.concatenate([w_in_c, _rotate_half_cols(k_rope_w)], axis=-1).astype(BF16)
    n_odd = w_qb.shape[0]
    wq = w_qb.reshape(n_odd, Q_LORA, H_C, NOPE_C + ROPE_C)
    wq_rope = wq[..., NOPE_C:]
    prep["w_qb_t"] = t_last(jnp.concatenate([wq[..., :NOPE_C], wq_rope, _rotate_half_cols(wq_rope)], axis=-1).reshape(
        n_odd, Q_LORA, H_C * 2 * LANES)).astype(BF16)
    wkv = w_kvb.reshape(n_odd, KV_LORA, H_C, NOPE_C + DV_C)
    prep["w_kn"] = wkv[..., :NOPE_C].reshape(n_odd, KV_LORA, H_C * NOPE_C).astype(BF16)
    prep["w_v_t"] = t_last(wkv[..., NOPE_C:].reshape(n_odd, KV_LORA, H_C * DV_C)).astype(BF16)
    prep["w_out_c"] = w_out_c.astype(BF16)
    prep["w_mlp1"] = w_mlp1.astype(BF16)
    prep["w_mlp2"] = w_mlp2.astype(BF16)
    return prep


def _rope_table(s):
    inv_freq = ROPE_THETA ** (-jnp.arange(0, ROPE_C, 2, dtype=F32) / ROPE_C)
    ang = jnp.arange(s, dtype=F32)[:, None] * inv_freq[None, :]
    cos, sin = jnp.cos(ang), jnp.sin(ang)
    return jnp.concatenate([cos, cos, sin, sin], axis=-1)


def _trunk(x, pw, bias, norm_mix, norm_mlp, norm_final, conv_w, gdn_norm, diff_lambda, subln, q_norm, kv_norm):
    bsz, s, d = x.shape
    x = x.reshape(bsz * s, d)
    cs = _rope_table(s)
    cst = cs.T
    for layer in range(DEPTH):
        i = layer // 2
        if layer % 2 == 0:
            p = norm_matmul(x, 0, d, norm_mix[layer], pw["w_e_main"][i], BF16, 1024, 1024)
            pt = norm_matmul_t(x, 0, d, norm_mix[layer], pw["w_e_t"][i], BF16, 1024, 1024)
            gates = norm_matmul(x, 0, d, norm_mix[layer], pw["w_e_gate"][i], F32, 1024, LANES)
            prep = gdn_prep(p, gates, conv_w[i], pw["alog_row"][i], pw["dtb_row"][i], bsz, s)
            o_a = gdn_scan(prep, p, gdn_norm[i], bsz, s)
            lam_init = 0.8 - 0.6 * math.exp(-0.3 * layer)
            o_b = diff_attention(p, pt, bias, diff_lambda[i], subln[i], lam_init, bsz, s)
            x = matmul_res([o_a, o_b], pw["w_out_e"][i], x, 1024, 1024)
        else:
            c = norm_matmul(x, 0, d, norm_mix[layer], pw["w_c"][i], F32, 1024, pw["w_c"].shape[-1])
            qt = norm_matmul_t(c, 0, Q_LORA, q_norm[i], pw["w_qb_t"][i], BF16, 1024, 1024)
            kn = norm_matmul(c, 1, KV_LORA, kv_norm[i], pw["w_kn"][i], BF16, 1024, 1024)
            vt = norm_matmul_t(c, 1, KV_LORA, kv_norm[i], pw["w_v_t"][i], BF16, 1024, 1024)
            krp = rope_k(c, cs, bsz, s)
            o_c = mla_attention(qt, kn, krp, vt, cst, bsz, s)
            x = matmul_res([o_c], pw["w_out_c"][i], x, 1024, 1024)
        x = mlp(x, norm_mlp[layer], pw["w_mlp1"][layer], pw["w_mlp2"][layer], norm_final,
                layer == DEPTH - 1, 1024, 512)
    return x.reshape(bsz, s, d)


def kernel(x_prompt, x_sample, norm_mix, norm_mlp, norm_final, rel_bias, w_in_e, conv_w, a_log, dt_bias, gdn_norm, diff_lambda, subln, w_out_e, w_in_c, q_norm, w_qb, kv_norm, w_kvb, w_out_c, w_mlp1, w_mlp2):
    pw = _prepare_weights(w_in_e, a_log, dt_bias, w_out_e, w_in_c, w_qb, w_out_c, w_mlp1, w_mlp2, w_kvb)
    bias = bias_tiles(rel_bias)
    run = lambda x: _trunk(x, pw, bias, norm_mix, norm_mlp, norm_final, conv_w, gdn_norm, diff_lambda,
                           subln, q_norm, kv_norm)
    return (run(x_prompt), run(x_sample))
```

```python
import functools
import math

import jax
import jax.numpy as jnp
from jax import lax
from jax.experimental import pallas as pl
from jax.experimental.pallas import tpu as pltpu

F32 = jnp.float32
BF16 = jnp.bfloat16

D_MODEL = 2048
DEPTH = 4
H_A = 8
DK_A = 128
DV_A = 128
QK_A = H_A * DK_A
V_A = H_A * DV_A
H_B = 8
DH_B = 64
DV_B = 128
QK_B = H_B * 2 * DH_B
V_B = H_B * DV_B
N_BUCKETS = 32
MAX_DIST = 128
H_C = 16
Q_LORA = 512
KV_LORA = 512
NOPE_C = 128
ROPE_C = 64
DV_C = 128
ROPE_THETA = 10000.0
D_FF = 4 * D_MODEL
EPS = 1e-6

LOG2E = math.log2(math.e)
LANES = 128
BF16_SUBLANES = 16
VMEM_LIMIT = 56 * 1024 * 1024

GDN_BLOCK = 256
GDN_CHUNK = 64
GDN_HEADS_PER_STEP = 2
DIFF_TILE = 512
MLA_TQ = 1024
MLA_TK = 512
ATT_PART = 512
ONES_ROWS = BF16_SUBLANES


def _params(sem):
    return pltpu.CompilerParams(dimension_semantics=sem, vmem_limit_bytes=VMEM_LIMIT)


def _dot(a, b):
    return jnp.dot(a, b, preferred_element_type=F32)


def _dot_nt(a, b):
    return lax.dot_general(a, b, (((1,), (1,)), ((), ())), preferred_element_type=F32)


def _rms(x, g):
    ms = jnp.mean(x * x, axis=-1, keepdims=True)
    return x * lax.rsqrt(ms + EPS) * g


def _norm_matmul_kernel(x_ref, g_ref, w_ref, o_ref, h_ref):
    @pl.when(pl.program_id(1) == 0)
    def _():
        h_ref[...] = _rms(x_ref[...].astype(F32), g_ref[...]).astype(BF16)

    o_ref[...] = _dot(h_ref[...], w_ref[...]).astype(o_ref.dtype)


def norm_matmul(x, kblock, k, g, w, out_dtype, tm, tn):
    t = x.shape[0]
    n = w.shape[1]
    tm = min(tm, t)
    tn = min(tn, n)
    return pl.pallas_call(
        _norm_matmul_kernel,
        grid=(t // tm, n // tn),
        in_specs=[
            pl.BlockSpec((tm, k), lambda i, j: (i, kblock)),
            pl.BlockSpec((1, k), lambda i, j: (0, 0)),
            pl.BlockSpec((k, tn), lambda i, j: (0, j)),
        ],
        out_specs=pl.BlockSpec((tm, tn), lambda i, j: (i, j)),
        out_shape=jax.ShapeDtypeStruct((t, n), out_dtype),
        scratch_shapes=[pltpu.VMEM((tm, k), BF16)],
        compiler_params=_params(("parallel", "arbitrary")),
        name="norm_matmul",
    )(x, g.reshape(1, k).astype(F32), w)


def _norm_matmul_t_kernel(x_ref, g_ref, wt_ref, o_ref, h_ref):
    @pl.when(pl.program_id(1) == 0)
    def _():
        h_ref[...] = _rms(x_ref[...].astype(F32), g_ref[...]).astype(BF16)

    o_ref[...] = _dot_nt(wt_ref[...], h_ref[...]).astype(o_ref.dtype)


def norm_matmul_t(x, kblock, k, g, wt, out_dtype, tm, tn):
    t = x.shape[0]
    n = wt.shape[0]
    tm = min(tm, t)
    tn = min(tn, n)
    return pl.pallas_call(
        _norm_matmul_t_kernel,
        grid=(t // tm, n // tn),
        in_specs=[
            pl.BlockSpec((tm, k), lambda i, j: (i, kblock)),
            pl.BlockSpec((1, k), lambda i, j: (0, 0)),
            pl.BlockSpec((tn, k), lambda i, j: (j, 0)),
        ],
        out_specs=pl.BlockSpec((tn, tm), lambda i, j: (j, i)),
        out_shape=jax.ShapeDtypeStruct((n, t), out_dtype),
        scratch_shapes=[pltpu.VMEM((tm, k), BF16)],
        compiler_params=_params(("parallel", "arbitrary")),
        name="norm_matmul_t",
    )(x, g.reshape(1, k).astype(F32), wt)


def _matmul_res_kernel(*refs, widths):
    n = len(widths)
    w_ref, r_ref, o_ref = refs[n:]
    acc = r_ref[...]
    off = 0
    for a_ref, wd in zip(refs[:n], widths):
        acc = acc + _dot(a_ref[...], w_ref[off:off + wd, :])
        off += wd
    o_ref[...] = acc


def matmul_res(parts, w, res, tm, tn):
    t = res.shape[0]
    n = w.shape[1]
    tm = min(tm, t)
    widths = tuple(p.shape[1] for p in parts)
    return pl.pallas_call(
        functools.partial(_matmul_res_kernel, widths=widths),
        grid=(t // tm, n // tn),
        in_specs=[pl.BlockSpec((tm, wd), lambda i, j: (i, 0)) for wd in widths]
        + [
            pl.BlockSpec((sum(widths), tn), lambda i, j: (0, j)),
            pl.BlockSpec((tm, tn), lambda i, j: (i, j)),
        ],
        out_specs=pl.BlockSpec((tm, tn), lambda i, j: (i, j)),
        out_shape=jax.ShapeDtypeStruct((t, n), F32),
        compiler_params=_params(("parallel", "parallel")),
        name="matmul_res",
    )(*parts, w, res)


def _mlp_kernel(x_ref, g_ref, w1_ref, w2_ref, gf_ref, o_ref, h_ref, *, nf, final_norm):
    f = pl.program_id(1)

    @pl.when(f == 0)
    def _():
        x = x_ref[...]
        h_ref[...] = _rms(x, g_ref[...]).astype(BF16)
        o_ref[...] = x

    a = _dot(h_ref[...], w1_ref[...])
    a = jnp.square(jnp.maximum(a, 0.0)).astype(BF16)
    o_ref[...] += _dot(a, w2_ref[...])

    if final_norm:
        @pl.when(f == nf - 1)
        def _():
            o_ref[...] = _rms(o_ref[...], gf_ref[...])


def mlp(x, g, w1, w2, g_final, final_norm, tm, tf):
    t, d = x.shape
    tm = min(tm, t)
    nf = D_FF // tf
    return pl.pallas_call(
        functools.partial(_mlp_kernel, nf=nf, final_norm=final_norm),
        grid=(t // tm, nf),
        in_specs=[
            pl.BlockSpec((tm, d), lambda i, f: (i, 0)),
            pl.BlockSpec((1, d), lambda i, f: (0, 0)),
            pl.BlockSpec((d, tf), lambda i, f: (0, f)),
            pl.BlockSpec((tf, d), lambda i, f: (f, 0)),
            pl.BlockSpec((1, d), lambda i, f: (0, 0)),
        ],
        out_specs=pl.BlockSpec((tm, d), lambda i, f: (i, 0)),
        out_shape=jax.ShapeDtypeStruct((t, d), F32),
        scratch_shapes=[pltpu.VMEM((tm, d), BF16)],
        compiler_params=_params(("parallel", "arbitrary")),
        name="mlp",
    )(x, g.reshape(1, d), w1, w2, g_final.reshape(1, d))


def _softplus(x):
    return jnp.maximum(x, 0.0) + jnp.log1p(jnp.exp(-jnp.abs(x)))


def _gdn_prep_kernel(q_ref, qp_ref, qn_ref, k_ref, kp_ref, kn_ref, v_ref, vp_ref, vn_ref,
                     cwq_ref, cwk_ref, cwv_ref, gt_ref, alog_ref, dtb_ref,
                     qd_ref, kd_ref, w_ref, u_ref, in_ref, egl_ref, *, tb, cc, n):
    hp = pl.program_id(1)
    i = pl.program_id(2)
    width = GDN_HEADS_PER_STEP * LANES
    roww = lax.broadcasted_iota(jnp.int32, (tb, width), 0)

    def conv_silu(x_ref, p_ref, n_ref, cw_ref):
        x = x_ref[...].astype(F32)
        prev = jnp.where(i == 0, 0.0, p_ref[BF16_SUBLANES - 1:BF16_SUBLANES, :].astype(F32))
        nxt = jnp.where(i == n - 1, 0.0, n_ref[0:1, :].astype(F32))
        xm = jnp.where(roww == 0, prev, pltpu.roll(x, 1, axis=0))
        xp = jnp.where(roww == tb - 1, nxt, pltpu.roll(x, tb - 1, axis=0))
        cw = cw_ref[...]
        y = xm * cw[0:1, :] + x * cw[1:2, :] + xp * cw[2:3, :]
        return y * jax.nn.sigmoid(y)

    q_all = conv_silu(q_ref, qp_ref, qn_ref, cwq_ref)
    k_all = conv_silu(k_ref, kp_ref, kn_ref, cwk_ref)
    v_all = conv_silu(v_ref, vp_ref, vn_ref, cwv_ref)

    row = lax.broadcasted_iota(jnp.int32, (tb, LANES), 0)
    lane = lax.broadcasted_iota(jnp.int32, (tb, LANES), 1)
    rin = row & (cc - 1)
    gt = gt_ref[...]
    beta_all = jax.nn.sigmoid(gt)
    g_all = -jnp.exp(alog_ref[...]) * _softplus(gt + dtb_ref[...])
    pre = g_all
    suf = g_all
    s = 1
    while s < cc:
        pre = pre + jnp.where(rin >= s, pltpu.roll(pre, s, axis=0), 0.0)
        suf = suf + jnp.where(rin < cc - s, pltpu.roll(suf, tb - s, axis=0), 0.0)
        s *= 2
    tot = pre + suf - g_all
    pre_t = pre.T
    suf_t = suf.T
    sub_t = lax.broadcasted_iota(jnp.int32, (LANES, tb), 0)

    def col(x, j):
        return jnp.sum(jnp.where(lane == j, x, 0.0), axis=1, keepdims=True)

    def rowv(xt, j):
        return jnp.sum(jnp.where(sub_t == j, xt, 0.0), axis=0, keepdims=True)

    lgc = cc.bit_length() - 1
    ii = lax.broadcasted_iota(jnp.int32, (tb, tb), 0)
    jj = lax.broadcasted_iota(jnp.int32, (tb, tb), 1)
    same = (ii >> lgc) == (jj >> lgc)
    incl = (same & (ii >= jj), same & (ii <= jj))
    offdiag = ii != jj
    eye = (ii == jj).astype(F32)
    pair = (ii >> 1) == (jj >> 1)
    levels = [((ii >> (lg + 1)) == (jj >> (lg + 1))) & ((ii >> lg) != (jj >> lg)) for lg in range(1, lgc)]

    chains = []
    for e in range(GDN_HEADS_PER_STEP):
        h = hp * GDN_HEADS_PER_STEP + e
        cols = slice(e * LANES, (e + 1) * LANES)
        q, k, v = q_all[:, cols], k_all[:, cols], v_all[:, cols]
        q = q * lax.rsqrt(jnp.sum(q * q, axis=-1, keepdims=True) + EPS) * (DK_A ** -0.5)
        k = k * lax.rsqrt(jnp.sum(k * k, axis=-1, keepdims=True) + EPS)
        k16 = k.astype(BF16)
        qk = _dot_nt(q.astype(BF16), k16)
        kk = _dot_nt(k16, k16)
        for d in range(2):
            jb = d * H_A + h
            jg = 2 * H_A + d * H_A + h
            gcol, grow = (col(pre, jg), rowv(pre_t, jg)) if d == 0 else (col(suf, jg), rowv(suf_t, jg))
            beta = col(beta_all, jb)
            tcol = col(tot, jg)
            dec = jnp.where(incl[d], jnp.exp(jnp.where(incl[d], gcol - grow, 0.0)), 0.0)
            eg = jnp.exp(gcol)
            a = jnp.where(offdiag, kk * dec, 0.0) * beta
            m = qk * dec
            m = m[:, 0:LANES] + m[:, LANES:2 * LANES]
            m = m + pltpu.roll(m, cc, axis=1)
            in_ref[d, e] = m[:, 0:cc].astype(BF16)
            qd_ref[d, e] = (q * eg).astype(BF16)
            kd_ref[d, e] = (k * jnp.exp(tcol - gcol)).astype(BF16)
            eb = jnp.broadcast_to(jnp.exp(tcol), (tb, LANES))
            for c in range(tb // cc):
                egl_ref[d, e, c] = jnp.broadcast_to(eb[c * cc:c * cc + 1, :], (8, LANES))
            rhs = jnp.concatenate([v * beta, k * (beta * eg)], axis=1).astype(BF16)
            chains.append((d, e, a, rhs))

    xs = [eye - jnp.where(pair, a, 0.0) for (_, _, a, _) in chains]
    for off in levels:
        x16s = [x.astype(BF16) for x in xs]
        ys = [_dot(jnp.where(off, a, 0.0).astype(BF16), x16) for (_, _, a, _), x16 in zip(chains, x16s)]
        xs = [x - _dot(x16, y.astype(BF16)) for x, x16, y in zip(xs, x16s, ys)]
    for (d, e, _, rhs), x in zip(chains, xs):
        uw = _dot(x.astype(BF16), rhs)
        u_ref[d, e] = uw[:, 0:DV_A]
        w_ref[d, e] = uw[:, DV_A:].astype(BF16)


def gdn_prep(p, gates, conv_w, alog_row, dtb_row, bsz, s):
    tb, cc, hps = GDN_BLOCK, GDN_CHUNK, GDN_HEADS_PER_STEP
    assert tb == 2 * LANES and cc == LANES // 2
    n = s // tb
    t = bsz * s
    width = hps * LANES
    rpb = tb // BF16_SUBLANES
    last_blk = t // BF16_SUBLANES - 1

    def main(cb):
        return pl.BlockSpec((tb, width), lambda b, hp, i: (b * n + i, cb // hps + hp))

    def prev(cb):
        return pl.BlockSpec((BF16_SUBLANES, width),
                            lambda b, hp, i: (jnp.maximum((b * n + i) * rpb - 1, 0), cb // hps + hp))

    def nxt(cb):
        return pl.BlockSpec((BF16_SUBLANES, width),
                            lambda b, hp, i: (jnp.minimum((b * n + i + 1) * rpb, last_blk), cb // hps + hp))

    def cw(cb):
        return pl.BlockSpec((3, width), lambda b, hp, i: (0, cb // hps + hp))

    row_spec = pl.BlockSpec((1, LANES), lambda b, hp, i: (0, 0))
    seq_out = lambda last: pl.BlockSpec((2, None, hps, tb, last), lambda b, hp, i: (0, b, hp, i, 0))
    out_shape = [
        jax.ShapeDtypeStruct((2, bsz, H_A, s, DK_A), BF16),
        jax.ShapeDtypeStruct((2, bsz, H_A, s, DK_A), BF16),
        jax.ShapeDtypeStruct((2, bsz, H_A, s, DK_A), BF16),
        jax.ShapeDtypeStruct((2, bsz, H_A, s, DV_A), F32),
        jax.ShapeDtypeStruct((2, bsz, H_A, s, cc), BF16),
        jax.ShapeDtypeStruct((2, bsz, H_A, s // cc, 8, LANES), F32),
    ]
    out_specs = [
        seq_out(DK_A), seq_out(DK_A), seq_out(DK_A), seq_out(DV_A), seq_out(cc),
        pl.BlockSpec((2, None, hps, tb // cc, 8, LANES), lambda b, hp, i: (0, b, hp, i, 0, 0)),
    ]
    kb0, vb0 = QK_A // LANES, 2 * QK_A // LANES
    return pl.pallas_call(
        functools.partial(_gdn_prep_kernel, tb=tb, cc=cc, n=n),
        grid=(bsz, H_A // hps, n),
        in_specs=[main(0), prev(0), nxt(0), main(kb0), prev(kb0), nxt(kb0), main(vb0), prev(vb0), nxt(vb0),
                  cw(0), cw(kb0), cw(vb0),
                  pl.BlockSpec((tb, LANES), lambda b, hp, i: (b * n + i, 0)),
                  row_spec, row_spec],
        out_specs=out_specs,
        out_shape=out_shape,
        compiler_params=_params(("parallel", "parallel", "parallel")),
        name="gdn_prep",
    )(p, p, p, p, p, p, p, p, p, conv_w, conv_w, conv_w, gates, alog_row, dtb_row)


def _gdn_scan_kernel(qdf, kdf, wf, uf, inf, eglf, qdb, kdb, wb, ub, inb, eglb, z_ref, gn_ref,
                     o_ref, st_ref, acc_ref, *, tb, cc, n, hps):
    i = pl.program_id(2)
    nc = tb // cc

    @pl.when(i == 0)
    def _():
        st_ref[...] = jnp.zeros_like(st_ref)
        acc_ref[...] = jnp.zeros_like(acc_ref)

    fwd = (qdf, kdf, wf, uf, inf, eglf)
    bwd = (qdb, kdb, wb, ub, inb, eglb)
    chains = [(d, e) for e in range(hps) for d in range(2)]
    st = {ch: st_ref[ch[0], ch[1]] for ch in chains}
    for step in range(nc):
        stage = {}
        for d, e in chains:
            qd, kd, w, u, intra, egl = fwd if d == 0 else bwd
            c = step if d == 0 else nc - 1 - step
            rows = slice(c * cc, (c + 1) * cc)
            st16 = st[d, e].astype(BF16)
            stage[d, e] = (c, rows, _dot(w[e, rows, :], st16), _dot(qd[e, rows, :], st16))
        for d, e in chains:
            qd, kd, w, u, intra, egl = fwd if d == 0 else bwd
            c, rows, ws, qs = stage[d, e]
            v16 = (u[e, rows, :] - ws).astype(BF16)
            o = qs + _dot(intra[e, rows, :], v16)
            kv = lax.dot_general(kd[e, rows, :], v16, (((0,), (0,)), ((), ())), preferred_element_type=F32)
            st[d, e] = st[d, e] * egl[e, c, 0:1, :] + kv
            row0 = i * tb if d == 0 else (n - 1 - i) * tb
            r0 = pl.multiple_of(row0 + c * cc, cc)
            acc_ref[pl.ds(r0, cc), e * DV_A:(e + 1) * DV_A] += o
    for d, e in chains:
        st_ref[d, e] = st[d, e]

    @pl.when(i == n - 1)
    def _():
        for e in range(hps):
            cols = slice(e * DV_A, (e + 1) * DV_A)
            z = z_ref[:, cols].astype(F32)
            y = _rms(acc_ref[:, cols], gn_ref[...]) * (z * jax.nn.sigmoid(z))
            o_ref[:, cols] = y.astype(o_ref.dtype)


def gdn_scan(prep, p, gdn_norm, bsz, s):
    qd, kd, w, u, intra, egl = prep
    tb, cc, hps = GDN_BLOCK, GDN_CHUNK, GDN_HEADS_PER_STEP
    n = s // tb

    def seq(d, last):
        if d == 0:
            return pl.BlockSpec((None, None, hps, tb, last), lambda b, hp, i: (0, b, hp, i, 0))
        return pl.BlockSpec((None, None, hps, tb, last), lambda b, hp, i: (1, b, hp, n - 1 - i, 0))

    def egl_spec(d):
        blk = (None, None, hps, tb // cc, 8, LANES)
        if d == 0:
            return pl.BlockSpec(blk, lambda b, hp, i: (0, b, hp, i, 0, 0))
        return pl.BlockSpec(blk, lambda b, hp, i: (1, b, hp, n - 1 - i, 0, 0))

    def dir_specs(d):
        return [seq(d, DK_A), seq(d, DK_A), seq(d, DK_A), seq(d, DV_A), seq(d, cc), egl_spec(d)]

    zb0 = (2 * QK_A + V_A) // (hps * LANES)
    return pl.pallas_call(
        functools.partial(_gdn_scan_kernel, tb=tb, cc=cc, n=n, hps=hps),
        grid=(bsz, H_A // hps, n),
        in_specs=dir_specs(0) + dir_specs(1) + [
            pl.BlockSpec((s, hps * LANES), lambda b, hp, i: (b, zb0 + hp)),
            pl.BlockSpec((1, DV_A), lambda b, hp, i: (0, 0)),
        ],
        out_specs=pl.BlockSpec((s, hps * DV_A), lambda b, hp, i: (b, hp)),
        out_shape=jax.ShapeDtypeStruct((bsz * s, V_A), BF16),
        scratch_shapes=[pltpu.VMEM((2, hps, DK_A, DV_A), F32), pltpu.VMEM((s, hps * DV_A), F32)],
        compiler_params=_params(("parallel", "parallel", "arbitrary")),
        name="gdn_scan",
    )(qd, kd, w, u, intra, egl, qd, kd, w, u, intra, egl, p, gdn_norm.reshape(1, DV_A))


def _bucket_thresholds():
    nb = N_BUCKETS // 2
    max_exact = nb // 2
    out = []
    prev = 0
    for dist in range(max_exact, MAX_DIST):
        val = int(math.log(dist / max_exact) / math.log(MAX_DIST / max_exact) * (nb - max_exact))
        val = min(val, nb - 1 - max_exact)
        if val > prev:
            out.extend([dist] * (val - prev))
            prev = val
    return tuple(out)


def _bias_kernel(rb_ref, o_ref, *, t, thresholds):
    d = pl.program_id(0) - 2
    h = pl.program_id(1)
    nb = N_BUCKETS // 2
    max_exact = nb // 2
    r = lax.broadcasted_iota(jnp.int32, (t, t), 0)
    c = lax.broadcasted_iota(jnp.int32, (t, t), 1)
    rel = d * t + r - c
    dist = jnp.abs(rel)
    large = jnp.full((t, t), max_exact, jnp.int32)
    for th in thresholds:
        large = large + (dist >= th).astype(jnp.int32)
    bucket = jnp.where(rel > 0, nb, 0) + jnp.where(dist < max_exact, dist, large)
    out = jnp.zeros((t, t), F32)
    for b in range(N_BUCKETS):
        out = jnp.where(bucket == b, rb_ref[b, h], out)
    o_ref[...] = out * LOG2E


def bias_tiles(rel_bias):
    t = DIFF_TILE
    return pl.pallas_call(
        functools.partial(_bias_kernel, t=t, thresholds=_bucket_thresholds()),
        grid=(5, H_B),
        in_specs=[pl.BlockSpec(memory_space=pltpu.SMEM)],
        out_specs=pl.BlockSpec((None, None, t, t), lambda d, h: (d, h, 0, 0)),
        out_shape=jax.ShapeDtypeStruct((5, H_B, t, t), F32),
        compiler_params=_params(("parallel", "parallel")),
        name="t5_bias_tiles",
    )(rel_bias)


def _online_softmax(n_tiles, scores, vt, acc_ref, nq):
    part = min(ATT_PART, nq)
    parts = [slice(c, c + part) for c in range(0, nq, part)]
    m = [None] * len(parts)
    dv = acc_ref.shape[0] - ONES_ROWS
    sts = [scores(0, cols) for cols in parts]
    for j in range(n_tiles):
        nxt = [scores(j + 1, cols) for cols in parts] if j + 1 < n_tiles else None
        pts = []
        for i, st in enumerate(sts):
            mx = jnp.max(st, axis=0, keepdims=True)
            if j == 0:
                m[i], alpha = mx, None
            else:
                m_new = jnp.maximum(m[i], mx)
                alpha = jnp.exp2(m[i] - m_new)
                m[i] = m_new
            pts.append((alpha, jnp.exp2(st - m[i]).astype(BF16)))
        v = vt(j)
        v1 = jnp.concatenate([v, jnp.ones((ONES_ROWS, v.shape[1]), BF16)], axis=0)
        for cols, (alpha, pt16) in zip(parts, pts):
            pv = _dot(v1, pt16)
            acc_ref[:, cols] = pv if j == 0 else alpha * acc_ref[:, cols] + pv
        sts = nxt
    return acc_ref[dv:dv + 1, :]


def _diff_attn_kernel(q_ref, k_ref, vt_ref, b_ref, dl_ref, sg_ref, o_ref, qs_ref, acc_ref, *, t, n_tiles, lam_init):
    qi = pl.program_id(2)
    q = q_ref[...].astype(F32) * (DH_B ** -0.5 * LOG2E)
    sub = lax.broadcasted_iota(jnp.int32, (LANES, t), 0)
    qs_ref[:, 0:t] = jnp.where(sub < DH_B, q, 0.0).astype(BF16)
    qs_ref[:, t:2 * t] = jnp.where(sub >= DH_B, q, 0.0).astype(BF16)

    def scores(j, cols):
        b0 = cols.start % t
        width = cols.stop - cols.start
        tile = jnp.clip(j - qi, -2, 2) + 2
        return _dot(k_ref[j * t:(j + 1) * t, :], qs_ref[:, cols]) + b_ref[tile, :, b0:b0 + width]

    l = _online_softmax(n_tiles, scores, lambda j: vt_ref[:, j * t:(j + 1) * t], acc_ref, 2 * t)

    dl = dl_ref[...]
    lam = (jnp.exp(jnp.sum(dl[0:1, :] * dl[1:2, :], axis=-1, keepdims=True))
           - jnp.exp(jnp.sum(dl[2:3, :] * dl[3:4, :], axis=-1, keepdims=True)) + lam_init)
    a = acc_ref[0:DV_B, :] / l
    o = (a[:, 0:t] - lam * a[:, t:2 * t]).T
    o_ref[...] = (_rms(o, sg_ref[...]) * (1.0 - lam_init)).astype(o_ref.dtype)


def diff_attention(p, pt, bias, diff_lambda, subln, lam_init, bsz, s):
    t = min(DIFF_TILE, s)
    nq = s // t
    kb0 = (2 * QK_A + 2 * V_A) // LANES
    vb0 = QK_B // LANES
    return pl.pallas_call(
        functools.partial(_diff_attn_kernel, t=t, n_tiles=nq, lam_init=lam_init),
        grid=(bsz, H_B, nq),
        in_specs=[
            pl.BlockSpec((LANES, t), lambda b, h, qi: (h, b * nq + qi)),
            pl.BlockSpec((s, LANES), lambda b, h, qi: (b, kb0 + h)),
            pl.BlockSpec((DV_B, s), lambda b, h, qi: (vb0 + h, b)),
            pl.BlockSpec((5, None, t, t), lambda b, h, qi: (0, h, 0, 0)),
            pl.BlockSpec((4, DH_B), lambda b, h, qi: (0, 0)),
            pl.BlockSpec((1, DV_B), lambda b, h, qi: (0, 0)),
        ],
        out_specs=pl.BlockSpec((t, DV_B), lambda b, h, qi: (b * nq + qi, h)),
        out_shape=jax.ShapeDtypeStruct((bsz * s, V_B), BF16),
        scratch_shapes=[
            pltpu.VMEM((LANES, 2 * t), BF16),
            pltpu.VMEM((DV_B + ONES_ROWS, 2 * t), F32),
        ],
        compiler_params=_params(("parallel", "parallel", "parallel")),
        name="diff_attention",
    )(pt, p, pt, bias, diff_lambda, subln.reshape(1, DV_B))


def _rope_k_kernel(c_ref, cs_ref, o_ref):
    tmp = c_ref[...] * cs_ref[...]
    y = tmp + pltpu.roll(tmp, ROPE_C, axis=1)
    lane = lax.broadcasted_iota(jnp.int32, y.shape, 1)
    o_ref[...] = jnp.where(lane < ROPE_C, y, 0.0).astype(o_ref.dtype)


def rope_k(c, cs, bsz, s):
    tr = min(512, s)
    nb = s // tr
    cb = (Q_LORA + KV_LORA) // LANES
    return pl.pallas_call(
        _rope_k_kernel,
        grid=(bsz, nb),
        in_specs=[
            pl.BlockSpec((tr, LANES), lambda b, i: (b * nb + i, cb)),
            pl.BlockSpec((tr, LANES), lambda b, i: (i, 0)),
        ],
        out_specs=pl.BlockSpec((tr, LANES), lambda b, i: (b * nb + i, 0)),
        out_shape=jax.ShapeDtypeStruct((bsz * s, LANES), BF16),
        compiler_params=_params(("parallel", "parallel")),
        name="rope_k",
    )(c, cs)


def _mla_attn_kernel(qa_ref, qb_ref, cs_ref, kn_ref, kr_ref, vt_ref, o_ref, qs_ref, acc_ref, *, tk, n_tiles):
    scale = (NOPE_C + ROPE_C) ** -0.5 * LOG2E
    tmp = qb_ref[...].astype(F32) * cs_ref[...]
    qr = tmp + pltpu.roll(tmp, ROPE_C, axis=0)
    qs_ref[0:LANES, :] = (qa_ref[...].astype(F32) * scale).astype(BF16)
    qs_ref[LANES:2 * LANES, :] = (qr * scale).astype(BF16)

    kcs = {}

    def scores(j, cols):
        if j not in kcs:
            rows = slice(j * tk, (j + 1) * tk)
            kcs[j] = jnp.concatenate([kn_ref[rows, :], kr_ref[rows, :]], axis=-1)
        return _dot(kcs[j], qs_ref[:, cols])

    l = _online_softmax(n_tiles, scores, lambda j: vt_ref[:, j * tk:(j + 1) * tk], acc_ref, qs_ref.shape[1])
    o_ref[...] = (acc_ref[0:DV_C, :] / l).T.astype(o_ref.dtype)


def mla_attention(qt, kn, krp, vt, cst, bsz, s):
    tq = min(MLA_TQ, s)
    tk = min(MLA_TK, s)
    nq = s // tq
    return pl.pallas_call(
        functools.partial(_mla_attn_kernel, tk=tk, n_tiles=s // tk),
        grid=(bsz, H_C, nq),
        in_specs=[
            pl.BlockSpec((LANES, tq), lambda b, h, qi: (2 * h, b * nq + qi)),
            pl.BlockSpec((LANES, tq), lambda b, h, qi: (2 * h + 1, b * nq + qi)),
            pl.BlockSpec((LANES, tq), lambda b, h, qi: (0, qi)),
            pl.BlockSpec((s, LANES), lambda b, h, qi: (b, h)),
            pl.BlockSpec((s, LANES), lambda b, h, qi: (b, 0)),
            pl.BlockSpec((DV_C, s), lambda b, h, qi: (h, b)),
        ],
        out_specs=pl.BlockSpec((tq, DV_C), lambda b, h, qi: (b * nq + qi, h)),
        out_shape=jax.ShapeDtypeStruct((bsz * s, H_C * DV_C), BF16),
        scratch_shapes=[
            pltpu.VMEM((2 * LANES, tq), BF16),
            pltpu.VMEM((DV_C + ONES_ROWS, tq), F32),
        ],
        compiler_params=_params(("parallel", "parallel", "parallel")),
        name="mla_attention",
    )(qt, qt, cst, kn, krp, vt)


def _rotate_half_cols(w):
    half = w.shape[-1] // 2
    return jnp.concatenate([-w[..., half:], w[..., :half]], axis=-1)


def _prepare_weights(w_in_e, a_log, dt_bias, w_out_e, w_in_c, w_qb, w_out_c, w_mlp1, w_mlp2, w_kvb):
    g0 = 2 * QK_A + 2 * V_A
    prep = {}
    b0 = g0 + 4 * H_A
    t_last = lambda w: jnp.swapaxes(w, -1, -2)
    prep["w_e_main"] = jnp.concatenate([w_in_e[:, :, :g0], w_in_e[:, :, b0 + QK_B:b0 + 2 * QK_B]],
                                       axis=-1).astype(BF16)
    prep["w_e_t"] = t_last(jnp.concatenate([w_in_e[:, :, b0:b0 + QK_B], w_in_e[:, :, b0 + 2 * QK_B:]],
                                           axis=-1)).astype(BF16)
    prep["w_e_gate"] = jnp.pad(w_in_e[:, :, g0:g0 + 4 * H_A], ((0, 0), (0, 0), (0, LANES - 4 * H_A))).astype(BF16)
    n_even = a_log.shape[0]
    pad_row = lambda v: jnp.pad(v.reshape(n_even, 1, 2 * H_A).astype(F32),
                                ((0, 0), (0, 0), (2 * H_A, LANES - 4 * H_A)))
    prep["alog_row"] = pad_row(a_log)
    prep["dtb_row"] = pad_row(dt_bias)
    prep["w_out_e"] = w_out_e.astype(BF16)
    k_rope_w = w_in_c[:, :, Q_LORA + KV_LORA:]
    prep["w_c"] = jnp.concatenate([w_in_c, _rotate_half_cols(k_rope_w)], axis=-1).astype(BF16)
    n_odd = w_qb.shape[0]
    wq = w_qb.reshape(n_odd, Q_LORA, H_C, NOPE_C + ROPE_C)
    wq_rope = wq[..., NOPE_C:]
    prep["w_qb_t"] = t_last(jnp.concatenate([wq[..., :NOPE_C], wq_rope, _rotate_half_cols(wq_rope)], axis=-1).reshape(
        n_odd, Q_LORA, H_C * 2 * LANES)).astype(BF16)
    wkv = w_kvb.reshape(n_odd, KV_LORA, H_C, NOPE_C + DV_C)
    prep["w_kn"] = wkv[..., :NOPE_C].reshape(n_odd, KV_LORA, H_C * NOPE_C).astype(BF16)
    prep["w_v_t"] = t_last(wkv[..., NOPE_C:].reshape(n_odd, KV_LORA, H_C * DV_C)).astype(BF16)
    prep["w_out_c"] = w_out_c.astype(BF16)
    prep["w_mlp1"] = w_mlp1.astype(BF16)
    prep["w_mlp2"] = w_mlp2.astype(BF16)
    return prep


def _rope_table(s):
    inv_freq = ROPE_THETA ** (-jnp.arange(0, ROPE_C, 2, dtype=F32) / ROPE_C)
    ang = jnp.arange(s, dtype=F32)[:, None] * inv_freq[None, :]
    cos, sin = jnp.cos(ang), jnp.sin(ang)
    return jnp.concatenate([cos, cos, sin, sin], axis=-1)


def _trunk(x, pw, bias, norm_mix, norm_mlp, norm_final, conv_w, gdn_norm, diff_lambda, subln, q_norm, kv_norm):
    bsz, s, d = x.shape
    x = x.reshape(bsz * s, d)
    cs = _rope_table(s)
    cst = cs.T
    for layer in range(DEPTH):
        i = layer // 2
        if layer % 2 == 0:
            p = norm_matmul(x, 0, d, norm_mix[layer], pw["w_e_main"][i], BF16, 1024, 1024)
            pt = norm_matmul_t(x, 0, d, norm_mix[layer], pw["w_e_t"][i], BF16, 1024, 1024)
            gates = norm_matmul(x, 0, d, norm_mix[layer], pw["w_e_gate"][i], F32, 1024, LANES)
            prep = gdn_prep(p, gates, conv_w[i], pw["alog_row"][i], pw["dtb_row"][i], bsz, s)
            o_a = gdn_scan(prep, p, gdn_norm[i], bsz, s)
            lam_init = 0.8 - 0.6 * math.exp(-0.3 * layer)
            o_b = diff_attention(p, pt, bias, diff_lambda[i], subln[i], lam_init, bsz, s)
            x = matmul_res([o_a, o_b], pw["w_out_e"][i], x, 1024, 1024)
        else:
            c = norm_matmul(x, 0, d, norm_mix[layer], pw["w_c"][i], F32, 1024, pw["w_c"].shape[-1])
            qt = norm_matmul_t(c, 0, Q_LORA, q_norm[i], pw["w_qb_t"][i], BF16, 1024, 1024)
            kn = norm_matmul(c, 1, KV_LORA, kv_norm[i], pw["w_kn"][i], BF16, 1024, 1024)
            vt = norm_matmul_t(c, 1, KV_LORA, kv_norm[i], pw["w_v_t"][i], BF16, 1024, 1024)
            krp = rope_k(c, cs, bsz, s)
            o_c = mla_attention(qt, kn, krp, vt, cst, bsz, s)
            x = matmul_res([o_c], pw["w_out_c"][i], x, 1024, 1024)
        x = mlp(x, norm_mlp[layer], pw["w_mlp1"][layer], pw["w_mlp2"][layer], norm_final,
                layer == DEPTH - 1, 1024, 512)
    return x.reshape(bsz, s, d)


def kernel(x_prompt, x_sample, norm_mix, norm_mlp, norm_final, rel_bias, w_in_e, conv_w, a_log, dt_bias, gdn_norm, diff_lambda, subln, w_out_e, w_in_c, q_norm, w_qb, kv_norm, w_kvb, w_out_c, w_mlp1, w_mlp2):
    pw = _prepare_weights(w_in_e, a_log, dt_bias, w_out_e, w_in_c, w_qb, w_out_c, w_mlp1, w_mlp2, w_kvb)
    bias = bias_tiles(rel_bias)
    run = lambda x: _trunk(x, pw, bias, norm_mix, norm_mlp, norm_final, conv_w, gdn_norm, diff_lambda,
                           subln, q_norm, kv_norm)
    return (run(x_prompt), run(x_sample))
```

```python
import functools
import math

import jax
import jax.numpy as jnp
from jax import lax
from jax.experimental import pallas as pl
from jax.experimental.pallas import tpu as pltpu

F32 = jnp.float32
BF16 = jnp.bfloat16

D_MODEL = 2048
DEPTH = 4
H_A = 8
DK_A = 128
DV_A = 128
QK_A = H_A * DK_A
V_A = H_A * DV_A
H_B = 8
DH_B = 64
DV_B = 128
QK_B = H_B * 2 * DH_B
V_B = H_B * DV_B
N_BUCKETS = 32
MAX_DIST = 128
H_C = 16
Q_LORA = 512
KV_LORA = 512
NOPE_C = 128
ROPE_C = 64
DV_C = 128
ROPE_THETA = 10000.0
D_FF = 4 * D_MODEL
EPS = 1e-6

LOG2E = math.log2(math.e)
LANES = 128
BF16_SUBLANES = 16
VMEM_LIMIT = 56 * 1024 * 1024

GDN_BLOCK = 256
GDN_CHUNK = 64
GDN_HEADS_PER_STEP = 4
DIFF_TILE = 512
MLA_TQ = 2048
MLA_TK = 512
ATT_PART = 512
ONES_ROWS = BF16_SUBLANES


def _params(sem):
    return pltpu.CompilerParams(dimension_semantics=sem, vmem_limit_bytes=VMEM_LIMIT)


def _dot(a, b):
    return jnp.dot(a, b, preferred_element_type=F32)


def _dot_nt(a, b):
    return lax.dot_general(a, b, (((1,), (1,)), ((), ())), preferred_element_type=F32)


def _rms(x, g):
    ms = jnp.mean(x * x, axis=-1, keepdims=True)
    return x * lax.rsqrt(ms + EPS) * g


def _norm_matmul_kernel(x_ref, g_ref, w_ref, o_ref, h_ref):
    @pl.when(pl.program_id(1) == 0)
    def _():
        h_ref[...] = _rms(x_ref[...].astype(F32), g_ref[...]).astype(BF16)

    o_ref[...] = _dot(h_ref[...], w_ref[...]).astype(o_ref.dtype)


def norm_matmul(x, kblock, k, g, w, out_dtype, tm, tn):
    t = x.shape[0]
    n = w.shape[1]
    tm = min(tm, t)
    tn = min(tn, n)
    return pl.pallas_call(
        _norm_matmul_kernel,
        grid=(t // tm, n // tn),
        in_specs=[
            pl.BlockSpec((tm, k), lambda i, j: (i, kblock)),
            pl.BlockSpec((1, k), lambda i, j: (0, 0)),
            pl.BlockSpec((k, tn), lambda i, j: (0, j)),
        ],
        out_specs=pl.BlockSpec((tm, tn), lambda i, j: (i, j)),
        out_shape=jax.ShapeDtypeStruct((t, n), out_dtype),
        scratch_shapes=[pltpu.VMEM((tm, k), BF16)],
        compiler_params=_params(("parallel", "arbitrary")),
        name="norm_matmul",
    )(x, g.reshape(1, k).astype(F32), w)


def _norm_matmul_aux_kernel(x_ref, g_ref, w_ref, wa_ref, o_ref, oa_ref, h_ref):
    @pl.when(pl.program_id(1) == 0)
    def _():
        h = _rms(x_ref[...].astype(F32), g_ref[...]).astype(BF16)
        h_ref[...] = h
        oa_ref[...] = _dot(h, wa_ref[...])

    o_ref[...] = _dot(h_ref[...], w_ref[...]).astype(o_ref.dtype)


def norm_matmul_aux(x, g, w, w_aux, out_dtype, tm, tn):
    t, k = x.shape
    n = w.shape[1]
    tm = min(tm, t)
    tn = min(tn, n)
    return pl.pallas_call(
        _norm_matmul_aux_kernel,
        grid=(t // tm, n // tn),
        in_specs=[
            pl.BlockSpec((tm, k), lambda i, j: (i, 0)),
            pl.BlockSpec((1, k), lambda i, j: (0, 0)),
            pl.BlockSpec((k, tn), lambda i, j: (0, j)),
            pl.BlockSpec((k, LANES), lambda i, j: (0, 0)),
        ],
        out_specs=[pl.BlockSpec((tm, tn), lambda i, j: (i, j)),
                   pl.BlockSpec((tm, LANES), lambda i, j: (i, 0))],
        out_shape=[jax.ShapeDtypeStruct((t, n), out_dtype), jax.ShapeDtypeStruct((t, LANES), F32)],
        scratch_shapes=[pltpu.VMEM((tm, k), BF16)],
        compiler_params=_params(("parallel", "arbitrary")),
        name="norm_matmul_aux",
    )(x, g.reshape(1, k).astype(F32), w, w_aux)


def _norm_matmul_t_kernel(x_ref, g_ref, wt_ref, o_ref, h_ref):
    @pl.when(pl.program_id(1) == 0)
    def _():
        h_ref[...] = _rms(x_ref[...].astype(F32), g_ref[...]).astype(BF16)

    o_ref[...] = _dot_nt(wt_ref[...], h_ref[...]).astype(o_ref.dtype)


def norm_matmul_t(x, kblock, k, g, wt, out_dtype, tm, tn):
    t = x.shape[0]
    n = wt.shape[0]
    tm = min(tm, t)
    tn = min(tn, n)
    return pl.pallas_call(
        _norm_matmul_t_kernel,
        grid=(t // tm, n // tn),
        in_specs=[
            pl.BlockSpec((tm, k), lambda i, j: (i, kblock)),
            pl.BlockSpec((1, k), lambda i, j: (0, 0)),
            pl.BlockSpec((tn, k), lambda i, j: (j, 0)),
        ],
        out_specs=pl.BlockSpec((tn, tm), lambda i, j: (j, i)),
        out_shape=jax.ShapeDtypeStruct((n, t), out_dtype),
        scratch_shapes=[pltpu.VMEM((tm, k), BF16)],
        compiler_params=_params(("parallel", "arbitrary")),
        name="norm_matmul_t",
    )(x, g.reshape(1, k).astype(F32), wt)


def _matmul_res_kernel(*refs, widths):
    n = len(widths)
    w_ref, r_ref, o_ref = refs[n:]
    acc = r_ref[...]
    off = 0
    for a_ref, wd in zip(refs[:n], widths):
        acc = acc + _dot(a_ref[...], w_ref[off:off + wd, :])
        off += wd
    o_ref[...] = acc


def matmul_res(parts, w, res, tm, tn):
    t = res.shape[0]
    n = w.shape[1]
    tm = min(tm, t)
    widths = tuple(p.shape[1] for p in parts)
    return pl.pallas_call(
        functools.partial(_matmul_res_kernel, widths=widths),
        grid=(t // tm, n // tn),
        in_specs=[pl.BlockSpec((tm, wd), lambda i, j: (i, 0)) for wd in widths]
        + [
            pl.BlockSpec((sum(widths), tn), lambda i, j: (0, j)),
            pl.BlockSpec((tm, tn), lambda i, j: (i, j)),
        ],
        out_specs=pl.BlockSpec((tm, tn), lambda i, j: (i, j)),
        out_shape=jax.ShapeDtypeStruct((t, n), F32),
        compiler_params=_params(("parallel", "parallel")),
        name="matmul_res",
    )(*parts, w, res)


def _mlp_kernel(x_ref, g_ref, w1_ref, w2_ref, gf_ref, o_ref, h_ref, *, nf, final_norm):
    f = pl.program_id(1)

    @pl.when(f == 0)
    def _():
        x = x_ref[...]
        h_ref[...] = _rms(x, g_ref[...]).astype(BF16)
        o_ref[...] = x

    a = _dot(h_ref[...], w1_ref[...])
    a = jnp.square(jnp.maximum(a, 0.0)).astype(BF16)
    o_ref[...] += _dot(a, w2_ref[...])

    if final_norm:
        @pl.when(f == nf - 1)
        def _():
            o_ref[...] = _rms(o_ref[...], gf_ref[...])


def mlp(x, g, w1, w2, g_final, final_norm, tm, tf):
    t, d = x.shape
    tm = min(tm, t)
    nf = D_FF // tf
    return pl.pallas_call(
        functools.partial(_mlp_kernel, nf=nf, final_norm=final_norm),
        grid=(t // tm, nf),
        in_specs=[
            pl.BlockSpec((tm, d), lambda i, f: (i, 0)),
            pl.BlockSpec((1, d), lambda i, f: (0, 0)),
            pl.BlockSpec((d, tf), lambda i, f: (0, f)),
            pl.BlockSpec((tf, d), lambda i, f: (f, 0)),
            pl.BlockSpec((1, d), lambda i, f: (0, 0)),
        ],
        out_specs=pl.BlockSpec((tm, d), lambda i, f: (i, 0)),
        out_shape=jax.ShapeDtypeStruct((t, d), F32),
        scratch_shapes=[pltpu.VMEM((tm, d), BF16)],
        compiler_params=_params(("parallel", "arbitrary")),
        name="mlp",
    )(x, g.reshape(1, d), w1, w2, g_final.reshape(1, d))


def _softplus(x):
    return jnp.maximum(x, 0.0) + jnp.log1p(jnp.exp(-jnp.abs(x)))


def _gdn_prep_kernel(q_ref, qp_ref, qn_ref, k_ref, kp_ref, kn_ref, v_ref, vp_ref, vn_ref,
                     cwq_ref, cwk_ref, cwv_ref, gt_ref, alog_ref, dtb_ref,
                     qd_ref, kd_ref, w_ref, u_ref, in_ref, egl_ref, *, tb, cc, n):
    hp = pl.program_id(1)
    i = pl.program_id(2)
    width = GDN_HEADS_PER_STEP * LANES
    roww = lax.broadcasted_iota(jnp.int32, (tb, width), 0)

    def conv_silu(x_ref, p_ref, n_ref, cw_ref):
        x = x_ref[...].astype(F32)
        prev = jnp.where(i == 0, 0.0, p_ref[BF16_SUBLANES - 1:BF16_SUBLANES, :].astype(F32))
        nxt = jnp.where(i == n - 1, 0.0, n_ref[0:1, :].astype(F32))
        xm = jnp.where(roww == 0, prev, pltpu.roll(x, 1, axis=0))
        xp = jnp.where(roww == tb - 1, nxt, pltpu.roll(x, tb - 1, axis=0))
        cw = cw_ref[...]
        y = xm * cw[0:1, :] + x * cw[1:2, :] + xp * cw[2:3, :]
        return y * jax.nn.sigmoid(y)

    q_all = conv_silu(q_ref, qp_ref, qn_ref, cwq_ref)
    k_all = conv_silu(k_ref, kp_ref, kn_ref, cwk_ref)
    v_all = conv_silu(v_ref, vp_ref, vn_ref, cwv_ref)

    row = lax.broadcasted_iota(jnp.int32, (tb, LANES), 0)
    lane = lax.broadcasted_iota(jnp.int32, (tb, LANES), 1)
    rin = row & (cc - 1)
    gt = gt_ref[...]
    beta_all = jax.nn.sigmoid(gt)
    g_all = -jnp.exp(alog_ref[...]) * _softplus(gt + dtb_ref[...])
    pre = g_all
    suf = g_all
    s = 1
    while s < cc:
        pre = pre + jnp.where(rin >= s, pltpu.roll(pre, s, axis=0), 0.0)
        suf = suf + jnp.where(rin < cc - s, pltpu.roll(suf, tb - s, axis=0), 0.0)
        s *= 2
    tot = pre + suf - g_all
    pre_t = pre.T
    suf_t = suf.T
    sub_t = lax.broadcasted_iota(jnp.int32, (LANES, tb), 0)

    def col(x, j):
        return jnp.sum(jnp.where(lane == j, x, 0.0), axis=1, keepdims=True)

    def rowv(xt, j):
        return jnp.sum(jnp.where(sub_t == j, xt, 0.0), axis=0, keepdims=True)

    lgc = cc.bit_length() - 1
    ii = lax.broadcasted_iota(jnp.int32, (tb, tb), 0)
    jj = lax.broadcasted_iota(jnp.int32, (tb, tb), 1)
    same = (ii >> lgc) == (jj >> lgc)
    incl = (same & (ii >= jj), same & (ii <= jj))
    offdiag = ii != jj
    eye = (ii == jj).astype(F32)
    pair = (ii >> 1) == (jj >> 1)
    levels = [((ii >> (lg + 1)) == (jj >> (lg + 1))) & ((ii >> lg) != (jj >> lg)) for lg in range(1, lgc)]

    chains = []
    for e in range(GDN_HEADS_PER_STEP):
        h = hp * GDN_HEADS_PER_STEP + e
        cols = slice(e * LANES, (e + 1) * LANES)
        q, k, v = q_all[:, cols], k_all[:, cols], v_all[:, cols]
        q = q * lax.rsqrt(jnp.sum(q * q, axis=-1, keepdims=True) + EPS) * (DK_A ** -0.5)
        k = k * lax.rsqrt(jnp.sum(k * k, axis=-1, keepdims=True) + EPS)
        k16 = k.astype(BF16)
        qk = _dot_nt(q.astype(BF16), k16)
        kk = _dot_nt(k16, k16)
        for d in range(2):
            jb = d * H_A + h
            jg = 2 * H_A + d * H_A + h
            gcol, grow = (col(pre, jg), rowv(pre_t, jg)) if d == 0 else (col(suf, jg), rowv(suf_t, jg))
            beta = col(beta_all, jb)
            tcol = col(tot, jg)
            dec = jnp.where(incl[d], jnp.exp(jnp.where(incl[d], gcol - grow, 0.0)), 0.0)
            eg = jnp.exp(gcol)
            a = jnp.where(offdiag, kk * dec, 0.0) * beta
            m = qk * dec
            m = m[:, 0:LANES] + m[:, LANES:2 * LANES]
            m = m + pltpu.roll(m, cc, axis=1)
            in_ref[d, e] = m[:, 0:cc].astype(BF16)
            qd_ref[d, e] = (q * eg).astype(BF16)
            kd_ref[d, e] = (k * jnp.exp(tcol - gcol)).astype(BF16)
            eb = jnp.broadcast_to(jnp.exp(tcol), (tb, LANES))
            for c in range(tb // cc):
                egl_ref[d, e, c] = jnp.broadcast_to(eb[c * cc:c * cc + 1, :], (8, LANES))
            rhs = jnp.concatenate([v * beta, k * (beta * eg)], axis=1).astype(BF16)
            chains.append((d, e, a, rhs))

    xs = [eye - jnp.where(pair, a, 0.0) for (_, _, a, _) in chains]
    for off in levels:
        x16s = [x.astype(BF16) for x in xs]
        ys = [_dot(jnp.where(off, a, 0.0).astype(BF16), x16) for (_, _, a, _), x16 in zip(chains, x16s)]
        xs = [x - _dot(x16, y.astype(BF16)) for x, x16, y in zip(xs, x16s, ys)]
    for (d, e, _, rhs), x in zip(chains, xs):
        uw = _dot(x.astype(BF16), rhs)
        u_ref[d, e] = uw[:, 0:DV_A]
        w_ref[d, e] = uw[:, DV_A:].astype(BF16)


def gdn_prep(p, gates, conv_w, alog_row, dtb_row, bsz, s):
    tb, cc, hps = GDN_BLOCK, GDN_CHUNK, GDN_HEADS_PER_STEP
    assert tb == 2 * LANES and cc == LANES // 2
    n = s // tb
    t = bsz * s
    width = hps * LANES
    rpb = tb // BF16_SUBLANES
    last_blk = t // BF16_SUBLANES - 1

    def main(cb):
        return pl.BlockSpec((tb, width), lambda b, hp, i: (b * n + i, cb // hps + hp))

    def prev(cb):
        return pl.BlockSpec((BF16_SUBLANES, width),
                            lambda b, hp, i: (jnp.maximum((b * n + i) * rpb - 1, 0), cb // hps + hp))

    def nxt(cb):
        return pl.BlockSpec((BF16_SUBLANES, width),
                            lambda b, hp, i: (jnp.minimum((b * n + i + 1) * rpb, last_blk), cb // hps + hp))

    def cw(cb):
        return pl.BlockSpec((3, width), lambda b, hp, i: (0, cb // hps + hp))

    row_spec = pl.BlockSpec((1, LANES), lambda b, hp, i: (0, 0))
    seq_out = lambda last: pl.BlockSpec((2, None, hps, tb, last), lambda b, hp, i: (0, b, hp, i, 0))
    out_shape = [
        jax.ShapeDtypeStruct((2, bsz, H_A, s, DK_A), BF16),
        jax.ShapeDtypeStruct((2, bsz, H_A, s, DK_A), BF16),
        jax.ShapeDtypeStruct((2, bsz, H_A, s, DK_A), BF16),
        jax.ShapeDtypeStruct((2, bsz, H_A, s, DV_A), F32),
        jax.ShapeDtypeStruct((2, bsz, H_A, s, cc), BF16),
        jax.ShapeDtypeStruct((2, bsz, H_A, s // cc, 8, LANES), F32),
    ]
    out_specs = [
        seq_out(DK_A), seq_out(DK_A), seq_out(DK_A), seq_out(DV_A), seq_out(cc),
        pl.BlockSpec((2, None, hps, tb // cc, 8, LANES), lambda b, hp, i: (0, b, hp, i, 0, 0)),
    ]
    kb0, vb0 = QK_A // LANES, 2 * QK_A // LANES
    return pl.pallas_call(
        functools.partial(_gdn_prep_kernel, tb=tb, cc=cc, n=n),
        grid=(bsz, H_A // hps, n),
        in_specs=[main(0), prev(0), nxt(0), main(kb0), prev(kb0), nxt(kb0), main(vb0), prev(vb0), nxt(vb0),
                  cw(0), cw(kb0), cw(vb0),
                  pl.BlockSpec((tb, LANES), lambda b, hp, i: (b * n + i, 0)),
                  row_spec, row_spec],
        out_specs=out_specs,
        out_shape=out_shape,
        compiler_params=_params(("parallel", "parallel", "parallel")),
        name="gdn_prep",
    )(p, p, p, p, p, p, p, p, p, conv_w, conv_w, conv_w, gates, alog_row, dtb_row)


def _gdn_scan_kernel(qdf, kdf, wf, uf, inf, eglf, qdb, kdb, wb, ub, inb, eglb, z_ref, gn_ref,
                     o_ref, st_ref, acc_ref, *, tb, cc, n, hps):
    i = pl.program_id(2)
    nc = tb // cc

    @pl.when(i == 0)
    def _():
        st_ref[...] = jnp.zeros_like(st_ref)
        acc_ref[...] = jnp.zeros_like(acc_ref)

    fwd = (qdf, kdf, wf, uf, inf, eglf)
    bwd = (qdb, kdb, wb, ub, inb, eglb)
    chains = [(d, e) for e in range(hps) for d in range(2)]
    st = {ch: st_ref[ch[0], ch[1]] for ch in chains}
    for step in range(nc):
        stage = {}
        for d, e in chains:
            qd, kd, w, u, intra, egl = fwd if d == 0 else bwd
            c = step if d == 0 else nc - 1 - step
            rows = slice(c * cc, (c + 1) * cc)
            st16 = st[d, e].astype(BF16)
            stage[d, e] = (c, rows, _dot(w[e, rows, :], st16), _dot(qd[e, rows, :], st16))
        for d, e in chains:
            qd, kd, w, u, intra, egl = fwd if d == 0 else bwd
            c, rows, ws, qs = stage[d, e]
            v16 = (u[e, rows, :] - ws).astype(BF16)
            o = qs + _dot(intra[e, rows, :], v16)
            kv = lax.dot_general(kd[e, rows, :], v16, (((0,), (0,)), ((), ())), preferred_element_type=F32)
            st[d, e] = st[d, e] * egl[e, c, 0:1, :] + kv
            row0 = i * tb if d == 0 else (n - 1 - i) * tb
            r0 = pl.multiple_of(row0 + c * cc, cc)
            acc_ref[pl.ds(r0, cc), e * DV_A:(e + 1) * DV_A] += o
    for d, e in chains:
        st_ref[d, e] = st[d, e]

    @pl.when(i == n - 1)
    def _():
        for e in range(hps):
            cols = slice(e * DV_A, (e + 1) * DV_A)
            z = z_ref[:, cols].astype(F32)
            y = _rms(acc_ref[:, cols], gn_ref[...]) * (z * jax.nn.sigmoid(z))
            o_ref[:, cols] = y.astype(o_ref.dtype)


def gdn_scan(prep, p, gdn_norm, bsz, s):
    qd, kd, w, u, intra, egl = prep
    tb, cc, hps = GDN_BLOCK, GDN_CHUNK, GDN_HEADS_PER_STEP
    n = s // tb

    def seq(d, last):
        if d == 0:
            return pl.BlockSpec((None, None, hps, tb, last), lambda b, hp, i: (0, b, hp, i, 0))
        return pl.BlockSpec((None, None, hps, tb, last), lambda b, hp, i: (1, b, hp, n - 1 - i, 0))

    def egl_spec(d):
        blk = (None, None, hps, tb // cc, 8, LANES)
        if d == 0:
            return pl.BlockSpec(blk, lambda b, hp, i: (0, b, hp, i, 0, 0))
        return pl.BlockSpec(blk, lambda b, hp, i: (1, b, hp, n - 1 - i, 0, 0))

    def dir_specs(d):
        return [seq(d, DK_A), seq(d, DK_A), seq(d, DK_A), seq(d, DV_A), seq(d, cc), egl_spec(d)]

    zb0 = (2 * QK_A + V_A) // (hps * LANES)
    return pl.pallas_call(
        functools.partial(_gdn_scan_kernel, tb=tb, cc=cc, n=n, hps=hps),
        grid=(bsz, H_A // hps, n),
        in_specs=dir_specs(0) + dir_specs(1) + [
            pl.BlockSpec((s, hps * LANES), lambda b, hp, i: (b, zb0 + hp)),
            pl.BlockSpec((1, DV_A), lambda b, hp, i: (0, 0)),
        ],
        out_specs=pl.BlockSpec((s, hps * DV_A), lambda b, hp, i: (b, hp)),
        out_shape=jax.ShapeDtypeStruct((bsz * s, V_A), BF16),
        scratch_shapes=[pltpu.VMEM((2, hps, DK_A, DV_A), F32), pltpu.VMEM((s, hps * DV_A), F32)],
        compiler_params=_params(("parallel", "parallel", "arbitrary")),
        name="gdn_scan",
    )(qd, kd, w, u, intra, egl, qd, kd, w, u, intra, egl, p, gdn_norm.reshape(1, DV_A))


def _bucket_thresholds():
    nb = N_BUCKETS // 2
    max_exact = nb // 2
    out = []
    prev = 0
    for dist in range(max_exact, MAX_DIST):
        val = int(math.log(dist / max_exact) / math.log(MAX_DIST / max_exact) * (nb - max_exact))
        val = min(val, nb - 1 - max_exact)
        if val > prev:
            out.extend([dist] * (val - prev))
            prev = val
    return tuple(out)


def _bias_kernel(rb_ref, o_ref, *, t, thresholds):
    d = pl.program_id(0) - 2
    h = pl.program_id(1)
    nb = N_BUCKETS // 2
    max_exact = nb // 2
    r = lax.broadcasted_iota(jnp.int32, (t, t), 0)
    c = lax.broadcasted_iota(jnp.int32, (t, t), 1)
    rel = d * t + r - c
    dist = jnp.abs(rel)
    large = jnp.full((t, t), max_exact, jnp.int32)
    for th in thresholds:
        large = large + (dist >= th).astype(jnp.int32)
    bucket = jnp.where(rel > 0, nb, 0) + jnp.where(dist < max_exact, dist, large)
    out = jnp.zeros((t, t), F32)
    for b in range(N_BUCKETS):
        out = jnp.where(bucket == b, rb_ref[b, h], out)
    o_ref[...] = out * LOG2E


def bias_tiles(rel_bias):
    t = DIFF_TILE
    return pl.pallas_call(
        functools.partial(_bias_kernel, t=t, thresholds=_bucket_thresholds()),
        grid=(5, H_B),
        in_specs=[pl.BlockSpec(memory_space=pltpu.SMEM)],
        out_specs=pl.BlockSpec((None, None, t, t), lambda d, h: (d, h, 0, 0)),
        out_shape=jax.ShapeDtypeStruct((5, H_B, t, t), F32),
        compiler_params=_params(("parallel", "parallel")),
        name="t5_bias_tiles",
    )(rel_bias)


def _online_softmax(n_tiles, scores, vt, acc_ref, nq):
    part = min(ATT_PART, nq)
    parts = [slice(c, c + part) for c in range(0, nq, part)]
    m = [None] * len(parts)
    dv = acc_ref.shape[0] - ONES_ROWS
    sts = [scores(0, cols) for cols in parts]
    for j in range(n_tiles):
        nxt = [scores(j + 1, cols) for cols in parts] if j + 1 < n_tiles else None
        pts = []
        for i, st in enumerate(sts):
            mx = jnp.max(st, axis=0, keepdims=True)
            if j == 0:
                m[i], alpha = mx, None
            else:
                m_new = jnp.maximum(m[i], mx)
                alpha = jnp.exp2(m[i] - m_new)
                m[i] = m_new
            pts.append((alpha, jnp.exp2(st - m[i]).astype(BF16)))
        v = vt(j)
        v1 = jnp.concatenate([v, jnp.ones((ONES_ROWS, v.shape[1]), BF16)], axis=0)
        for cols, (alpha, pt16) in zip(parts, pts):
            pv = _dot(v1, pt16)
            acc_ref[:, cols] = pv if j == 0 else alpha * acc_ref[:, cols] + pv
        sts = nxt
    return acc_ref[dv:dv + 1, :]


def _diff_attn_kernel(q_ref, k_ref, vt_ref, b_ref, dl_ref, sg_ref, o_ref, qs_ref, acc_ref, *, t, n_tiles, lam_init):
    qi = pl.program_id(2)
    q = q_ref[...].astype(F32) * (DH_B ** -0.5 * LOG2E)
    sub = lax.broadcasted_iota(jnp.int32, (LANES, t), 0)
    qs_ref[:, 0:t] = jnp.where(sub < DH_B, q, 0.0).astype(BF16)
    qs_ref[:, t:2 * t] = jnp.where(sub >= DH_B, q, 0.0).astype(BF16)

    def scores(j, cols):
        b0 = cols.start % t
        width = cols.stop - cols.start
        tile = jnp.clip(j - qi, -2, 2) + 2
        return _dot(k_ref[j * t:(j + 1) * t, :], qs_ref[:, cols]) + b_ref[tile, :, b0:b0 + width]

    l = _online_softmax(n_tiles, scores, lambda j: vt_ref[:, j * t:(j + 1) * t], acc_ref, 2 * t)

    dl = dl_ref[...]
    lam = (jnp.exp(jnp.sum(dl[0:1, :] * dl[1:2, :], axis=-1, keepdims=True))
           - jnp.exp(jnp.sum(dl[2:3, :] * dl[3:4, :], axis=-1, keepdims=True)) + lam_init)
    a = acc_ref[0:DV_B, :] / l
    o = (a[:, 0:t] - lam * a[:, t:2 * t]).T
    o_ref[...] = (_rms(o, sg_ref[...]) * (1.0 - lam_init)).astype(o_ref.dtype)


def diff_attention(p, pt, bias, diff_lambda, subln, lam_init, bsz, s):
    t = min(DIFF_TILE, s)
    nq = s // t
    kb0 = (2 * QK_A + 2 * V_A) // LANES
    vb0 = QK_B // LANES
    return pl.pallas_call(
        functools.partial(_diff_attn_kernel, t=t, n_tiles=nq, lam_init=lam_init),
        grid=(bsz, H_B, nq),
        in_specs=[
            pl.BlockSpec((LANES, t), lambda b, h, qi: (h, b * nq + qi)),
            pl.BlockSpec((s, LANES), lambda b, h, qi: (b, kb0 + h)),
            pl.BlockSpec((DV_B, s), lambda b, h, qi: (vb0 + h, b)),
            pl.BlockSpec((5, None, t, t), lambda b, h, qi: (0, h, 0, 0)),
            pl.BlockSpec((4, DH_B), lambda b, h, qi: (0, 0)),
            pl.BlockSpec((1, DV_B), lambda b, h, qi: (0, 0)),
        ],
        out_specs=pl.BlockSpec((t, DV_B), lambda b, h, qi: (b * nq + qi, h)),
        out_shape=jax.ShapeDtypeStruct((bsz * s, V_B), BF16),
        scratch_shapes=[
            pltpu.VMEM((LANES, 2 * t), BF16),
            pltpu.VMEM((DV_B + ONES_ROWS, 2 * t), F32),
        ],
        compiler_params=_params(("parallel", "parallel", "parallel")),
        name="diff_attention",
    )(pt, p, pt, bias, diff_lambda, subln.reshape(1, DV_B))


def _rope_k_kernel(c_ref, cs_ref, o_ref):
    tmp = c_ref[...] * cs_ref[...]
    y = tmp + pltpu.roll(tmp, ROPE_C, axis=1)
    lane = lax.broadcasted_iota(jnp.int32, y.shape, 1)
    o_ref[...] = jnp.where(lane < ROPE_C, y, 0.0).astype(o_ref.dtype)


def rope_k(c, cs, bsz, s):
    tr = min(512, s)
    nb = s // tr
    cb = (Q_LORA + KV_LORA) // LANES
    return pl.pallas_call(
        _rope_k_kernel,
        grid=(bsz, nb),
        in_specs=[
            pl.BlockSpec((tr, LANES), lambda b, i: (b * nb + i, cb)),
            pl.BlockSpec((tr, LANES), lambda b, i: (i, 0)),
        ],
        out_specs=pl.BlockSpec((tr, LANES), lambda b, i: (b * nb + i, 0)),
        out_shape=jax.ShapeDtypeStruct((bsz * s, LANES), BF16),
        compiler_params=_params(("parallel", "parallel")),
        name="rope_k",
    )(c, cs)


def _mla_attn_kernel(qa_ref, qb_ref, cs_ref, kn_ref, kr_ref, vt_ref, o_ref, qs_ref, acc_ref, *, tk, n_tiles):
    scale = (NOPE_C + ROPE_C) ** -0.5 * LOG2E
    tmp = qb_ref[...].astype(F32) * cs_ref[...]
    qr = tmp + pltpu.roll(tmp, ROPE_C, axis=0)
    qs_ref[0:LANES, :] = (qa_ref[...].astype(F32) * scale).astype(BF16)
    qs_ref[LANES:2 * LANES, :] = (qr * scale).astype(BF16)

    kcs = {}

    def scores(j, cols):
        if j not in kcs:
            rows = slice(j * tk, (j + 1) * tk)
            kcs[j] = jnp.concatenate([kn_ref[rows, :], kr_ref[rows, :]], axis=-1)
        return _dot(kcs[j], qs_ref[:, cols])

    l = _online_softmax(n_tiles, scores, lambda j: vt_ref[:, j * tk:(j + 1) * tk], acc_ref, qs_ref.shape[1])
    o_ref[...] = (acc_ref[0:DV_C, :] / l).T.astype(o_ref.dtype)


def mla_attention(qt, kn, krp, vt, cst, bsz, s):
    tq = min(MLA_TQ, s)
    tk = min(MLA_TK, s)
    nq = s // tq
    return pl.pallas_call(
        functools.partial(_mla_attn_kernel, tk=tk, n_tiles=s // tk),
        grid=(bsz, H_C, nq),
        in_specs=[
            pl.BlockSpec((LANES, tq), lambda b, h, qi: (2 * h, b * nq + qi)),
            pl.BlockSpec((LANES, tq), lambda b, h, qi: (2 * h + 1, b * nq + qi)),
            pl.BlockSpec((LANES, tq), lambda b, h, qi: (0, qi)),
            pl.BlockSpec((s, LANES), lambda b, h, qi: (b, h)),
            pl.BlockSpec((s, LANES), lambda b, h, qi: (b, 0)),
            pl.BlockSpec((DV_C, s), lambda b, h, qi: (h, b)),
        ],
        out_specs=pl.BlockSpec((tq, DV_C), lambda b, h, qi: (b * nq + qi, h)),
        out_shape=jax.ShapeDtypeStruct((bsz * s, H_C * DV_C), BF16),
        scratch_shapes=[
            pltpu.VMEM((2 * LANES, tq), BF16),
            pltpu.VMEM((DV_C + ONES_ROWS, tq), F32),
        ],
        compiler_params=_params(("parallel", "parallel", "parallel")),
        name="mla_attention",
    )(qt, qt, cst, kn, krp, vt)


def _rotate_half_cols(w):
    half = w.shape[-1] // 2
    return jnp.concatenate([-w[..., half:], w[..., :half]], axis=-1)


def _prepare_weights(w_in_e, a_log, dt_bias, w_out_e, w_in_c, w_qb, w_out_c, w_mlp1, w_mlp2, w_kvb):
    g0 = 2 * QK_A + 2 * V_A
    prep = {}
    b0 = g0 + 4 * H_A
    t_last = lambda w: jnp.swapaxes(w, -1, -2)
    prep["w_e_main"] = jnp.concatenate([w_in_e[:, :, :g0], w_in_e[:, :, b0 + QK_B:b0 + 2 * QK_B]],
                                       axis=-1).astype(BF16)
    prep["w_e_t"] = t_last(jnp.concatenate([w_in_e[:, :, b0:b0 + QK_B], w_in_e[:, :, b0 + 2 * QK_B:]],
                                           axis=-1)).astype(BF16)
    prep["w_e_gate"] = jnp.pad(w_in_e[:, :, g0:g0 + 4 * H_A], ((0, 0), (0, 0), (0, LANES - 4 * H_A))).astype(BF16)
    n_even = a_log.shape[0]
    pad_row = lambda v: jnp.pad(v.reshape(n_even, 1, 2 * H_A).astype(F32),
                                ((0, 0), (0, 0), (2 * H_A, LANES - 4 * H_A)))
    prep["alog_row"] = pad_row(a_log)
    prep["dtb_row"] = pad_row(dt_bias)
    prep["w_out_e"] = w_out_e.astype(BF16)
    k_rope_w = w_in_c[:, :, Q_LORA + KV_LORA:]
    prep["w_c"] = jnp.concatenate([w_in_c, _rotate_half_cols(k_rope_w)], axis=-1).astype(BF16)
    n_odd = w_qb.shape[0]
    wq = w_qb.reshape(n_odd, Q_LORA, H_C, NOPE_C + ROPE_C)
    wq_rope = wq[..., NOPE_C:]
    prep["w_qb_t"] = t_last(jnp.concatenate([wq[..., :NOPE_C], wq_rope, _rotate_half_cols(wq_rope)], axis=-1).reshape(
        n_odd, Q_LORA, H_C * 2 * LANES)).astype(BF16)
    wkv = w_kvb.reshape(n_odd, KV_LORA, H_C, NOPE_C + DV_C)
    prep["w_kn"] = wkv[..., :NOPE_C].reshape(n_odd, KV_LORA, H_C * NOPE_C).astype(BF16)
    prep["w_v_t"] = t_last(wkv[..., NOPE_C:].reshape(n_odd, KV_LORA, H_C * DV_C)).astype(BF16)
    prep["w_out_c"] = w_out_c.astype(BF16)
    prep["w_mlp1"] = w_mlp1.astype(BF16)
    prep["w_mlp2"] = w_mlp2.astype(BF16)
    return prep


def _rope_table(s):
    inv_freq = ROPE_THETA ** (-jnp.arange(0, ROPE_C, 2, dtype=F32) / ROPE_C)
    ang = jnp.arange(s, dtype=F32)[:, None] * inv_freq[None, :]
    cos, sin = jnp.cos(ang), jnp.sin(ang)
    return jnp.concatenate([cos, cos, sin, sin], axis=-1)


def _trunk(x, pw, bias, norm_mix, norm_mlp, norm_final, conv_w, gdn_norm, diff_lambda, subln, q_norm, kv_norm):
    bsz, s, d = x.shape
    x = x.reshape(bsz * s, d)
    cs = _rope_table(s)
    cst = cs.T
    for layer in range(DEPTH):
        i = layer // 2
        if layer % 2 == 0:
            p, gates = norm_matmul_aux(x, norm_mix[layer], pw["w_e_main"][i], pw["w_e_gate"][i], BF16, 1024, 1024)
            pt = norm_matmul_t(x, 0, d, norm_mix[layer], pw["w_e_t"][i], BF16, 1024, 1024)
            prep = gdn_prep(p, gates, conv_w[i], pw["alog_row"][i], pw["dtb_row"][i], bsz, s)
            o_a = gdn_scan(prep, p, gdn_norm[i], bsz, s)
            lam_init = 0.8 - 0.6 * math.exp(-0.3 * layer)
            o_b = diff_attention(p, pt, bias, diff_lambda[i], subln[i], lam_init, bsz, s)
            x = matmul_res([o_a, o_b], pw["w_out_e"][i], x, 1024, 1024)
        else:
            c = norm_matmul(x, 0, d, norm_mix[layer], pw["w_c"][i], F32, 1024, pw["w_c"].shape[-1])
            qt = norm_matmul_t(c, 0, Q_LORA, q_norm[i], pw["w_qb_t"][i], BF16, 1024, 1024)
            kn = norm_matmul(c, 1, KV_LORA, kv_norm[i], pw["w_kn"][i], BF16, 1024, 1024)
            vt = norm_matmul_t(c, 1, KV_LORA, kv_norm[i], pw["w_v_t"][i], BF16, 1024, 1024)
            krp = rope_k(c, cs, bsz, s)
            o_c = mla_attention(qt, kn, krp, vt, cst, bsz, s)
            x = matmul_res([o_c], pw["w_out_c"][i], x, 1024, 1024)
        x = mlp(x, norm_mlp[layer], pw["w_mlp1"][layer], pw["w_mlp2"][layer], norm_final,
                layer == DEPTH - 1, 1024, 512)
    return x.reshape(bsz, s, d)


def kernel(x_prompt, x_sample, norm_mix, norm_mlp, norm_final, rel_bias, w_in_e, conv_w, a_log, dt_bias, gdn_norm, diff_lambda, subln, w_out_e, w_in_c, q_norm, w_qb, kv_norm, w_kvb, w_out_c, w_mlp1, w_mlp2):
    pw = _prepare_weights(w_in_e, a_log, dt_bias, w_out_e, w_in_c, w_qb, w_out_c, w_mlp1, w_mlp2, w_kvb)
    bias = bias_tiles(rel_bias)
    run = lambda x: _trunk(x, pw, bias, norm_mix, norm_mlp, norm_final, conv_w, gdn_norm, diff_lambda,
                           subln, q_norm, kv_norm)
    return (run(x_prompt), run(x_sample))
```

```python
import functools
import math

import jax
import jax.numpy as jnp
from jax import lax
from jax.experimental import pallas as pl
from jax.experimental.pallas import tpu as pltpu

F32 = jnp.float32
BF16 = jnp.bfloat16

D_MODEL = 2048
DEPTH = 4
H_A = 8
DK_A = 128
DV_A = 128
QK_A = H_A * DK_A
V_A = H_A * DV_A
H_B = 8
DH_B = 64
DV_B = 128
QK_B = H_B * 2 * DH_B
V_B = H_B * DV_B
N_BUCKETS = 32
MAX_DIST = 128
H_C = 16
Q_LORA = 512
KV_LORA = 512
NOPE_C = 128
ROPE_C = 64
DV_C = 128
ROPE_THETA = 10000.0
D_FF = 4 * D_MODEL
EPS = 1e-6

LOG2E = math.log2(math.e)
LANES = 128
BF16_SUBLANES = 16
VMEM_LIMIT = 56 * 1024 * 1024

GDN_BLOCK = 256
GDN_CHUNK = 64
GDN_HEADS_PER_STEP = 4
DIFF_TILE = 512
MLA_TQ = 2048
MLA_TK = 512
ATT_PART = 512
ONES_ROWS = BF16_SUBLANES


def _params(sem):
    return pltpu.CompilerParams(dimension_semantics=sem, vmem_limit_bytes=VMEM_LIMIT)


def _dot(a, b):
    return jnp.dot(a, b, preferred_element_type=F32)


def _dot_nt(a, b):
    return lax.dot_general(a, b, (((1,), (1,)), ((), ())), preferred_element_type=F32)


def _rms(x, g):
    ms = jnp.mean(x * x, axis=-1, keepdims=True)
    return x * lax.rsqrt(ms + EPS) * g


def _norm_matmul_kernel(x_ref, g_ref, w_ref, o_ref, h_ref):
    @pl.when(pl.program_id(1) == 0)
    def _():
        h_ref[...] = _rms(x_ref[...].astype(F32), g_ref[...]).astype(BF16)

    o_ref[...] = _dot(h_ref[...], w_ref[...]).astype(o_ref.dtype)


def norm_matmul(x, kblock, k, g, w, out_dtype, tm, tn):
    t = x.shape[0]
    n = w.shape[1]
    tm = min(tm, t)
    tn = min(tn, n)
    return pl.pallas_call(
        _norm_matmul_kernel,
        grid=(t // tm, n // tn),
        in_specs=[
            pl.BlockSpec((tm, k), lambda i, j: (i, kblock)),
            pl.BlockSpec((1, k), lambda i, j: (0, 0)),
            pl.BlockSpec((k, tn), lambda i, j: (0, j)),
        ],
        out_specs=pl.BlockSpec((tm, tn), lambda i, j: (i, j)),
        out_shape=jax.ShapeDtypeStruct((t, n), out_dtype),
        scratch_shapes=[pltpu.VMEM((tm, k), BF16)],
        compiler_params=_params(("parallel", "arbitrary")),
        name="norm_matmul",
    )(x, g.reshape(1, k).astype(F32), w)


def _norm_matmul_aux_kernel(x_ref, g_ref, w_ref, wa_ref, o_ref, oa_ref, h_ref):
    @pl.when(pl.program_id(1) == 0)
    def _():
        h = _rms(x_ref[...].astype(F32), g_ref[...]).astype(BF16)
        h_ref[...] = h
        oa_ref[...] = _dot(h, wa_ref[...])

    o_ref[...] = _dot(h_ref[...], w_ref[...]).astype(o_ref.dtype)


def norm_matmul_aux(x, g, w, w_aux, out_dtype, tm, tn):
    t, k = x.shape
    n = w.shape[1]
    tm = min(tm, t)
    tn = min(tn, n)
    return pl.pallas_call(
        _norm_matmul_aux_kernel,
        grid=(t // tm, n // tn),
        in_specs=[
            pl.BlockSpec((tm, k), lambda i, j: (i, 0)),
            pl.BlockSpec((1, k), lambda i, j: (0, 0)),
            pl.BlockSpec((k, tn), lambda i, j: (0, j)),
            pl.BlockSpec((k, LANES), lambda i, j: (0, 0)),
        ],
        out_specs=[pl.BlockSpec((tm, tn), lambda i, j: (i, j)),
                   pl.BlockSpec((tm, LANES), lambda i, j: (i, 0))],
        out_shape=[jax.ShapeDtypeStruct((t, n), out_dtype), jax.ShapeDtypeStruct((t, LANES), F32)],
        scratch_shapes=[pltpu.VMEM((tm, k), BF16)],
        compiler_params=_params(("parallel", "arbitrary")),
        name="norm_matmul_aux",
    )(x, g.reshape(1, k).astype(F32), w, w_aux)


def _norm_matmul_t_kernel(x_ref, g_ref, wt_ref, o_ref, h_ref):
    @pl.when(pl.program_id(1) == 0)
    def _():
        h_ref[...] = _rms(x_ref[...].astype(F32), g_ref[...]).astype(BF16)

    o_ref[...] = _dot_nt(wt_ref[...], h_ref[...]).astype(o_ref.dtype)


def norm_matmul_t(x, kblock, k, g, wt, out_dtype, tm, tn):
    t = x.shape[0]
    n = wt.shape[0]
    tm = min(tm, t)
    tn = min(tn, n)
    return pl.pallas_call(
        _norm_matmul_t_kernel,
        grid=(t // tm, n // tn),
        in_specs=[
            pl.BlockSpec((tm, k), lambda i, j: (i, kblock)),
            pl.BlockSpec((1, k), lambda i, j: (0, 0)),
            pl.BlockSpec((tn, k), lambda i, j: (j, 0)),
        ],
        out_specs=pl.BlockSpec((tn, tm), lambda i, j: (j, i)),
        out_shape=jax.ShapeDtypeStruct((n, t), out_dtype),
        scratch_shapes=[pltpu.VMEM((tm, k), BF16)],
        compiler_params=_params(("parallel", "arbitrary")),
        name="norm_matmul_t",
    )(x, g.reshape(1, k).astype(F32), wt)


def _matmul_res_kernel(*refs, widths):
    n = len(widths)
    w_ref, r_ref, o_ref = refs[n:]
    acc = r_ref[...]
    off = 0
    for a_ref, wd in zip(refs[:n], widths):
        acc = acc + _dot(a_ref[...], w_ref[off:off + wd, :])
        off += wd
    o_ref[...] = acc


def matmul_res(parts, w, res, tm, tn):
    t = res.shape[0]
    n = w.shape[1]
    tm = min(tm, t)
    widths = tuple(p.shape[1] for p in parts)
    return pl.pallas_call(
        functools.partial(_matmul_res_kernel, widths=widths),
        grid=(t // tm, n // tn),
        in_specs=[pl.BlockSpec((tm, wd), lambda i, j: (i, 0)) for wd in widths]
        + [
            pl.BlockSpec((sum(widths), tn), lambda i, j: (0, j)),
            pl.BlockSpec((tm, tn), lambda i, j: (i, j)),
        ],
        out_specs=pl.BlockSpec((tm, tn), lambda i, j: (i, j)),
        out_shape=jax.ShapeDtypeStruct((t, n), F32),
        compiler_params=_params(("parallel", "parallel")),
        name="matmul_res",
    )(*parts, w, res)


def _mlp_kernel(x_ref, g_ref, w1_ref, w2_ref, gf_ref, o_ref, h_ref, *, nf, final_norm):
    f = pl.program_id(1)

    @pl.when(f == 0)
    def _():
        x = x_ref[...]
        h_ref[...] = _rms(x, g_ref[...]).astype(BF16)
        o_ref[...] = x

    a = _dot(h_ref[...], w1_ref[...])
    a = jnp.square(jnp.maximum(a, 0.0)).astype(BF16)
    o_ref[...] += _dot(a, w2_ref[...])

    if final_norm:
        @pl.when(f == nf - 1)
        def _():
            o_ref[...] = _rms(o_ref[...], gf_ref[...])


def mlp(x, g, w1, w2, g_final, final_norm, tm, tf):
    t, d = x.shape
    tm = min(tm, t)
    nf = D_FF // tf
    return pl.pallas_call(
        functools.partial(_mlp_kernel, nf=nf, final_norm=final_norm),
        grid=(t // tm, nf),
        in_specs=[
            pl.BlockSpec((tm, d), lambda i, f: (i, 0)),
            pl.BlockSpec((1, d), lambda i, f: (0, 0)),
            pl.BlockSpec((d, tf), lambda i, f: (0, f)),
            pl.BlockSpec((tf, d), lambda i, f: (f, 0)),
            pl.BlockSpec((1, d), lambda i, f: (0, 0)),
        ],
        out_specs=pl.BlockSpec((tm, d), lambda i, f: (i, 0)),
        out_shape=jax.ShapeDtypeStruct((t, d), F32),
        scratch_shapes=[pltpu.VMEM((tm, d), BF16)],
        compiler_params=_params(("parallel", "arbitrary")),
        name="mlp",
    )(x, g.reshape(1, d), w1, w2, g_final.reshape(1, d))


def _softplus(x):
    return jnp.maximum(x, 0.0) + jnp.log1p(jnp.exp(-jnp.abs(x)))


def _gdn_prep_kernel(q_ref, qp_ref, qn_ref, k_ref, kp_ref, kn_ref, v_ref, vp_ref, vn_ref,
                     cwq_ref, cwk_ref, cwv_ref, gt_ref, alog_ref, dtb_ref,
                     qd_ref, kd_ref, w_ref, u_ref, in_ref, egl_ref, *, tb, cc, n):
    hp = pl.program_id(1)
    i = pl.program_id(2)
    width = GDN_HEADS_PER_STEP * LANES
    roww = lax.broadcasted_iota(jnp.int32, (tb, width), 0)

    def conv_silu(x_ref, p_ref, n_ref, cw_ref):
        x = x_ref[...].astype(F32)
        prev = jnp.where(i == 0, 0.0, p_ref[BF16_SUBLANES - 1:BF16_SUBLANES, :].astype(F32))
        nxt = jnp.where(i == n - 1, 0.0, n_ref[0:1, :].astype(F32))
        xm = jnp.where(roww == 0, prev, pltpu.roll(x, 1, axis=0))
        xp = jnp.where(roww == tb - 1, nxt, pltpu.roll(x, tb - 1, axis=0))
        cw = cw_ref[...]
        y = xm * cw[0:1, :] + x * cw[1:2, :] + xp * cw[2:3, :]
        return y * jax.nn.sigmoid(y)

    q_all = conv_silu(q_ref, qp_ref, qn_ref, cwq_ref)
    k_all = conv_silu(k_ref, kp_ref, kn_ref, cwk_ref)
    v_all = conv_silu(v_ref, vp_ref, vn_ref, cwv_ref)

    row = lax.broadcasted_iota(jnp.int32, (tb, LANES), 0)
    lane = lax.broadcasted_iota(jnp.int32, (tb, LANES), 1)
    rin = row & (cc - 1)
    gt = gt_ref[...]
    beta_all = jax.nn.sigmoid(gt)
    g_all = -jnp.exp(alog_ref[...]) * _softplus(gt + dtb_ref[...])
    pre = g_all
    suf = g_all
    s = 1
    while s < cc:
        pre = pre + jnp.where(rin >= s, pltpu.roll(pre, s, axis=0), 0.0)
        suf = suf + jnp.where(rin < cc - s, pltpu.roll(suf, tb - s, axis=0), 0.0)
        s *= 2
    tot = pre + suf - g_all
    pre_t = pre.T
    suf_t = suf.T
    sub_t = lax.broadcasted_iota(jnp.int32, (LANES, tb), 0)

    def col(x, j):
        return jnp.broadcast_to(jnp.sum(jnp.where(lane == j, x, 0.0), axis=1, keepdims=True), (tb, LANES))

    def wide(c):
        return jnp.concatenate([c] * (tb // LANES), axis=1)

    def rowv(xt, j):
        return jnp.sum(jnp.where(sub_t == j, xt, 0.0), axis=0, keepdims=True)

    lgc = cc.bit_length() - 1
    ii = lax.broadcasted_iota(jnp.int32, (tb, tb), 0)
    jj = lax.broadcasted_iota(jnp.int32, (tb, tb), 1)
    same = (ii >> lgc) == (jj >> lgc)
    incl = (same & (ii >= jj), same & (ii <= jj))
    offdiag = ii != jj
    eye = (ii == jj).astype(F32)
    pair = (ii >> 1) == (jj >> 1)
    levels = [((ii >> (lg + 1)) == (jj >> (lg + 1))) & ((ii >> lg) != (jj >> lg)) for lg in range(1, lgc)]

    chains = []
    for e in range(GDN_HEADS_PER_STEP):
        h = hp * GDN_HEADS_PER_STEP + e
        cols = slice(e * LANES, (e + 1) * LANES)
        q, k, v = q_all[:, cols], k_all[:, cols], v_all[:, cols]
        sumsq = lambda t: jnp.broadcast_to(jnp.sum(t * t, axis=-1, keepdims=True), (tb, LANES))
        q = q * (lax.rsqrt(sumsq(q) + EPS) * (DK_A ** -0.5))
        k = k * lax.rsqrt(sumsq(k) + EPS)
        k16 = k.astype(BF16)
        qk = _dot_nt(q.astype(BF16), k16)
        kk = _dot_nt(k16, k16)
        for d in range(2):
            jb = d * H_A + h
            jg = 2 * H_A + d * H_A + h
            gcol, grow = (col(pre, jg), rowv(pre_t, jg)) if d == 0 else (col(suf, jg), rowv(suf_t, jg))
            beta = col(beta_all, jb)
            tcol = col(tot, jg)
            dec = jnp.where(incl[d], jnp.exp(jnp.where(incl[d], wide(gcol) - grow, 0.0)), 0.0)
            eg = jnp.exp(gcol)
            a = jnp.where(offdiag, kk * dec, 0.0) * wide(beta)
            m = qk * dec
            m = m[:, 0:LANES] + m[:, LANES:2 * LANES]
            m = m + pltpu.roll(m, cc, axis=1)
            in_ref[d, e] = m[:, 0:cc].astype(BF16)
            qd_ref[d, e] = (q * eg).astype(BF16)
            kd_ref[d, e] = (k * jnp.exp(tcol - gcol)).astype(BF16)
            eb = jnp.exp(tcol)
            for c in range(tb // cc):
                egl_ref[d, e, c] = jnp.broadcast_to(eb[c * cc:c * cc + 1, :], (8, LANES))
            rhs = jnp.concatenate([v * beta, k * (beta * eg)], axis=1).astype(BF16)
            chains.append((d, e, a, rhs))

    xs = [eye - jnp.where(pair, a, 0.0) for (_, _, a, _) in chains]
    for off in levels:
        x16s = [x.astype(BF16) for x in xs]
        ys = [_dot(jnp.where(off, a, 0.0).astype(BF16), x16) for (_, _, a, _), x16 in zip(chains, x16s)]
        xs = [x - _dot(x16, y.astype(BF16)) for x, x16, y in zip(xs, x16s, ys)]
    for (d, e, _, rhs), x in zip(chains, xs):
        uw = _dot(x.astype(BF16), rhs)
        u_ref[d, e] = uw[:, 0:DV_A]
        w_ref[d, e] = uw[:, DV_A:].astype(BF16)


def gdn_prep(p, gates, conv_w, alog_row, dtb_row, bsz, s):
    tb, cc, hps = GDN_BLOCK, GDN_CHUNK, GDN_HEADS_PER_STEP
    assert tb == 2 * LANES and cc == LANES // 2
    n = s // tb
    t = bsz * s
    width = hps * LANES
    rpb = tb // BF16_SUBLANES
    last_blk = t // BF16_SUBLANES - 1

    def main(cb):
        return pl.BlockSpec((tb, width), lambda b, hp, i: (b * n + i, cb // hps + hp))

    def prev(cb):
        return pl.BlockSpec((BF16_SUBLANES, width),
                            lambda b, hp, i: (jnp.maximum((b * n + i) * rpb - 1, 0), cb // hps + hp))

    def nxt(cb):
        return pl.BlockSpec((BF16_SUBLANES, width),
                            lambda b, hp, i: (jnp.minimum((b * n + i + 1) * rpb, last_blk), cb // hps + hp))

    def cw(cb):
        return pl.BlockSpec((3, width), lambda b, hp, i: (0, cb // hps + hp))

    row_spec = pl.BlockSpec((1, LANES), lambda b, hp, i: (0, 0))
    seq_out = lambda last: pl.BlockSpec((2, None, hps, tb, last), lambda b, hp, i: (0, b, hp, i, 0))
    out_shape = [
        jax.ShapeDtypeStruct((2, bsz, H_A, s, DK_A), BF16),
        jax.ShapeDtypeStruct((2, bsz, H_A, s, DK_A), BF16),
        jax.ShapeDtypeStruct((2, bsz, H_A, s, DK_A), BF16),
        jax.ShapeDtypeStruct((2, bsz, H_A, s, DV_A), F32),
        jax.ShapeDtypeStruct((2, bsz, H_A, s, cc), BF16),
        jax.ShapeDtypeStruct((2, bsz, H_A, s // cc, 8, LANES), F32),
    ]
    out_specs = [
        seq_out(DK_A), seq_out(DK_A), seq_out(DK_A), seq_out(DV_A), seq_out(cc),
        pl.BlockSpec((2, None, hps, tb // cc, 8, LANES), lambda b, hp, i: (0, b, hp, i, 0, 0)),
    ]
    kb0, vb0 = QK_A // LANES, 2 * QK_A // LANES
    return pl.pallas_call(
        functools.partial(_gdn_prep_kernel, tb=tb, cc=cc, n=n),
        grid=(bsz, H_A // hps, n),
        in_specs=[main(0), prev(0), nxt(0), main(kb0), prev(kb0), nxt(kb0), main(vb0), prev(vb0), nxt(vb0),
                  cw(0), cw(kb0), cw(vb0),
                  pl.BlockSpec((tb, LANES), lambda b, hp, i: (b * n + i, 0)),
                  row_spec, row_spec],
        out_specs=out_specs,
        out_shape=out_shape,
        compiler_params=_params(("parallel", "parallel", "parallel")),
        name="gdn_prep",
    )(p, p, p, p, p, p, p, p, p, conv_w, conv_w, conv_w, gates, alog_row, dtb_row)


def _gdn_scan_kernel(qdf, kdf, wf, uf, inf, eglf, qdb, kdb, wb, ub, inb, eglb, z_ref, gn_ref,
                     o_ref, st_ref, acc_ref, *, tb, cc, n, hps):
    i = pl.program_id(2)
    nc = tb // cc

    @pl.when(i == 0)
    def _():
        st_ref[...] = jnp.zeros_like(st_ref)
        acc_ref[...] = jnp.zeros_like(acc_ref)

    fwd = (qdf, kdf, wf, uf, inf, eglf)
    bwd = (qdb, kdb, wb, ub, inb, eglb)
    chains = [(d, e) for e in range(hps) for d in range(2)]
    st = {ch: st_ref[ch[0], ch[1]] for ch in chains}
    for step in range(nc):
        stage = {}
        for d, e in chains:
            qd, kd, w, u, intra, egl = fwd if d == 0 else bwd
            c = step if d == 0 else nc - 1 - step
            rows = slice(c * cc, (c + 1) * cc)
            st16 = st[d, e].astype(BF16)
            stage[d, e] = (c, rows, _dot(w[e, rows, :], st16), _dot(qd[e, rows, :], st16))
        for d, e in chains:
            qd, kd, w, u, intra, egl = fwd if d == 0 else bwd
            c, rows, ws, qs = stage[d, e]
            v16 = (u[e, rows, :] - ws).astype(BF16)
            o = qs + _dot(intra[e, rows, :], v16)
            kv = lax.dot_general(kd[e, rows, :], v16, (((0,), (0,)), ((), ())), preferred_element_type=F32)
            st[d, e] = st[d, e] * egl[e, c, 0:1, :] + kv
            row0 = i * tb if d == 0 else (n - 1 - i) * tb
            r0 = pl.multiple_of(row0 + c * cc, cc)
            acc_ref[pl.ds(r0, cc), e * DV_A:(e + 1) * DV_A] += o
    for d, e in chains:
        st_ref[d, e] = st[d, e]

    @pl.when(i == n - 1)
    def _():
        for e in range(hps):
            cols = slice(e * DV_A, (e + 1) * DV_A)
            z = z_ref[:, cols].astype(F32)
            y = _rms(acc_ref[:, cols], gn_ref[...]) * (z * jax.nn.sigmoid(z))
            o_ref[:, cols] = y.astype(o_ref.dtype)


def gdn_scan(prep, p, gdn_norm, bsz, s):
    qd, kd, w, u, intra, egl = prep
    tb, cc, hps = GDN_BLOCK, GDN_CHUNK, GDN_HEADS_PER_STEP
    n = s // tb

    def seq(d, last):
        if d == 0:
            return pl.BlockSpec((None, None, hps, tb, last), lambda b, hp, i: (0, b, hp, i, 0))
        return pl.BlockSpec((None, None, hps, tb, last), lambda b, hp, i: (1, b, hp, n - 1 - i, 0))

    def egl_spec(d):
        blk = (None, None, hps, tb // cc, 8, LANES)
        if d == 0:
            return pl.BlockSpec(blk, lambda b, hp, i: (0, b, hp, i, 0, 0))
        return pl.BlockSpec(blk, lambda b, hp, i: (1, b, hp, n - 1 - i, 0, 0))

    def dir_specs(d):
        return [seq(d, DK_A), seq(d, DK_A), seq(d, DK_A), seq(d, DV_A), seq(d, cc), egl_spec(d)]

    zb0 = (2 * QK_A + V_A) // (hps * LANES)
    return pl.pallas_call(
        functools.partial(_gdn_scan_kernel, tb=tb, cc=cc, n=n, hps=hps),
        grid=(bsz, H_A // hps, n),
        in_specs=dir_specs(0) + dir_specs(1) + [
            pl.BlockSpec((s, hps * LANES), lambda b, hp, i: (b, zb0 + hp)),
            pl.BlockSpec((1, DV_A), lambda b, hp, i: (0, 0)),
        ],
        out_specs=pl.BlockSpec((s, hps * DV_A), lambda b, hp, i: (b, hp)),
        out_shape=jax.ShapeDtypeStruct((bsz * s, V_A), BF16),
        scratch_shapes=[pltpu.VMEM((2, hps, DK_A, DV_A), F32), pltpu.VMEM((s, hps * DV_A), F32)],
        compiler_params=_params(("parallel", "parallel", "arbitrary")),
        name="gdn_scan",
    )(qd, kd, w, u, intra, egl, qd, kd, w, u, intra, egl, p, gdn_norm.reshape(1, DV_A))


def _bucket_thresholds():
    nb = N_BUCKETS // 2
    max_exact = nb // 2
    out = []
    prev = 0
    for dist in range(max_exact, MAX_DIST):
        val = int(math.log(dist / max_exact) / math.log(MAX_DIST / max_exact) * (nb - max_exact))
        val = min(val, nb - 1 - max_exact)
        if val > prev:
            out.extend([dist] * (val - prev))
            prev = val
    return tuple(out)


def _bias_kernel(rb_ref, o_ref, *, t, thresholds):
    d = pl.program_id(0) - 2
    h = pl.program_id(1)
    nb = N_BUCKETS // 2
    max_exact = nb // 2
    r = lax.broadcasted_iota(jnp.int32, (t, t), 0)
    c = lax.broadcasted_iota(jnp.int32, (t, t), 1)
    rel = d * t + r - c
    dist = jnp.abs(rel)
    large = jnp.full((t, t), max_exact, jnp.int32)
    for th in thresholds:
        large = large + (dist >= th).astype(jnp.int32)
    bucket = jnp.where(rel > 0, nb, 0) + jnp.where(dist < max_exact, dist, large)
    out = jnp.zeros((t, t), F32)
    for b in range(N_BUCKETS):
        out = jnp.where(bucket == b, rb_ref[b, h], out)
    o_ref[...] = out * LOG2E


def bias_tiles(rel_bias):
    t = DIFF_TILE
    return pl.pallas_call(
        functools.partial(_bias_kernel, t=t, thresholds=_bucket_thresholds()),
        grid=(5, H_B),
        in_specs=[pl.BlockSpec(memory_space=pltpu.SMEM)],
        out_specs=pl.BlockSpec((None, None, t, t), lambda d, h: (d, h, 0, 0)),
        out_shape=jax.ShapeDtypeStruct((5, H_B, t, t), F32),
        compiler_params=_params(("parallel", "parallel")),
        name="t5_bias_tiles",
    )(rel_bias)


def _online_softmax(n_tiles, scores, vt, acc_ref, nq, shift=None):
    part = min(ATT_PART, nq)
    parts = [slice(c, c + part) for c in range(0, nq, part)]
    m = [None] * len(parts)
    dv = acc_ref.shape[0] - ONES_ROWS
    sts = [scores(0, cols) for cols in parts]
    for j in range(n_tiles):
        nxt = [scores(j + 1, cols) for cols in parts] if j + 1 < n_tiles else None
        c = shift(j) if shift is not None else None
        pts = []
        for i, st in enumerate(sts):
            mx = jnp.max(st, axis=0, keepdims=True)
            if c is not None:
                mx = mx + c
            if j == 0:
                m[i], alpha = mx, None
            else:
                m_new = jnp.maximum(m[i], mx)
                alpha = jnp.exp2(m[i] - m_new)
                m[i] = m_new
            ref = m[i] if c is None else m[i] - c
            pts.append((alpha, jnp.exp2(st - ref).astype(BF16)))
        v = vt(j)
        v1 = jnp.concatenate([v, jnp.ones((ONES_ROWS, v.shape[1]), BF16)], axis=0)
        for cols, (alpha, pt16) in zip(parts, pts):
            pv = _dot(v1, pt16)
            acc_ref[:, cols] = pv if j == 0 else alpha * acc_ref[:, cols] + pv
        sts = nxt
    return acc_ref[dv:dv + 1, :]


def _diff_attn_kernel(q_ref, k_ref, vt_ref, b_ref, dl_ref, sg_ref, o_ref, qs_ref, acc_ref, *, t, n_tiles, lam_init):
    qi = pl.program_id(2)
    q = q_ref[...].astype(F32) * (DH_B ** -0.5 * LOG2E)
    sub = lax.broadcasted_iota(jnp.int32, (LANES, t), 0)
    qs_ref[:, 0:t] = jnp.where(sub < DH_B, q, 0.0).astype(BF16)
    qs_ref[:, t:2 * t] = jnp.where(sub >= DH_B, q, 0.0).astype(BF16)

    n_near = min(3, n_tiles)

    def tile_of(r):
        return lax.rem(qi + (n_tiles - 1 + r), n_tiles)

    def start(r):
        return pl.multiple_of(tile_of(r) * t, t)

    def scores(r, cols):
        st = _dot(k_ref[pl.ds(start(r), t), :], qs_ref[:, cols])
        if r >= n_near:
            return st
        b0 = cols.start % t
        width = cols.stop - cols.start
        tile = jnp.clip(tile_of(r) - qi, -2, 2) + 2
        return st + b_ref[tile, :, b0:b0 + width]

    def shift(r):
        if r < n_near:
            return None
        return jnp.where(tile_of(r) > qi, b_ref[4, 0:1, 0:1], b_ref[0, 0:1, 0:1])

    l = _online_softmax(n_tiles, scores, lambda r: vt_ref[:, pl.ds(start(r), t)], acc_ref, 2 * t, shift)

    dl = dl_ref[...]
    lam = (jnp.exp(jnp.sum(dl[0:1, :] * dl[1:2, :], axis=-1, keepdims=True))
           - jnp.exp(jnp.sum(dl[2:3, :] * dl[3:4, :], axis=-1, keepdims=True)) + lam_init)
    a = acc_ref[0:DV_B, :] / l
    o = (a[:, 0:t] - lam * a[:, t:2 * t]).T
    o_ref[...] = (_rms(o, sg_ref[...]) * (1.0 - lam_init)).astype(o_ref.dtype)


def diff_attention(p, pt, bias, diff_lambda, subln, lam_init, bsz, s):
    t = min(DIFF_TILE, s)
    assert t >= MAX_DIST
    nq = s // t
    kb0 = (2 * QK_A + 2 * V_A) // LANES
    vb0 = QK_B // LANES
    return pl.pallas_call(
        functools.partial(_diff_attn_kernel, t=t, n_tiles=nq, lam_init=lam_init),
        grid=(bsz, H_B, nq),
        in_specs=[
            pl.BlockSpec((LANES, t), lambda b, h, qi: (h, b * nq + qi)),
            pl.BlockSpec((s, LANES), lambda b, h, qi: (b, kb0 + h)),
            pl.BlockSpec((DV_B, s), lambda b, h, qi: (vb0 + h, b)),
            pl.BlockSpec((5, None, t, t), lambda b, h, qi: (0, h, 0, 0)),
            pl.BlockSpec((4, DH_B), lambda b, h, qi: (0, 0)),
            pl.BlockSpec((1, DV_B), lambda b, h, qi: (0, 0)),
        ],
        out_specs=pl.BlockSpec((t, DV_B), lambda b, h, qi: (b * nq + qi, h)),
        out_shape=jax.ShapeDtypeStruct((bsz * s, V_B), BF16),
        scratch_shapes=[
            pltpu.VMEM((LANES, 2 * t), BF16),
            pltpu.VMEM((DV_B + ONES_ROWS, 2 * t), F32),
        ],
        compiler_params=_params(("parallel", "parallel", "parallel")),
        name="diff_attention",
    )(pt, p, pt, bias, diff_lambda, subln.reshape(1, DV_B))


def _rope_k_kernel(c_ref, cs_ref, o_ref):
    tmp = c_ref[...] * cs_ref[...]
    y = tmp + pltpu.roll(tmp, ROPE_C, axis=1)
    lane = lax.broadcasted_iota(jnp.int32, y.shape, 1)
    o_ref[...] = jnp.where(lane < ROPE_C, y, 0.0).astype(o_ref.dtype)


def rope_k(c, cs, bsz, s):
    tr = min(512, s)
    nb = s // tr
    cb = (Q_LORA + KV_LORA) // LANES
    return pl.pallas_call(
        _rope_k_kernel,
        grid=(bsz, nb),
        in_specs=[
            pl.BlockSpec((tr, LANES), lambda b, i: (b * nb + i, cb)),
            pl.BlockSpec((tr, LANES), lambda b, i: (i, 0)),
        ],
        out_specs=pl.BlockSpec((tr, LANES), lambda b, i: (b * nb + i, 0)),
        out_shape=jax.ShapeDtypeStruct((bsz * s, LANES), BF16),
        compiler_params=_params(("parallel", "parallel")),
        name="rope_k",
    )(c, cs)


def _mla_attn_kernel(qa_ref, qb_ref, cs_ref, kn_ref, kr_ref, vt_ref, o_ref, qs_ref, acc_ref, *, tk, n_tiles):
    scale = (NOPE_C + ROPE_C) ** -0.5 * LOG2E
    tmp = qb_ref[...].astype(F32) * cs_ref[...]
    qr = tmp + pltpu.roll(tmp, ROPE_C, axis=0)
    qs_ref[0:LANES, :] = (qa_ref[...].astype(F32) * scale).astype(BF16)
    qs_ref[LANES:2 * LANES, :] = (qr * scale).astype(BF16)

    kcs = {}

    def scores(j, cols):
        if j not in kcs:
            rows = slice(j * tk, (j + 1) * tk)
            kcs[j] = jnp.concatenate([kn_ref[rows, :], kr_ref[rows, :]], axis=-1)
        return _dot(kcs[j], qs_ref[:, cols])

    l = _online_softmax(n_tiles, scores, lambda j: vt_ref[:, j * tk:(j + 1) * tk], acc_ref, qs_ref.shape[1])
    o_ref[...] = (acc_ref[0:DV_C, :] / l).T.astype(o_ref.dtype)


def mla_attention(qt, kn, krp, vt, cst, bsz, s):
    tq = min(MLA_TQ, s)
    tk = min(MLA_TK, s)
    nq = s // tq
    return pl.pallas_call(
        functools.partial(_mla_attn_kernel, tk=tk, n_tiles=s // tk),
        grid=(bsz, H_C, nq),
        in_specs=[
            pl.BlockSpec((LANES, tq), lambda b, h, qi: (2 * h, b * nq + qi)),
            pl.BlockSpec((LANES, tq), lambda b, h, qi: (2 * h + 1, b * nq + qi)),
            pl.BlockSpec((LANES, tq), lambda b, h, qi: (0, qi)),
            pl.BlockSpec((s, LANES), lambda b, h, qi: (b, h)),
            pl.BlockSpec((s, LANES), lambda b, h, qi: (b, 0)),
            pl.BlockSpec((DV_C, s), lambda b, h, qi: (h, b)),
        ],
        out_specs=pl.BlockSpec((tq, DV_C), lambda b, h, qi: (b * nq + qi, h)),
        out_shape=jax.ShapeDtypeStruct((bsz * s, H_C * DV_C), BF16),
        scratch_shapes=[
            pltpu.VMEM((2 * LANES, tq), BF16),
            pltpu.VMEM((DV_C + ONES_ROWS, tq), F32),
        ],
        compiler_params=_params(("parallel", "parallel", "parallel")),
        name="mla_attention",
    )(qt, qt, cst, kn, krp, vt)


def _rotate_half_cols(w):
    half = w.shape[-1] // 2
    return jnp.concatenate([-w[..., half:], w[..., :half]], axis=-1)


def _prepare_weights(w_in_e, a_log, dt_bias, w_out_e, w_in_c, w_qb, w_out_c, w_mlp1, w_mlp2, w_kvb):
    g0 = 2 * QK_A + 2 * V_A
    prep = {}
    b0 = g0 + 4 * H_A
    t_last = lambda w: jnp.swapaxes(w, -1, -2)
    prep["w_e_main"] = jnp.concatenate([w_in_e[:, :, :g0], w_in_e[:, :, b0 + QK_B:b0 + 2 * QK_B]],
                                       axis=-1).astype(BF16)
    prep["w_e_t"] = t_last(jnp.concatenate([w_in_e[:, :, b0:b0 + QK_B], w_in_e[:, :, b0 + 2 * QK_B:]],
                                           axis=-1)).astype(BF16)
    prep["w_e_gate"] = jnp.pad(w_in_e[:, :, g0:g0 + 4 * H_A], ((0, 0), (0, 0), (0, LANES - 4 * H_A))).astype(BF16)
    n_even = a_log.shape[0]
    pad_row = lambda v: jnp.pad(v.reshape(n_even, 1, 2 * H_A).astype(F32),
                                ((0, 0), (0, 0), (2 * H_A, LANES - 4 * H_A)))
    prep["alog_row"] = pad_row(a_log)
    prep["dtb_row"] = pad_row(dt_bias)
    prep["w_out_e"] = w_out_e.astype(BF16)
    k_rope_w = w_in_c[:, :, Q_LORA + KV_LORA:]
    prep["w_c"] = jnp.concatenate([w_in_c, _rotate_half_cols(k_rope_w)], axis=-1).astype(BF16)
    n_odd = w_qb.shape[0]
    wq = w_qb.reshape(n_odd, Q_LORA, H_C, NOPE_C + ROPE_C)
    wq_rope = wq[..., NOPE_C:]
    prep["w_qb_t"] = t_last(jnp.concatenate([wq[..., :NOPE_C], wq_rope, _rotate_half_cols(wq_rope)], axis=-1).reshape(
        n_odd, Q_LORA, H_C * 2 * LANES)).astype(BF16)
    wkv = w_kvb.reshape(n_odd, KV_LORA, H_C, NOPE_C + DV_C)
    prep["w_kn"] = wkv[..., :NOPE_C].reshape(n_odd, KV_LORA, H_C * NOPE_C).astype(BF16)
    prep["w_v_t"] = t_last(wkv[..., NOPE_C:].reshape(n_odd, KV_LORA, H_C * DV_C)).astype(BF16)
    prep["w_out_c"] = w_out_c.astype(BF16)
    prep["w_mlp1"] = w_mlp1.astype(BF16)
    prep["w_mlp2"] = w_mlp2.astype(BF16)
    return prep


def _rope_table(s):
    inv_freq = ROPE_THETA ** (-jnp.arange(0, ROPE_C, 2, dtype=F32) / ROPE_C)
    ang = jnp.arange(s, dtype=F32)[:, None] * inv_freq[None, :]
    cos, sin = jnp.cos(ang), jnp.sin(ang)
    return jnp.concatenate([cos, cos, sin, sin], axis=-1)


def _trunk(x, pw, bias, norm_mix, norm_mlp, norm_final, conv_w, gdn_norm, diff_lambda, subln, q_norm, kv_norm):
    bsz, s, d = x.shape
    x = x.reshape(bsz * s, d)
    cs = _rope_table(s)
    cst = cs.T
    for layer in range(DEPTH):
        i = layer // 2
        if layer % 2 == 0:
            p, gates = norm_matmul_aux(x, norm_mix[layer], pw["w_e_main"][i], pw["w_e_gate"][i], BF16, 1024, 1024)
            pt = norm_matmul_t(x, 0, d, norm_mix[layer], pw["w_e_t"][i], BF16, 1024, 1024)
            prep = gdn_prep(p, gates, conv_w[i], pw["alog_row"][i], pw["dtb_row"][i], bsz, s)
            o_a = gdn_scan(prep, p, gdn_norm[i], bsz, s)
            lam_init = 0.8 - 0.6 * math.exp(-0.3 * layer)
            o_b = diff_attention(p, pt, bias, diff_lambda[i], subln[i], lam_init, bsz, s)
            x = matmul_res([o_a, o_b], pw["w_out_e"][i], x, 1024, 1024)
        else:
            c = norm_matmul(x, 0, d, norm_mix[layer], pw["w_c"][i], F32, 1024, pw["w_c"].shape[-1])
            qt = norm_matmul_t(c, 0, Q_LORA, q_norm[i], pw["w_qb_t"][i], BF16, 1024, 1024)
            kn = norm_matmul(c, 1, KV_LORA, kv_norm[i], pw["w_kn"][i], BF16, 1024, 1024)
            vt = norm_matmul_t(c, 1, KV_LORA, kv_norm[i], pw["w_v_t"][i], BF16, 1024, 1024)
            krp = rope_k(c, cs, bsz, s)
            o_c = mla_attention(qt, kn, krp, vt, cst, bsz, s)
            x = matmul_res([o_c], pw["w_out_c"][i], x, 1024, 1024)
        x = mlp(x, norm_mlp[layer], pw["w_mlp1"][layer], pw["w_mlp2"][layer], norm_final,
                layer == DEPTH - 1, 1024, 512)
    return x.reshape(bsz, s, d)


def kernel(x_prompt, x_sample, norm_mix, norm_mlp, norm_final, rel_bias, w_in_e, conv_w, a_log, dt_bias, gdn_norm, diff_lambda, subln, w_out_e, w_in_c, q_norm, w_qb, kv_norm, w_kvb, w_out_c, w_mlp1, w_mlp2):
    pw = _prepare_weights(w_in_e, a_log, dt_bias, w_out_e, w_in_c, w_qb, w_out_c, w_mlp1, w_mlp2, w_kvb)
    bias = bias_tiles(rel_bias)
    run = lambda x: _trunk(x, pw, bias, norm_mix, norm_mlp, norm_final, conv_w, gdn_norm, diff_lambda,
                           subln, q_norm, kv_norm)
    return (run(x_prompt), run(x_sample))
```

```python
import functools
import math

import jax
import jax.numpy as jnp
from jax import lax
from jax.experimental import pallas as pl
from jax.experimental.pallas import tpu as pltpu

F32 = jnp.float32
BF16 = jnp.bfloat16

D_MODEL = 2048
DEPTH = 4
H_A = 8
DK_A = 128
DV_A = 128
QK_A = H_A * DK_A
V_A = H_A * DV_A
H_B = 8
DH_B = 64
DV_B = 128
QK_B = H_B * 2 * DH_B
V_B = H_B * DV_B
N_BUCKETS = 32
MAX_DIST = 128
H_C = 16
Q_LORA = 512
KV_LORA = 512
NOPE_C = 128
ROPE_C = 64
DV_C = 128
ROPE_THETA = 10000.0
D_FF = 4 * D_MODEL
EPS = 1e-6

LOG2E = math.log2(math.e)
LANES = 128
BF16_SUBLANES = 16
VMEM_LIMIT = 56 * 1024 * 1024

GDN_BLOCK = 256
GDN_CHUNK = 64
GDN_HEADS_PER_STEP = 4
DIFF_TILE = 512
MLA_TQ = 2048
MLA_TK = 512
ATT_PART = 512
ONES_ROWS = BF16_SUBLANES


def _params(sem):
    return pltpu.CompilerParams(dimension_semantics=sem, vmem_limit_bytes=VMEM_LIMIT)


def _dot(a, b):
    return jnp.dot(a, b, preferred_element_type=F32)


def _dot_nt(a, b):
    return lax.dot_general(a, b, (((1,), (1,)), ((), ())), preferred_element_type=F32)


def _rms(x, g):
    ms = jnp.mean(x * x, axis=-1, keepdims=True)
    return x * lax.rsqrt(ms + EPS) * g


def _norm_matmul_kernel(x_ref, g_ref, w_ref, o_ref, h_ref):
    @pl.when(pl.program_id(1) == 0)
    def _():
        h_ref[...] = _rms(x_ref[...].astype(F32), g_ref[...]).astype(BF16)

    o_ref[...] = _dot(h_ref[...], w_ref[...]).astype(o_ref.dtype)


def norm_matmul(x, kblock, k, g, w, out_dtype, tm, tn):
    t = x.shape[0]
    n = w.shape[1]
    tm = min(tm, t)
    tn = min(tn, n)
    return pl.pallas_call(
        _norm_matmul_kernel,
        grid=(t // tm, n // tn),
        in_specs=[
            pl.BlockSpec((tm, k), lambda i, j: (i, kblock)),
            pl.BlockSpec((1, k), lambda i, j: (0, 0)),
            pl.BlockSpec((k, tn), lambda i, j: (0, j)),
        ],
        out_specs=pl.BlockSpec((tm, tn), lambda i, j: (i, j)),
        out_shape=jax.ShapeDtypeStruct((t, n), out_dtype),
        scratch_shapes=[pltpu.VMEM((tm, k), BF16)],
        compiler_params=_params(("parallel", "arbitrary")),
        name="norm_matmul",
    )(x, g.reshape(1, k).astype(F32), w)


def _norm_matmul_aux_kernel(x_ref, g_ref, w_ref, wa_ref, o_ref, oa_ref, h_ref):
    @pl.when(pl.program_id(1) == 0)
    def _():
        h = _rms(x_ref[...].astype(F32), g_ref[...]).astype(BF16)
        h_ref[...] = h
        oa_ref[...] = _dot(h, wa_ref[...])

    o_ref[...] = _dot(h_ref[...], w_ref[...]).astype(o_ref.dtype)


def norm_matmul_aux(x, g, w, w_aux, out_dtype, tm, tn):
    t, k = x.shape
    n = w.shape[1]
    tm = min(tm, t)
    tn = min(tn, n)
    return pl.pallas_call(
        _norm_matmul_aux_kernel,
        grid=(t // tm, n // tn),
        in_specs=[
            pl.BlockSpec((tm, k), lambda i, j: (i, 0)),
            pl.BlockSpec((1, k), lambda i, j: (0, 0)),
            pl.BlockSpec((k, tn), lambda i, j: (0, j)),
            pl.BlockSpec((k, LANES), lambda i, j: (0, 0)),
        ],
        out_specs=[pl.BlockSpec((tm, tn), lambda i, j: (i, j)),
                   pl.BlockSpec((tm, LANES), lambda i, j: (i, 0))],
        out_shape=[jax.ShapeDtypeStruct((t, n), out_dtype), jax.ShapeDtypeStruct((t, LANES), F32)],
        scratch_shapes=[pltpu.VMEM((tm, k), BF16)],
        compiler_params=_params(("parallel", "arbitrary")),
        name="norm_matmul_aux",
    )(x, g.reshape(1, k).astype(F32), w, w_aux)


def _norm_matmul_t_kernel(x_ref, g_ref, wt_ref, o_ref, h_ref):
    @pl.when(pl.program_id(1) == 0)
    def _():
        h_ref[...] = _rms(x_ref[...].astype(F32), g_ref[...]).astype(BF16)

    o_ref[...] = _dot_nt(wt_ref[...], h_ref[...]).astype(o_ref.dtype)


def norm_matmul_t(x, kblock, k, g, wt, out_dtype, tm, tn):
    t = x.shape[0]
    n = wt.shape[0]
    tm = min(tm, t)
    tn = min(tn, n)
    return pl.pallas_call(
        _norm_matmul_t_kernel,
        grid=(t // tm, n // tn),
        in_specs=[
            pl.BlockSpec((tm, k), lambda i, j: (i, kblock)),
            pl.BlockSpec((1, k), lambda i, j: (0, 0)),
            pl.BlockSpec((tn, k), lambda i, j: (j, 0)),
        ],
        out_specs=pl.BlockSpec((tn, tm), lambda i, j: (j, i)),
        out_shape=jax.ShapeDtypeStruct((n, t), out_dtype),
        scratch_shapes=[pltpu.VMEM((tm, k), BF16)],
        compiler_params=_params(("parallel", "arbitrary")),
        name="norm_matmul_t",
    )(x, g.reshape(1, k).astype(F32), wt)


def _matmul_res_kernel(*refs, widths):
    n = len(widths)
    w_ref, r_ref, o_ref = refs[n:]
    acc = r_ref[...]
    off = 0
    for a_ref, wd in zip(refs[:n], widths):
        acc = acc + _dot(a_ref[...], w_ref[off:off + wd, :])
        off += wd
    o_ref[...] = acc


def matmul_res(parts, w, res, tm, tn):
    t = res.shape[0]
    n = w.shape[1]
    tm = min(tm, t)
    widths = tuple(p.shape[1] for p in parts)
    return pl.pallas_call(
        functools.partial(_matmul_res_kernel, widths=widths),
        grid=(t // tm, n // tn),
        in_specs=[pl.BlockSpec((tm, wd), lambda i, j: (i, 0)) for wd in widths]
        + [
            pl.BlockSpec((sum(widths), tn), lambda i, j: (0, j)),
            pl.BlockSpec((tm, tn), lambda i, j: (i, j)),
        ],
        out_specs=pl.BlockSpec((tm, tn), lambda i, j: (i, j)),
        out_shape=jax.ShapeDtypeStruct((t, n), F32),
        compiler_params=_params(("parallel", "parallel")),
        name="matmul_res",
    )(*parts, w, res)


def _mlp_kernel(x_ref, g_ref, w1_ref, w2_ref, gf_ref, o_ref, h_ref, *, nf, final_norm):
    f = pl.program_id(1)

    @pl.when(f == 0)
    def _():
        x = x_ref[...]
        h_ref[...] = _rms(x, g_ref[...]).astype(BF16)
        o_ref[...] = x

    a = _dot(h_ref[...], w1_ref[...])
    a = jnp.square(jnp.maximum(a, 0.0)).astype(BF16)
    o_ref[...] += _dot(a, w2_ref[...])

    if final_norm:
        @pl.when(f == nf - 1)
        def _():
            o_ref[...] = _rms(o_ref[...], gf_ref[...])


def mlp(x, g, w1, w2, g_final, final_norm, tm, tf):
    t, d = x.shape
    tm = min(tm, t)
    nf = D_FF // tf
    return pl.pallas_call(
        functools.partial(_mlp_kernel, nf=nf, final_norm=final_norm),
        grid=(t // tm, nf),
        in_specs=[
            pl.BlockSpec((tm, d), lambda i, f: (i, 0)),
            pl.BlockSpec((1, d), lambda i, f: (0, 0)),
            pl.BlockSpec((d, tf), lambda i, f: (0, f)),
            pl.BlockSpec((tf, d), lambda i, f: (f, 0)),
            pl.BlockSpec((1, d), lambda i, f: (0, 0)),
        ],
        out_specs=pl.BlockSpec((tm, d), lambda i, f: (i, 0)),
        out_shape=jax.ShapeDtypeStruct((t, d), F32),
        scratch_shapes=[pltpu.VMEM((tm, d), BF16)],
        compiler_params=_params(("parallel", "arbitrary")),
        name="mlp",
    )(x, g.reshape(1, d), w1, w2, g_final.reshape(1, d))


def _softplus(x):
    return jnp.maximum(x, 0.0) + jnp.log1p(jnp.exp(-jnp.abs(x)))


def _gdn_prep_kernel(q_ref, qp_ref, qn_ref, k_ref, kp_ref, kn_ref, v_ref, vp_ref, vn_ref,
                     cwq_ref, cwk_ref, cwv_ref, gt_ref, alog_ref, dtb_ref,
                     qd_ref, kd_ref, w_ref, u_ref, in_ref, egl_ref, *, tb, cc, n):
    hp = pl.program_id(1)
    i = pl.program_id(2)
    width = GDN_HEADS_PER_STEP * LANES
    roww = lax.broadcasted_iota(jnp.int32, (tb, width), 0)

    def conv_silu(x_ref, p_ref, n_ref, cw_ref):
        x = x_ref[...].astype(F32)
        prev = jnp.where(i == 0, 0.0, p_ref[BF16_SUBLANES - 1:BF16_SUBLANES, :].astype(F32))
        nxt = jnp.where(i == n - 1, 0.0, n_ref[0:1, :].astype(F32))
        xm = jnp.where(roww == 0, prev, pltpu.roll(x, 1, axis=0))
        xp = jnp.where(roww == tb - 1, nxt, pltpu.roll(x, tb - 1, axis=0))
        cw = cw_ref[...]
        y = xm * cw[0:1, :] + x * cw[1:2, :] + xp * cw[2:3, :]
        return y * jax.nn.sigmoid(y)

    q_all = conv_silu(q_ref, qp_ref, qn_ref, cwq_ref)
    k_all = conv_silu(k_ref, kp_ref, kn_ref, cwk_ref)
    v_all = conv_silu(v_ref, vp_ref, vn_ref, cwv_ref)

    row = lax.broadcasted_iota(jnp.int32, (tb, LANES), 0)
    lane = lax.broadcasted_iota(jnp.int32, (tb, LANES), 1)
    rin = row & (cc - 1)
    gt = gt_ref[...]
    beta_all = jax.nn.sigmoid(gt)
    g_all = -jnp.exp(alog_ref[...]) * _softplus(gt + dtb_ref[...])
    pre = g_all
    suf = g_all
    s = 1
    while s < cc:
        pre = pre + jnp.where(rin >= s, pltpu.roll(pre, s, axis=0), 0.0)
        suf = suf + jnp.where(rin < cc - s, pltpu.roll(suf, tb - s, axis=0), 0.0)
        s *= 2
    tot = pre + suf - g_all
    pre_t = pre.T
    suf_t = suf.T
    sub_t = lax.broadcasted_iota(jnp.int32, (LANES, tb), 0)

    def col(x, j):
        return jnp.broadcast_to(jnp.sum(jnp.where(lane == j, x, 0.0), axis=1, keepdims=True), (tb, LANES))

    def wide(c):
        return jnp.concatenate([c] * (tb // LANES), axis=1)

    def rowv(xt, j):
        return jnp.sum(jnp.where(sub_t == j, xt, 0.0), axis=0, keepdims=True)

    lgc = cc.bit_length() - 1
    ii = lax.broadcasted_iota(jnp.int32, (tb, tb), 0)
    jj = lax.broadcasted_iota(jnp.int32, (tb, tb), 1)
    same = (ii >> lgc) == (jj >> lgc)
    incl = (same & (ii >= jj), same & (ii <= jj))
    offdiag = ii != jj
    eye = (ii == jj).astype(F32)
    pair = (ii >> 1) == (jj >> 1)
    levels = [((ii >> (lg + 1)) == (jj >> (lg + 1))) & ((ii >> lg) != (jj >> lg)) for lg in range(1, lgc)]

    chains = []
    for e in range(GDN_HEADS_PER_STEP):
        h = hp * GDN_HEADS_PER_STEP + e
        cols = slice(e * LANES, (e + 1) * LANES)
        q, k, v = q_all[:, cols], k_all[:, cols], v_all[:, cols]
        sumsq = lambda t: jnp.broadcast_to(jnp.sum(t * t, axis=-1, keepdims=True), (tb, LANES))
        q = q * (lax.rsqrt(sumsq(q) + EPS) * (DK_A ** -0.5))
        k = k * lax.rsqrt(sumsq(k) + EPS)
        k16 = k.astype(BF16)
        qk = _dot_nt(q.astype(BF16), k16)
        kk = _dot_nt(k16, k16)
        for d in range(2):
            jb = d * H_A + h
            jg = 2 * H_A + d * H_A + h
            gcol, grow = (col(pre, jg), rowv(pre_t, jg)) if d == 0 else (col(suf, jg), rowv(suf_t, jg))
            beta = col(beta_all, jb)
            tcol = col(tot, jg)
            dec = jnp.where(incl[d], jnp.exp(jnp.where(incl[d], wide(gcol) - grow, 0.0)), 0.0)
            eg = jnp.exp(gcol)
            a = jnp.where(offdiag, kk * dec, 0.0) * wide(beta)
            m = qk * dec
            in_ref[d, e] = (m[:, 0:LANES] + m[:, LANES:2 * LANES]).astype(BF16)
            qd_ref[d, e] = (q * eg).astype(BF16)
            kd_ref[d, e] = (k * jnp.exp(tcol - gcol)).astype(BF16)
            eb = jnp.exp(tcol)
            for c in range(tb // cc):
                egl_ref[d, e, c] = jnp.broadcast_to(eb[c * cc:c * cc + 1, :], (8, LANES))
            rhs = jnp.concatenate([v * beta, k * (beta * eg)], axis=1).astype(BF16)
            chains.append((d, e, a, rhs))

    xs = [eye - jnp.where(pair, a, 0.0) for (_, _, a, _) in chains]
    for off in levels:
        x16s = [x.astype(BF16) for x in xs]
        ys = [_dot(jnp.where(off, a, 0.0).astype(BF16), x16) for (_, _, a, _), x16 in zip(chains, x16s)]
        xs = [x - _dot(x16, y.astype(BF16)) for x, x16, y in zip(xs, x16s, ys)]
    for (d, e, _, rhs), x in zip(chains, xs):
        uw = _dot(x.astype(BF16), rhs)
        u_ref[d, e] = uw[:, 0:DV_A]
        w_ref[d, e] = uw[:, DV_A:].astype(BF16)


def gdn_prep(p, gates, conv_w, alog_row, dtb_row, bsz, s):
    tb, cc, hps = GDN_BLOCK, GDN_CHUNK, GDN_HEADS_PER_STEP
    assert tb == 2 * LANES and cc == LANES // 2
    n = s // tb
    t = bsz * s
    width = hps * LANES
    rpb = tb // BF16_SUBLANES
    last_blk = t // BF16_SUBLANES - 1

    def main(cb):
        return pl.BlockSpec((tb, width), lambda b, hp, i: (b * n + i, cb // hps + hp))

    def prev(cb):
        return pl.BlockSpec((BF16_SUBLANES, width),
                            lambda b, hp, i: (jnp.maximum((b * n + i) * rpb - 1, 0), cb // hps + hp))

    def nxt(cb):
        return pl.BlockSpec((BF16_SUBLANES, width),
                            lambda b, hp, i: (jnp.minimum((b * n + i + 1) * rpb, last_blk), cb // hps + hp))

    def cw(cb):
        return pl.BlockSpec((3, width), lambda b, hp, i: (0, cb // hps + hp))

    row_spec = pl.BlockSpec((1, LANES), lambda b, hp, i: (0, 0))
    seq_out = lambda last: pl.BlockSpec((2, None, hps, tb, last), lambda b, hp, i: (0, b, hp, i, 0))
    out_shape = [
        jax.ShapeDtypeStruct((2, bsz, H_A, s, DK_A), BF16),
        jax.ShapeDtypeStruct((2, bsz, H_A, s, DK_A), BF16),
        jax.ShapeDtypeStruct((2, bsz, H_A, s, DK_A), BF16),
        jax.ShapeDtypeStruct((2, bsz, H_A, s, DV_A), F32),
        jax.ShapeDtypeStruct((2, bsz, H_A, s, LANES), BF16),
        jax.ShapeDtypeStruct((2, bsz, H_A, s // cc, 8, LANES), F32),
    ]
    out_specs = [
        seq_out(DK_A), seq_out(DK_A), seq_out(DK_A), seq_out(DV_A), seq_out(LANES),
        pl.BlockSpec((2, None, hps, tb // cc, 8, LANES), lambda b, hp, i: (0, b, hp, i, 0, 0)),
    ]
    kb0, vb0 = QK_A // LANES, 2 * QK_A // LANES
    return pl.pallas_call(
        functools.partial(_gdn_prep_kernel, tb=tb, cc=cc, n=n),
        grid=(bsz, H_A // hps, n),
        in_specs=[main(0), prev(0), nxt(0), main(kb0), prev(kb0), nxt(kb0), main(vb0), prev(vb0), nxt(vb0),
                  cw(0), cw(kb0), cw(vb0),
                  pl.BlockSpec((tb, LANES), lambda b, hp, i: (b * n + i, 0)),
                  row_spec, row_spec],
        out_specs=out_specs,
        out_shape=out_shape,
        compiler_params=_params(("parallel", "parallel", "parallel")),
        name="gdn_prep",
    )(p, p, p, p, p, p, p, p, p, conv_w, conv_w, conv_w, gates, alog_row, dtb_row)


def _gdn_scan_kernel(qdf, kdf, wf, uf, inf, eglf, qdb, kdb, wb, ub, inb, eglb, z_ref, gn_ref,
                     o_ref, st_ref, acc_ref, *, tb, cc, n, hps):
    i = pl.program_id(2)
    nc = tb // cc

    @pl.when(i == 0)
    def _():
        st_ref[...] = jnp.zeros_like(st_ref)
        acc_ref[...] = jnp.zeros_like(acc_ref)

    fwd = (qdf, kdf, wf, uf, inf, eglf)
    bwd = (qdb, kdb, wb, ub, inb, eglb)
    chains = [(d, e) for e in range(hps) for d in range(2)]
    st = {ch: st_ref[ch[0], ch[1]] for ch in chains}
    for step in range(nc):
        stage = {}
        for d, e in chains:
            qd, kd, w, u, intra, egl = fwd if d == 0 else bwd
            c = step if d == 0 else nc - 1 - step
            rows = slice(c * cc, (c + 1) * cc)
            st16 = st[d, e].astype(BF16)
            stage[d, e] = (c, rows, _dot(w[e, rows, :], st16), _dot(qd[e, rows, :], st16))
        for d, e in chains:
            qd, kd, w, u, intra, egl = fwd if d == 0 else bwd
            c, rows, ws, qs = stage[d, e]
            v16 = (u[e, rows, :] - ws).astype(BF16)
            o = qs + _dot(intra[e, rows, :], jnp.concatenate([v16] * (LANES // cc), axis=0))
            kv = lax.dot_general(kd[e, rows, :], v16, (((0,), (0,)), ((), ())), preferred_element_type=F32)
            st[d, e] = st[d, e] * egl[e, c, 0:1, :] + kv
            row0 = i * tb if d == 0 else (n - 1 - i) * tb
            r0 = pl.multiple_of(row0 + c * cc, cc)
            acc_ref[pl.ds(r0, cc), e * DV_A:(e + 1) * DV_A] += o
    for d, e in chains:
        st_ref[d, e] = st[d, e]

    @pl.when(i == n - 1)
    def _():
        for e in range(hps):
            cols = slice(e * DV_A, (e + 1) * DV_A)
            z = z_ref[:, cols].astype(F32)
            y = _rms(acc_ref[:, cols], gn_ref[...]) * (z * jax.nn.sigmoid(z))
            o_ref[:, cols] = y.astype(o_ref.dtype)


def gdn_scan(prep, p, gdn_norm, bsz, s):
    qd, kd, w, u, intra, egl = prep
    tb, cc, hps = GDN_BLOCK, GDN_CHUNK, GDN_HEADS_PER_STEP
    n = s // tb

    def seq(d, last):
        if d == 0:
            return pl.BlockSpec((None, None, hps, tb, last), lambda b, hp, i: (0, b, hp, i, 0))
        return pl.BlockSpec((None, None, hps, tb, last), lambda b, hp, i: (1, b, hp, n - 1 - i, 0))

    def egl_spec(d):
        blk = (None, None, hps, tb // cc, 8, LANES)
        if d == 0:
            return pl.BlockSpec(blk, lambda b, hp, i: (0, b, hp, i, 0, 0))
        return pl.BlockSpec(blk, lambda b, hp, i: (1, b, hp, n - 1 - i, 0, 0))

    def dir_specs(d):
        return [seq(d, DK_A), seq(d, DK_A), seq(d, DK_A), seq(d, DV_A), seq(d, LANES), egl_spec(d)]

    zb0 = (2 * QK_A + V_A) // (hps * LANES)
    return pl.pallas_call(
        functools.partial(_gdn_scan_kernel, tb=tb, cc=cc, n=n, hps=hps),
        grid=(bsz, H_A // hps, n),
        in_specs=dir_specs(0) + dir_specs(1) + [
            pl.BlockSpec((s, hps * LANES), lambda b, hp, i: (b, zb0 + hp)),
            pl.BlockSpec((1, DV_A), lambda b, hp, i: (0, 0)),
        ],
        out_specs=pl.BlockSpec((s, hps * DV_A), lambda b, hp, i: (b, hp)),
        out_shape=jax.ShapeDtypeStruct((bsz * s, V_A), BF16),
        scratch_shapes=[pltpu.VMEM((2, hps, DK_A, DV_A), F32), pltpu.VMEM((s, hps * DV_A), F32)],
        compiler_params=_params(("parallel", "parallel", "arbitrary")),
        name="gdn_scan",
    )(qd, kd, w, u, intra, egl, qd, kd, w, u, intra, egl, p, gdn_norm.reshape(1, DV_A))


def _bucket_thresholds():
    nb = N_BUCKETS // 2
    max_exact = nb // 2
    out = []
    prev = 0
    for dist in range(max_exact, MAX_DIST):
        val = int(math.log(dist / max_exact) / math.log(MAX_DIST / max_exact) * (nb - max_exact))
        val = min(val, nb - 1 - max_exact)
        if val > prev:
            out.extend([dist] * (val - prev))
            prev = val
    return tuple(out)


def _bias_kernel(rb_ref, o_ref, *, t, thresholds):
    d = pl.program_id(0) - 2
    h = pl.program_id(1)
    nb = N_BUCKETS // 2
    max_exact = nb // 2
    r = lax.broadcasted_iota(jnp.int32, (t, t), 0)
    c = lax.broadcasted_iota(jnp.int32, (t, t), 1)
    rel = d * t + r - c
    dist = jnp.abs(rel)
    large = jnp.full((t, t), max_exact, jnp.int32)
    for th in thresholds:
        large = large + (dist >= th).astype(jnp.int32)
    bucket = jnp.where(rel > 0, nb, 0) + jnp.where(dist < max_exact, dist, large)
    out = jnp.zeros((t, t), F32)
    for b in range(N_BUCKETS):
        out = jnp.where(bucket == b, rb_ref[b, h], out)
    o_ref[...] = out * LOG2E


def bias_tiles(rel_bias):
    t = DIFF_TILE
    return pl.pallas_call(
        functools.partial(_bias_kernel, t=t, thresholds=_bucket_thresholds()),
        grid=(5, H_B),
        in_specs=[pl.BlockSpec(memory_space=pltpu.SMEM)],
        out_specs=pl.BlockSpec((None, None, t, t), lambda d, h: (d, h, 0, 0)),
        out_shape=jax.ShapeDtypeStruct((5, H_B, t, t), F32),
        compiler_params=_params(("parallel", "parallel")),
        name="t5_bias_tiles",
    )(rel_bias)


def _online_softmax(n_tiles, scores, vt, acc_ref, nq, shift=None):
    part = min(ATT_PART, nq)
    parts = [slice(c, c + part) for c in range(0, nq, part)]
    m = [None] * len(parts)
    dv = acc_ref.shape[0] - ONES_ROWS
    sts = [scores(0, cols) for cols in parts]
    for j in range(n_tiles):
        nxt = [scores(j + 1, cols) for cols in parts] if j + 1 < n_tiles else None
        c = shift(j) if shift is not None else None
        pts = []
        for i, st in enumerate(sts):
            mx = jnp.max(st, axis=0, keepdims=True)
            if c is not None:
                mx = mx + c
            if j == 0:
                m[i], alpha = mx, None
            else:
                m_new = jnp.maximum(m[i], mx)
                alpha = jnp.exp2(m[i] - m_new)
                m[i] = m_new
            ref = m[i] if c is None else m[i] - c
            pts.append((alpha, jnp.exp2(st - ref).astype(BF16)))
        v = vt(j)
        v1 = jnp.concatenate([v, jnp.ones((ONES_ROWS, v.shape[1]), BF16)], axis=0)
        for cols, (alpha, pt16) in zip(parts, pts):
            pv = _dot(v1, pt16)
            acc_ref[:, cols] = pv if j == 0 else alpha * acc_ref[:, cols] + pv
        sts = nxt
    return acc_ref[dv:dv + 1, :]


def _diff_attn_kernel(q_ref, k_ref, vt_ref, b_ref, dl_ref, sg_ref, o_ref, qs_ref, acc_ref, *, t, n_tiles, lam_init):
    qi = pl.program_id(2)
    q = q_ref[...].astype(F32) * (DH_B ** -0.5 * LOG2E)
    sub = lax.broadcasted_iota(jnp.int32, (LANES, t), 0)
    qs_ref[:, 0:t] = jnp.where(sub < DH_B, q, 0.0).astype(BF16)
    qs_ref[:, t:2 * t] = jnp.where(sub >= DH_B, q, 0.0).astype(BF16)

    n_near = min(3, n_tiles)

    def tile_of(r):
        return lax.rem(qi + (n_tiles - 1 + r), n_tiles)

    def start(r):
        return pl.multiple_of(tile_of(r) * t, t)

    def scores(r, cols):
        st = _dot(k_ref[pl.ds(start(r), t), :], qs_ref[:, cols])
        if r >= n_near:
            return st
        b0 = cols.start % t
        width = cols.stop - cols.start
        tile = jnp.clip(tile_of(r) - qi, -2, 2) + 2
        return st + b_ref[tile, :, b0:b0 + width]

    def shift(r):
        if r < n_near:
            return None
        return jnp.where(tile_of(r) > qi, b_ref[4, 0:1, 0:1], b_ref[0, 0:1, 0:1])

    l = _online_softmax(n_tiles, scores, lambda r: vt_ref[:, pl.ds(start(r), t)], acc_ref, 2 * t, shift)

    dl = dl_ref[...]
    lam = (jnp.exp(jnp.sum(dl[0:1, :] * dl[1:2, :], axis=-1, keepdims=True))
           - jnp.exp(jnp.sum(dl[2:3, :] * dl[3:4, :], axis=-1, keepdims=True)) + lam_init)
    a = acc_ref[0:DV_B, :] / l
    o = (a[:, 0:t] - lam * a[:, t:2 * t]).T
    o_ref[...] = (_rms(o, sg_ref[...]) * (1.0 - lam_init)).astype(o_ref.dtype)


def diff_attention(p, pt, bias, diff_lambda, subln, lam_init, bsz, s):
    t = min(DIFF_TILE, s)
    assert t >= MAX_DIST
    nq = s // t
    kb0 = (2 * QK_A + 2 * V_A) // LANES
    vb0 = QK_B // LANES
    return pl.pallas_call(
        functools.partial(_diff_attn_kernel, t=t, n_tiles=nq, lam_init=lam_init),
        grid=(bsz, H_B, nq),
        in_specs=[
            pl.BlockSpec((LANES, t), lambda b, h, qi: (h, b * nq + qi)),
            pl.BlockSpec((s, LANES), lambda b, h, qi: (b, kb0 + h)),
            pl.BlockSpec((DV_B, s), lambda b, h, qi: (vb0 + h, b)),
            pl.BlockSpec((5, None, t, t), lambda b, h, qi: (0, h, 0, 0)),
            pl.BlockSpec((4, DH_B), lambda b, h, qi: (0, 0)),
            pl.BlockSpec((1, DV_B), lambda b, h, qi: (0, 0)),
        ],
        out_specs=pl.BlockSpec((t, DV_B), lambda b, h, qi: (b * nq + qi, h)),
        out_shape=jax.ShapeDtypeStruct((bsz * s, V_B), BF16),
        scratch_shapes=[
            pltpu.VMEM((LANES, 2 * t), BF16),
            pltpu.VMEM((DV_B + ONES_ROWS, 2 * t), F32),
        ],
        compiler_params=_params(("parallel", "parallel", "parallel")),
        name="diff_attention",
    )(pt, p, pt, bias, diff_lambda, subln.reshape(1, DV_B))


def _rope_k_kernel(c_ref, cs_ref, o_ref):
    tmp = c_ref[...] * cs_ref[...]
    y = tmp + pltpu.roll(tmp, ROPE_C, axis=1)
    lane = lax.broadcasted_iota(jnp.int32, y.shape, 1)
    o_ref[...] = jnp.where(lane < ROPE_C, y, 0.0).astype(o_ref.dtype)


def rope_k(c, cs, bsz, s):
    tr = min(512, s)
    nb = s // tr
    cb = (Q_LORA + KV_LORA) // LANES
    return pl.pallas_call(
        _rope_k_kernel,
        grid=(bsz, nb),
        in_specs=[
            pl.BlockSpec((tr, LANES), lambda b, i: (b * nb + i, cb)),
            pl.BlockSpec((tr, LANES), lambda b, i: (i, 0)),
        ],
        out_specs=pl.BlockSpec((tr, LANES), lambda b, i: (b * nb + i, 0)),
        out_shape=jax.ShapeDtypeStruct((bsz * s, LANES), BF16),
        compiler_params=_params(("parallel", "parallel")),
        name="rope_k",
    )(c, cs)


def _mla_attn_kernel(qa_ref, qb_ref, cs_ref, kn_ref, kr_ref, vt_ref, o_ref, qs_ref, acc_ref, *, tk, n_tiles):
    scale = (NOPE_C + ROPE_C) ** -0.5 * LOG2E
    tmp = qb_ref[...].astype(F32) * cs_ref[...]
    qr = tmp + pltpu.roll(tmp, ROPE_C, axis=0)
    qs_ref[0:LANES, :] = (qa_ref[...].astype(F32) * scale).astype(BF16)
    qs_ref[LANES:2 * LANES, :] = (qr * scale).astype(BF16)

    kcs = {}

    def scores(j, cols):
        if j not in kcs:
            rows = slice(j * tk, (j + 1) * tk)
            kcs[j] = jnp.concatenate([kn_ref[rows, :], kr_ref[rows, :]], axis=-1)
        return _dot(kcs[j], qs_ref[:, cols])

    l = _online_softmax(n_tiles, scores, lambda j: vt_ref[:, j * tk:(j + 1) * tk], acc_ref, qs_ref.shape[1])
    o_ref[...] = (acc_ref[0:DV_C, :] / l).T.astype(o_ref.dtype)


def mla_attention(qt, kn, krp, vt, cst, bsz, s):
    tq = min(MLA_TQ, s)
    tk = min(MLA_TK, s)
    nq = s // tq
    return pl.pallas_call(
        functools.partial(_mla_attn_kernel, tk=tk, n_tiles=s // tk),
        grid=(bsz, H_C, nq),
        in_specs=[
            pl.BlockSpec((LANES, tq), lambda b, h, qi: (2 * h, b * nq + qi)),
            pl.BlockSpec((LANES, tq), lambda b, h, qi: (2 * h + 1, b * nq + qi)),
            pl.BlockSpec((LANES, tq), lambda b, h, qi: (0, qi)),
            pl.BlockSpec((s, LANES), lambda b, h, qi: (b, h)),
            pl.BlockSpec((s, LANES), lambda b, h, qi: (b, 0)),
            pl.BlockSpec((DV_C, s), lambda b, h, qi: (h, b)),
        ],
        out_specs=pl.BlockSpec((tq, DV_C), lambda b, h, qi: (b * nq + qi, h)),
        out_shape=jax.ShapeDtypeStruct((bsz * s, H_C * DV_C), BF16),
        scratch_shapes=[
            pltpu.VMEM((2 * LANES, tq), BF16),
            pltpu.VMEM((DV_C + ONES_ROWS, tq), F32),
        ],
        compiler_params=_params(("parallel", "parallel", "parallel")),
        name="mla_attention",
    )(qt, qt, cst, kn, krp, vt)


def _rotate_half_cols(w):
    half = w.shape[-1] // 2
    return jnp.concatenate([-w[..., half:], w[..., :half]], axis=-1)


def _prepare_weights(w_in_e, a_log, dt_bias, w_out_e, w_in_c, w_qb, w_out_c, w_mlp1, w_mlp2, w_kvb):
    g0 = 2 * QK_A + 2 * V_A
    prep = {}
    b0 = g0 + 4 * H_A
    t_last = lambda w: jnp.swapaxes(w, -1, -2)
    prep["w_e_main"] = jnp.concatenate([w_in_e[:, :, :g0], w_in_e[:, :, b0 + QK_B:b0 + 2 * QK_B]],
                                       axis=-1).astype(BF16)
    prep["w_e_t"] = t_last(jnp.concatenate([w_in_e[:, :, b0:b0 + QK_B], w_in_e[:, :, b0 + 2 * QK_B:]],
                                           axis=-1)).astype(BF16)
    prep["w_e_gate"] = jnp.pad(w_in_e[:, :, g0:g0 + 4 * H_A], ((0, 0), (0, 0), (0, LANES - 4 * H_A))).astype(BF16)
    n_even = a_log.shape[0]
    pad_row = lambda v: jnp.pad(v.reshape(n_even, 1, 2 * H_A).astype(F32),
                                ((0, 0), (0, 0), (2 * H_A, LANES - 4 * H_A)))
    prep["alog_row"] = pad_row(a_log)
    prep["dtb_row"] = pad_row(dt_bias)
    prep["w_out_e"] = w_out_e.astype(BF16)
    k_rope_w = w_in_c[:, :, Q_LORA + KV_LORA:]
    prep["w_c"] = jnp.concatenate([w_in_c, _rotate_half_cols(k_rope_w)], axis=-1).astype(BF16)
    n_odd = w_qb.shape[0]
    wq = w_qb.reshape(n_odd, Q_LORA, H_C, NOPE_C + ROPE_C)
    wq_rope = wq[..., NOPE_C:]
    prep["w_qb_t"] = t_last(jnp.concatenate([wq[..., :NOPE_C], wq_rope, _rotate_half_cols(wq_rope)], axis=-1).reshape(
        n_odd, Q_LORA, H_C * 2 * LANES)).astype(BF16)
    wkv = w_kvb.reshape(n_odd, KV_LORA, H_C, NOPE_C + DV_C)
    prep["w_kn"] = wkv[..., :NOPE_C].reshape(n_odd, KV_LORA, H_C * NOPE_C).astype(BF16)
    prep["w_v_t"] = t_last(wkv[..., NOPE_C:].reshape(n_odd, KV_LORA, H_C * DV_C)).astype(BF16)
    prep["w_out_c"] = w_out_c.astype(BF16)
    prep["w_mlp1"] = w_mlp1.astype(BF16)
    prep["w_mlp2"] = w_mlp2.astype(BF16)
    return prep


def _rope_table(s):
    inv_freq = ROPE_THETA ** (-jnp.arange(0, ROPE_C, 2, dtype=F32) / ROPE_C)
    ang = jnp.arange(s, dtype=F32)[:, None] * inv_freq[None, :]
    cos, sin = jnp.cos(ang), jnp.sin(ang)
    return jnp.concatenate([cos, cos, sin, sin], axis=-1)


def _trunk(x, pw, bias, norm_mix, norm_mlp, norm_final, conv_w, gdn_norm, diff_lambda, subln, q_norm, kv_norm):
    bsz, s, d = x.shape
    x = x.reshape(bsz * s, d)
    cs = _rope_table(s)
    cst = cs.T
    for layer in range(DEPTH):
        i = layer // 2
        if layer % 2 == 0:
            p, gates = norm_matmul_aux(x, norm_mix[layer], pw["w_e_main"][i], pw["w_e_gate"][i], BF16, 1024, 1024)
            pt = norm_matmul_t(x, 0, d, norm_mix[layer], pw["w_e_t"][i], BF16, 1024, 1024)
            prep = gdn_prep(p, gates, conv_w[i], pw["alog_row"][i], pw["dtb_row"][i], bsz, s)
            o_a = gdn_scan(prep, p, gdn_norm[i], bsz, s)
            lam_init = 0.8 - 0.6 * math.exp(-0.3 * layer)
            o_b = diff_attention(p, pt, bias, diff_lambda[i], subln[i], lam_init, bsz, s)
            x = matmul_res([o_a, o_b], pw["w_out_e"][i], x, 1024, 1024)
        else:
            c = norm_matmul(x, 0, d, norm_mix[layer], pw["w_c"][i], F32, 1024, pw["w_c"].shape[-1])
            qt = norm_matmul_t(c, 0, Q_LORA, q_norm[i], pw["w_qb_t"][i], BF16, 1024, 1024)
            kn = norm_matmul(c, 1, KV_LORA, kv_norm[i], pw["w_kn"][i], BF16, 1024, 1024)
            vt = norm_matmul_t(c, 1, KV_LORA, kv_norm[i], pw["w_v_t"][i], BF16, 1024, 1024)
            krp = rope_k(c, cs, bsz, s)
            o_c = mla_attention(qt, kn, krp, vt, cst, bsz, s)
            x = matmul_res([o_c], pw["w_out_c"][i], x, 1024, 1024)
        x = mlp(x, norm_mlp[layer], pw["w_mlp1"][layer], pw["w_mlp2"][layer], norm_final,
                layer == DEPTH - 1, 1024, 512)
    return x.reshape(bsz, s, d)


def kernel(x_prompt, x_sample, norm_mix, norm_mlp, norm_final, rel_bias, w_in_e, conv_w, a_log, dt_bias, gdn_norm, diff_lambda, subln, w_out_e, w_in_c, q_norm, w_qb, kv_norm, w_kvb, w_out_c, w_mlp1, w_mlp2):
    pw = _prepare_weights(w_in_e, a_log, dt_bias, w_out_e, w_in_c, w_qb, w_out_c, w_mlp1, w_mlp2, w_kvb)
    bias = bias_tiles(rel_bias)
    run = lambda x: _trunk(x, pw, bias, norm_mix, norm_mlp, norm_final, conv_w, gdn_norm, diff_lambda,
                           subln, q_norm, kv_norm)
    return (run(x_prompt), run(x_sample))
```

```python
import functools
import math

import jax
import jax.numpy as jnp
from jax import lax
from jax.experimental import pallas as pl
from jax.experimental.pallas import tpu as pltpu

F32 = jnp.float32
BF16 = jnp.bfloat16

D_MODEL = 2048
DEPTH = 4
H_A = 8
DK_A = 128
DV_A = 128
QK_A = H_A * DK_A
V_A = H_A * DV_A
H_B = 8
DH_B = 64
DV_B = 128
QK_B = H_B * 2 * DH_B
V_B = H_B * DV_B
N_BUCKETS = 32
MAX_DIST = 128
H_C = 16
Q_LORA = 512
KV_LORA = 512
NOPE_C = 128
ROPE_C = 64
DV_C = 128
ROPE_THETA = 10000.0
D_FF = 4 * D_MODEL
EPS = 1e-6

LOG2E = math.log2(math.e)
LANES = 128
BF16_SUBLANES = 16
VMEM_LIMIT = 56 * 1024 * 1024

GDN_BLOCK = 256
GDN_CHUNK = 64
GDN_HEADS_PER_STEP = 4
DIFF_TILE = 512
MLA_TQ = 2048
MLA_TK = 512
ATT_PART = 512
ONES_ROWS = BF16_SUBLANES


def _params(sem):
    return pltpu.CompilerParams(dimension_semantics=sem, vmem_limit_bytes=VMEM_LIMIT)


def _dot(a, b):
    return jnp.dot(a, b, preferred_element_type=F32)


def _dot_nt(a, b):
    return lax.dot_general(a, b, (((1,), (1,)), ((), ())), preferred_element_type=F32)


def _rms(x, g):
    ms = jnp.mean(x * x, axis=-1, keepdims=True)
    return x * lax.rsqrt(ms + EPS) * g


def _norm_matmul_kernel(x_ref, g_ref, w_ref, o_ref, h_ref):
    @pl.when(pl.program_id(1) == 0)
    def _():
        h_ref[...] = _rms(x_ref[...].astype(F32), g_ref[...]).astype(BF16)

    o_ref[...] = _dot(h_ref[...], w_ref[...]).astype(o_ref.dtype)


def norm_matmul(x, kblock, k, g, w, out_dtype, tm, tn):
    t = x.shape[0]
    n = w.shape[1]
    tm = min(tm, t)
    tn = min(tn, n)
    return pl.pallas_call(
        _norm_matmul_kernel,
        grid=(t // tm, n // tn),
        in_specs=[
            pl.BlockSpec((tm, k), lambda i, j: (i, kblock)),
            pl.BlockSpec((1, k), lambda i, j: (0, 0)),
            pl.BlockSpec((k, tn), lambda i, j: (0, j)),
        ],
        out_specs=pl.BlockSpec((tm, tn), lambda i, j: (i, j)),
        out_shape=jax.ShapeDtypeStruct((t, n), out_dtype),
        scratch_shapes=[pltpu.VMEM((tm, k), BF16)],
        compiler_params=_params(("parallel", "arbitrary")),
        name="norm_matmul",
    )(x, g.reshape(1, k).astype(F32), w)


def _norm_matmul_aux_kernel(x_ref, g_ref, w_ref, wa_ref, o_ref, oa_ref, h_ref):
    @pl.when(pl.program_id(1) == 0)
    def _():
        h = _rms(x_ref[...].astype(F32), g_ref[...]).astype(BF16)
        h_ref[...] = h
        oa_ref[...] = _dot(h, wa_ref[...])

    o_ref[...] = _dot(h_ref[...], w_ref[...]).astype(o_ref.dtype)


def norm_matmul_aux(x, g, w, w_aux, out_dtype, tm, tn):
    t, k = x.shape
    n = w.shape[1]
    tm = min(tm, t)
    tn = min(tn, n)
    return pl.pallas_call(
        _norm_matmul_aux_kernel,
        grid=(t // tm, n // tn),
        in_specs=[
            pl.BlockSpec((tm, k), lambda i, j: (i, 0)),
            pl.BlockSpec((1, k), lambda i, j: (0, 0)),
            pl.BlockSpec((k, tn), lambda i, j: (0, j)),
            pl.BlockSpec((k, LANES), lambda i, j: (0, 0)),
        ],
        out_specs=[pl.BlockSpec((tm, tn), lambda i, j: (i, j)),
                   pl.BlockSpec((tm, LANES), lambda i, j: (i, 0))],
        out_shape=[jax.ShapeDtypeStruct((t, n), out_dtype), jax.ShapeDtypeStruct((t, LANES), F32)],
        scratch_shapes=[pltpu.VMEM((tm, k), BF16)],
        compiler_params=_params(("parallel", "arbitrary")),
        name="norm_matmul_aux",
    )(x, g.reshape(1, k).astype(F32), w, w_aux)


def _norm_matmul_t_kernel(x_ref, g_ref, wt_ref, o_ref, h_ref):
    @pl.when(pl.program_id(1) == 0)
    def _():
        h_ref[...] = _rms(x_ref[...].astype(F32), g_ref[...]).astype(BF16)

    o_ref[...] = _dot_nt(wt_ref[...], h_ref[...]).astype(o_ref.dtype)


def norm_matmul_t(x, kblock, k, g, wt, out_dtype, tm, tn):
    t = x.shape[0]
    n = wt.shape[0]
    tm = min(tm, t)
    tn = min(tn, n)
    return pl.pallas_call(
        _norm_matmul_t_kernel,
        grid=(t // tm, n // tn),
        in_specs=[
            pl.BlockSpec((tm, k), lambda i, j: (i, kblock)),
            pl.BlockSpec((1, k), lambda i, j: (0, 0)),
            pl.BlockSpec((tn, k), lambda i, j: (j, 0)),
        ],
        out_specs=pl.BlockSpec((tn, tm), lambda i, j: (j, i)),
        out_shape=jax.ShapeDtypeStruct((n, t), out_dtype),
        scratch_shapes=[pltpu.VMEM((tm, k), BF16)],
        compiler_params=_params(("parallel", "arbitrary")),
        name="norm_matmul_t",
    )(x, g.reshape(1, k).astype(F32), wt)


def _matmul_res_kernel(*refs, widths):
    n = len(widths)
    w_ref, r_ref, o_ref = refs[n:]
    acc = r_ref[...]
    off = 0
    for a_ref, wd in zip(refs[:n], widths):
        acc = acc + _dot(a_ref[...], w_ref[off:off + wd, :])
        off += wd
    o_ref[...] = acc


def matmul_res(parts, w, res, tm, tn):
    t = res.shape[0]
    n = w.shape[1]
    tm = min(tm, t)
    widths = tuple(p.shape[1] for p in parts)
    return pl.pallas_call(
        functools.partial(_matmul_res_kernel, widths=widths),
        grid=(t // tm, n // tn),
        in_specs=[pl.BlockSpec((tm, wd), lambda i, j: (i, 0)) for wd in widths]
        + [
            pl.BlockSpec((sum(widths), tn), lambda i, j: (0, j)),
            pl.BlockSpec((tm, tn), lambda i, j: (i, j)),
        ],
        out_specs=pl.BlockSpec((tm, tn), lambda i, j: (i, j)),
        out_shape=jax.ShapeDtypeStruct((t, n), F32),
        compiler_params=_params(("parallel", "parallel")),
        name="matmul_res",
    )(*parts, w, res)


def _mlp_kernel(x_ref, g_ref, w1_ref, w2_ref, gf_ref, o_ref, h_ref, *, nf, final_norm):
    f = pl.program_id(1)

    @pl.when(f == 0)
    def _():
        x = x_ref[...]
        h_ref[...] = _rms(x, g_ref[...]).astype(BF16)
        o_ref[...] = x

    a = _dot(h_ref[...], w1_ref[...])
    a = jnp.square(jnp.maximum(a, 0.0)).astype(BF16)
    o_ref[...] += _dot(a, w2_ref[...])

    if final_norm:
        @pl.when(f == nf - 1)
        def _():
            o_ref[...] = _rms(o_ref[...], gf_ref[...])


def mlp(x, g, w1, w2, g_final, final_norm, tm, tf):
    t, d = x.shape
    tm = min(tm, t)
    nf = D_FF // tf
    return pl.pallas_call(
        functools.partial(_mlp_kernel, nf=nf, final_norm=final_norm),
        grid=(t // tm, nf),
        in_specs=[
            pl.BlockSpec((tm, d), lambda i, f: (i, 0)),
            pl.BlockSpec((1, d), lambda i, f: (0, 0)),
            pl.BlockSpec((d, tf), lambda i, f: (0, f)),
            pl.BlockSpec((tf, d), lambda i, f: (f, 0)),
            pl.BlockSpec((1, d), lambda i, f: (0, 0)),
        ],
        out_specs=pl.BlockSpec((tm, d), lambda i, f: (i, 0)),
        out_shape=jax.ShapeDtypeStruct((t, d), F32),
        scratch_shapes=[pltpu.VMEM((tm, d), BF16)],
        compiler_params=_params(("parallel", "arbitrary")),
        name="mlp",
    )(x, g.reshape(1, d), w1, w2, g_final.reshape(1, d))


def _softplus(x):
    return jnp.maximum(x, 0.0) + jnp.log1p(jnp.exp(-jnp.abs(x)))


def _gdn_prep_kernel(q_ref, qp_ref, qn_ref, k_ref, kp_ref, kn_ref, v_ref, vp_ref, vn_ref,
                     cwq_ref, cwk_ref, cwv_ref, gt_ref, alog_ref, dtb_ref,
                     qd_ref, kd_ref, w_ref, u_ref, in_ref, egl_ref, *, tb, cc, n):
    hp = pl.program_id(1)
    i = pl.program_id(2)
    width = GDN_HEADS_PER_STEP * LANES
    roww = lax.broadcasted_iota(jnp.int32, (tb, width), 0)

    def conv_silu(x_ref, p_ref, n_ref, cw_ref):
        x = x_ref[...].astype(F32)
        prev = jnp.where(i == 0, 0.0, p_ref[BF16_SUBLANES - 1:BF16_SUBLANES, :].astype(F32))
        nxt = jnp.where(i == n - 1, 0.0, n_ref[0:1, :].astype(F32))
        xm = jnp.where(roww == 0, prev, pltpu.roll(x, 1, axis=0))
        xp = jnp.where(roww == tb - 1, nxt, pltpu.roll(x, tb - 1, axis=0))
        cw = cw_ref[...]
        y = xm * cw[0:1, :] + x * cw[1:2, :] + xp * cw[2:3, :]
        return y * jax.nn.sigmoid(y)

    q_all = conv_silu(q_ref, qp_ref, qn_ref, cwq_ref)
    k_all = conv_silu(k_ref, kp_ref, kn_ref, cwk_ref)
    v_all = conv_silu(v_ref, vp_ref, vn_ref, cwv_ref)

    row = lax.broadcasted_iota(jnp.int32, (tb, LANES), 0)
    lane = lax.broadcasted_iota(jnp.int32, (tb, LANES), 1)
    rin = row & (cc - 1)
    gt = gt_ref[...]
    beta_all = jax.nn.sigmoid(gt)
    g_all = -jnp.exp(alog_ref[...]) * _softplus(gt + dtb_ref[...])
    pre = g_all
    suf = g_all
    s = 1
    while s < cc:
        pre = pre + jnp.where(rin >= s, pltpu.roll(pre, s, axis=0), 0.0)
        suf = suf + jnp.where(rin < cc - s, pltpu.roll(suf, tb - s, axis=0), 0.0)
        s *= 2
    tot = pre + suf - g_all
    pre_t = pre.T
    suf_t = suf.T
    sub_t = lax.broadcasted_iota(jnp.int32, (LANES, tb), 0)

    def col(x, j):
        return jnp.broadcast_to(jnp.sum(jnp.where(lane == j, x, 0.0), axis=1, keepdims=True), (tb, LANES))

    def wide(c):
        return jnp.concatenate([c] * (tb // LANES), axis=1)

    def rowv(xt, j):
        return jnp.sum(jnp.where(sub_t == j, xt, 0.0), axis=0, keepdims=True)

    lgc = cc.bit_length() - 1
    ii = lax.broadcasted_iota(jnp.int32, (tb, tb), 0)
    jj = lax.broadcasted_iota(jnp.int32, (tb, tb), 1)
    same = (ii >> lgc) == (jj >> lgc)
    incl = (same & (ii >= jj), same & (ii <= jj))
    offdiag = ii != jj
    eye = (ii == jj).astype(F32)
    pair = (ii >> 1) == (jj >> 1)
    levels = [((ii >> (lg + 1)) == (jj >> (lg + 1))) & ((ii >> lg) != (jj >> lg)) for lg in range(1, lgc)]

    chains = []
    for e in range(GDN_HEADS_PER_STEP):
        h = hp * GDN_HEADS_PER_STEP + e
        cols = slice(e * LANES, (e + 1) * LANES)
        q, k, v = q_all[:, cols], k_all[:, cols], v_all[:, cols]
        sumsq = lambda t: jnp.broadcast_to(jnp.sum(t * t, axis=-1, keepdims=True), (tb, LANES))
        q = q * (lax.rsqrt(sumsq(q) + EPS) * (DK_A ** -0.5))
        k = k * lax.rsqrt(sumsq(k) + EPS)
        k16 = k.astype(BF16)
        qk = _dot_nt(q.astype(BF16), k16)
        kk = _dot_nt(k16, k16)
        for d in range(2):
            jb = d * H_A + h
            jg = 2 * H_A + d * H_A + h
            gcol, grow = (col(pre, jg), rowv(pre_t, jg)) if d == 0 else (col(suf, jg), rowv(suf_t, jg))
            beta = col(beta_all, jb)
            tcol = col(tot, jg)
            dec = jnp.where(incl[d], jnp.exp(jnp.where(incl[d], wide(gcol) - grow, 0.0)), 0.0)
            eg = jnp.exp(gcol)
            a = jnp.where(offdiag, kk * dec, 0.0) * wide(beta)
            m = qk * dec
            in_ref[d, e] = (m[:, 0:LANES] + m[:, LANES:2 * LANES]).astype(BF16)
            qd_ref[d, e] = (q * eg).astype(BF16)
            kd_ref[d, e] = (k * jnp.exp(tcol - gcol)).astype(BF16)
            eb = jnp.exp(tcol)
            for c in range(tb // cc):
                egl_ref[d, e, c] = jnp.broadcast_to(eb[c * cc:c * cc + 1, :], (8, LANES))
            rhs = jnp.concatenate([v * beta, k * (beta * eg)], axis=1).astype(BF16)
            chains.append((d, e, a, rhs))

    nas = [-a for (_, _, a, _) in chains]
    xs = [(eye + jnp.where(pair, na, 0.0)).astype(BF16) for na in nas]
    for off in levels:
        ys = [_dot(jnp.where(off, na, 0.0).astype(BF16), x) for na, x in zip(nas, xs)]
        xs = [jnp.where(off, _dot(x, y.astype(BF16)).astype(BF16), x) for x, y in zip(xs, ys)]
    for (d, e, _, rhs), x in zip(chains, xs):
        uw = _dot(x, rhs)
        u_ref[d, e] = uw[:, 0:DV_A]
        w_ref[d, e] = uw[:, DV_A:].astype(BF16)


def gdn_prep(p, gates, conv_w, alog_row, dtb_row, bsz, s):
    tb, cc, hps = GDN_BLOCK, GDN_CHUNK, GDN_HEADS_PER_STEP
    assert tb == 2 * LANES and cc == LANES // 2
    n = s // tb
    t = bsz * s
    width = hps * LANES
    rpb = tb // BF16_SUBLANES
    last_blk = t // BF16_SUBLANES - 1

    def main(cb):
        return pl.BlockSpec((tb, width), lambda b, hp, i: (b * n + i, cb // hps + hp))

    def prev(cb):
        return pl.BlockSpec((BF16_SUBLANES, width),
                            lambda b, hp, i: (jnp.maximum((b * n + i) * rpb - 1, 0), cb // hps + hp))

    def nxt(cb):
        return pl.BlockSpec((BF16_SUBLANES, width),
                            lambda b, hp, i: (jnp.minimum((b * n + i + 1) * rpb, last_blk), cb // hps + hp))

    def cw(cb):
        return pl.BlockSpec((3, width), lambda b, hp, i: (0, cb // hps + hp))

    row_spec = pl.BlockSpec((1, LANES), lambda b, hp, i: (0, 0))
    seq_out = lambda last: pl.BlockSpec((2, None, hps, tb, last), lambda b, hp, i: (0, b, hp, i, 0))
    out_shape = [
        jax.ShapeDtypeStruct((2, bsz, H_A, s, DK_A), BF16),
        jax.ShapeDtypeStruct((2, bsz, H_A, s, DK_A), BF16),
        jax.ShapeDtypeStruct((2, bsz, H_A, s, DK_A), BF16),
        jax.ShapeDtypeStruct((2, bsz, H_A, s, DV_A), F32),
        jax.ShapeDtypeStruct((2, bsz, H_A, s, LANES), BF16),
        jax.ShapeDtypeStruct((2, bsz, H_A, s // cc, 8, LANES), F32),
    ]
    out_specs = [
        seq_out(DK_A), seq_out(DK_A), seq_out(DK_A), seq_out(DV_A), seq_out(LANES),
        pl.BlockSpec((2, None, hps, tb // cc, 8, LANES), lambda b, hp, i: (0, b, hp, i, 0, 0)),
    ]
    kb0, vb0 = QK_A // LANES, 2 * QK_A // LANES
    return pl.pallas_call(
        functools.partial(_gdn_prep_kernel, tb=tb, cc=cc, n=n),
        grid=(bsz, H_A // hps, n),
        in_specs=[main(0), prev(0), nxt(0), main(kb0), prev(kb0), nxt(kb0), main(vb0), prev(vb0), nxt(vb0),
                  cw(0), cw(kb0), cw(vb0),
                  pl.BlockSpec((tb, LANES), lambda b, hp, i: (b * n + i, 0)),
                  row_spec, row_spec],
        out_specs=out_specs,
        out_shape=out_shape,
        compiler_params=_params(("parallel", "parallel", "parallel")),
        name="gdn_prep",
    )(p, p, p, p, p, p, p, p, p, conv_w, conv_w, conv_w, gates, alog_row, dtb_row)


def _gdn_scan_kernel(qdf, kdf, wf, uf, inf, eglf, qdb, kdb, wb, ub, inb, eglb, z_ref, gn_ref,
                     o_ref, st_ref, acc_ref, *, tb, cc, n, hps):
    i = pl.program_id(2)
    nc = tb // cc

    @pl.when(i == 0)
    def _():
        st_ref[...] = jnp.zeros_like(st_ref)
        acc_ref[...] = jnp.zeros_like(acc_ref)

    fwd = (qdf, kdf, wf, uf, inf, eglf)
    bwd = (qdb, kdb, wb, ub, inb, eglb)
    chains = [(d, e) for e in range(hps) for d in range(2)]
    st = {ch: st_ref[ch[0], ch[1]] for ch in chains}
    for step in range(nc):
        stage = {}
        for d, e in chains:
            qd, kd, w, u, intra, egl = fwd if d == 0 else bwd
            c = step if d == 0 else nc - 1 - step
            rows = slice(c * cc, (c + 1) * cc)
            st16 = st[d, e].astype(BF16)
            stage[d, e] = (c, rows, _dot(w[e, rows, :], st16), _dot(qd[e, rows, :], st16))
        for d, e in chains:
            qd, kd, w, u, intra, egl = fwd if d == 0 else bwd
            c, rows, ws, qs = stage[d, e]
            v16 = (u[e, rows, :] - ws).astype(BF16)
            o = qs + _dot(intra[e, rows, :], jnp.concatenate([v16] * (LANES // cc), axis=0))
            kv = lax.dot_general(kd[e, rows, :], v16, (((0,), (0,)), ((), ())), preferred_element_type=F32)
            st[d, e] = st[d, e] * egl[e, c, 0:1, :] + kv
            row0 = i * tb if d == 0 else (n - 1 - i) * tb
            r0 = pl.multiple_of(row0 + c * cc, cc)
            acc_ref[pl.ds(r0, cc), e * DV_A:(e + 1) * DV_A] += o
    for d, e in chains:
        st_ref[d, e] = st[d, e]

    @pl.when(i == n - 1)
    def _():
        for e in range(hps):
            cols = slice(e * DV_A, (e + 1) * DV_A)
            z = z_ref[:, cols].astype(F32)
            y = _rms(acc_ref[:, cols], gn_ref[...]) * (z * jax.nn.sigmoid(z))
            o_ref[:, cols] = y.astype(o_ref.dtype)


def gdn_scan(prep, p, gdn_norm, bsz, s):
    qd, kd, w, u, intra, egl = prep
    tb, cc, hps = GDN_BLOCK, GDN_CHUNK, GDN_HEADS_PER_STEP
    n = s // tb

    def seq(d, last):
        if d == 0:
            return pl.BlockSpec((None, None, hps, tb, last), lambda b, hp, i: (0, b, hp, i, 0))
        return pl.BlockSpec((None, None, hps, tb, last), lambda b, hp, i: (1, b, hp, n - 1 - i, 0))

    def egl_spec(d):
        blk = (None, None, hps, tb // cc, 8, LANES)
        if d == 0:
            return pl.BlockSpec(blk, lambda b, hp, i: (0, b, hp, i, 0, 0))
        return pl.BlockSpec(blk, lambda b, hp, i: (1, b, hp, n - 1 - i, 0, 0))

    def dir_specs(d):
        return [seq(d, DK_A), seq(d, DK_A), seq(d, DK_A), seq(d, DV_A), seq(d, LANES), egl_spec(d)]

    zb0 = (2 * QK_A + V_A) // (hps * LANES)
    return pl.pallas_call(
        functools.partial(_gdn_scan_kernel, tb=tb, cc=cc, n=n, hps=hps),
        grid=(bsz, H_A // hps, n),
        in_specs=dir_specs(0) + dir_specs(1) + [
            pl.BlockSpec((s, hps * LANES), lambda b, hp, i: (b, zb0 + hp)),
            pl.BlockSpec((1, DV_A), lambda b, hp, i: (0, 0)),
        ],
        out_specs=pl.BlockSpec((s, hps * DV_A), lambda b, hp, i: (b, hp)),
        out_shape=jax.ShapeDtypeStruct((bsz * s, V_A), BF16),
        scratch_shapes=[pltpu.VMEM((2, hps, DK_A, DV_A), F32), pltpu.VMEM((s, hps * DV_A), F32)],
        compiler_params=_params(("parallel", "parallel", "arbitrary")),
        name="gdn_scan",
    )(qd, kd, w, u, intra, egl, qd, kd, w, u, intra, egl, p, gdn_norm.reshape(1, DV_A))


def _bucket_thresholds():
    nb = N_BUCKETS // 2
    max_exact = nb // 2
    out = []
    prev = 0
    for dist in range(max_exact, MAX_DIST):
        val = int(math.log(dist / max_exact) / math.log(MAX_DIST / max_exact) * (nb - max_exact))
        val = min(val, nb - 1 - max_exact)
        if val > prev:
            out.extend([dist] * (val - prev))
            prev = val
    return tuple(out)


def _bias_kernel(rb_ref, o_ref, *, t, thresholds):
    d = pl.program_id(0) - 2
    h = pl.program_id(1)
    nb = N_BUCKETS // 2
    max_exact = nb // 2

    @pl.when(jnp.abs(d) >= 2)
    def _():
        far = jnp.where(d > 0, rb_ref[N_BUCKETS - 1, h], rb_ref[nb - 1, h])
        o_ref[...] = jnp.full((t, t), far * LOG2E, F32)

    @pl.when(jnp.abs(d) < 2)
    def _():
        r = lax.broadcasted_iota(jnp.int32, (t, t), 0)
        c = lax.broadcasted_iota(jnp.int32, (t, t), 1)
        rel = d * t + r - c
        dist = jnp.abs(rel)
        large = jnp.full((t, t), max_exact, jnp.int32)
        for th in thresholds:
            large = large + (dist >= th).astype(jnp.int32)
        bucket = jnp.where(rel > 0, nb, 0) + jnp.where(dist < max_exact, dist, large)
        out = jnp.zeros((t, t), F32)
        for b in range(N_BUCKETS):
            out = jnp.where(bucket == b, rb_ref[b, h], out)
        o_ref[...] = out * LOG2E


def bias_tiles(rel_bias):
    t = DIFF_TILE
    return pl.pallas_call(
        functools.partial(_bias_kernel, t=t, thresholds=_bucket_thresholds()),
        grid=(5, H_B),
        in_specs=[pl.BlockSpec(memory_space=pltpu.SMEM)],
        out_specs=pl.BlockSpec((None, None, t, t), lambda d, h: (d, h, 0, 0)),
        out_shape=jax.ShapeDtypeStruct((5, H_B, t, t), F32),
        compiler_params=_params(("parallel", "parallel")),
        name="t5_bias_tiles",
    )(rel_bias)


def _online_softmax(n_tiles, scores, vt, acc_ref, nq, shift=None):
    part = min(ATT_PART, nq)
    parts = [slice(c, c + part) for c in range(0, nq, part)]
    m = [None] * len(parts)
    dv = acc_ref.shape[0] - ONES_ROWS
    sts = [scores(0, cols) for cols in parts]
    for j in range(n_tiles):
        nxt = [scores(j + 1, cols) for cols in parts] if j + 1 < n_tiles else None
        c = shift(j) if shift is not None else None
        pts = []
        for i, st in enumerate(sts):
            mx = jnp.max(st, axis=0, keepdims=True)
            if c is not None:
                mx = mx + c
            if j == 0:
                m[i], alpha = mx, None
            else:
                m_new = jnp.maximum(m[i], mx)
                alpha = jnp.exp2(m[i] - m_new)
                m[i] = m_new
            ref = m[i] if c is None else m[i] - c
            pts.append((alpha, jnp.exp2(st - ref).astype(BF16)))
        v = vt(j)
        v1 = jnp.concatenate([v, jnp.ones((ONES_ROWS, v.shape[1]), BF16)], axis=0)
        for cols, (alpha, pt16) in zip(parts, pts):
            pv = _dot(v1, pt16)
            acc_ref[:, cols] = pv if j == 0 else alpha * acc_ref[:, cols] + pv
        sts = nxt
    return acc_ref[dv:dv + 1, :]


def _diff_attn_kernel(q_ref, k_ref, vt_ref, b_ref, dl_ref, sg_ref, o_ref, qs_ref, acc_ref, *, t, n_tiles, lam_init):
    qi = pl.program_id(2)
    q = q_ref[...].astype(F32) * (DH_B ** -0.5 * LOG2E)
    sub = lax.broadcasted_iota(jnp.int32, (LANES, t), 0)
    qs_ref[:, 0:t] = jnp.where(sub < DH_B, q, 0.0).astype(BF16)
    qs_ref[:, t:2 * t] = jnp.where(sub >= DH_B, q, 0.0).astype(BF16)

    n_near = min(3, n_tiles)

    def tile_of(r):
        return lax.rem(qi + (n_tiles - 1 + r), n_tiles)

    def start(r):
        return pl.multiple_of(tile_of(r) * t, t)

    def scores(r, cols):
        st = _dot(k_ref[pl.ds(start(r), t), :], qs_ref[:, cols])
        if r >= n_near:
            return st
        b0 = cols.start % t
        width = cols.stop - cols.start
        tile = jnp.clip(tile_of(r) - qi, -2, 2) + 2
        return st + b_ref[tile, :, b0:b0 + width]

    def shift(r):
        if r < n_near:
            return None
        return jnp.where(tile_of(r) > qi, b_ref[4, 0:1, 0:1], b_ref[0, 0:1, 0:1])

    l = _online_softmax(n_tiles, scores, lambda r: vt_ref[:, pl.ds(start(r), t)], acc_ref, 2 * t, shift)

    dl = dl_ref[...]
    lam = (jnp.exp(jnp.sum(dl[0:1, :] * dl[1:2, :], axis=-1, keepdims=True))
           - jnp.exp(jnp.sum(dl[2:3, :] * dl[3:4, :], axis=-1, keepdims=True)) + lam_init)
    a = acc_ref[0:DV_B, :] / l
    o = (a[:, 0:t] - lam * a[:, t:2 * t]).T
    o_ref[...] = (_rms(o, sg_ref[...]) * (1.0 - lam_init)).astype(o_ref.dtype)


def diff_attention(p, pt, bias, diff_lambda, subln, lam_init, bsz, s):
    t = min(DIFF_TILE, s)
    assert t >= MAX_DIST
    nq = s // t
    kb0 = (2 * QK_A + 2 * V_A) // LANES
    vb0 = QK_B // LANES
    return pl.pallas_call(
        functools.partial(_diff_attn_kernel, t=t, n_tiles=nq, lam_init=lam_init),
        grid=(bsz, H_B, nq),
        in_specs=[
            pl.BlockSpec((LANES, t), lambda b, h, qi: (h, b * nq + qi)),
            pl.BlockSpec((s, LANES), lambda b, h, qi: (b, kb0 + h)),
            pl.BlockSpec((DV_B, s), lambda b, h, qi: (vb0 + h, b)),
            pl.BlockSpec((5, None, t, t), lambda b, h, qi: (0, h, 0, 0)),
            pl.BlockSpec((4, DH_B), lambda b, h, qi: (0, 0)),
            pl.BlockSpec((1, DV_B), lambda b, h, qi: (0, 0)),
        ],
        out_specs=pl.BlockSpec((t, DV_B), lambda b, h, qi: (b * nq + qi, h)),
        out_shape=jax.ShapeDtypeStruct((bsz * s, V_B), BF16),
        scratch_shapes=[
            pltpu.VMEM((LANES, 2 * t), BF16),
            pltpu.VMEM((DV_B + ONES_ROWS, 2 * t), F32),
        ],
        compiler_params=_params(("parallel", "parallel", "parallel")),
        name="diff_attention",
    )(pt, p, pt, bias, diff_lambda, subln.reshape(1, DV_B))


def _rope_k_kernel(c_ref, cs_ref, o_ref):
    tmp = c_ref[...] * cs_ref[...]
    y = tmp + pltpu.roll(tmp, ROPE_C, axis=1)
    lane = lax.broadcasted_iota(jnp.int32, y.shape, 1)
    o_ref[...] = jnp.where(lane < ROPE_C, y, 0.0).astype(o_ref.dtype)


def rope_k(c, cs, bsz, s):
    tr = min(512, s)
    nb = s // tr
    cb = (Q_LORA + KV_LORA) // LANES
    return pl.pallas_call(
        _rope_k_kernel,
        grid=(bsz, nb),
        in_specs=[
            pl.BlockSpec((tr, LANES), lambda b, i: (b * nb + i, cb)),
            pl.BlockSpec((tr, LANES), lambda b, i: (i, 0)),
        ],
        out_specs=pl.BlockSpec((tr, LANES), lambda b, i: (b * nb + i, 0)),
        out_shape=jax.ShapeDtypeStruct((bsz * s, LANES), BF16),
        compiler_params=_params(("parallel", "parallel")),
        name="rope_k",
    )(c, cs)


def _mla_attn_kernel(qa_ref, qb_ref, cs_ref, kn_ref, kr_ref, vt_ref, o_ref, qs_ref, acc_ref, *, tk, n_tiles):
    scale = (NOPE_C + ROPE_C) ** -0.5 * LOG2E
    tmp = qb_ref[...].astype(F32) * cs_ref[...]
    qr = tmp + pltpu.roll(tmp, ROPE_C, axis=0)
    qs_ref[0:LANES, :] = (qa_ref[...].astype(F32) * scale).astype(BF16)
    qs_ref[LANES:2 * LANES, :] = (qr * scale).astype(BF16)

    kcs = {}

    def scores(j, cols):
        if j not in kcs:
            rows = slice(j * tk, (j + 1) * tk)
            kcs[j] = jnp.concatenate([kn_ref[rows, :], kr_ref[rows, :]], axis=-1)
        return _dot(kcs[j], qs_ref[:, cols])

    l = _online_softmax(n_tiles, scores, lambda j: vt_ref[:, j * tk:(j + 1) * tk], acc_ref, qs_ref.shape[1])
    o_ref[...] = (acc_ref[0:DV_C, :] / l).T.astype(o_ref.dtype)


def mla_attention(qt, kn, krp, vt, cst, bsz, s):
    tq = min(MLA_TQ, s)
    tk = min(MLA_TK, s)
    nq = s // tq
    return pl.pallas_call(
        functools.partial(_mla_attn_kernel, tk=tk, n_tiles=s // tk),
        grid=(bsz, H_C, nq),
        in_specs=[
            pl.BlockSpec((LANES, tq), lambda b, h, qi: (2 * h, b * nq + qi)),
            pl.BlockSpec((LANES, tq), lambda b, h, qi: (2 * h + 1, b * nq + qi)),
            pl.BlockSpec((LANES, tq), lambda b, h, qi: (0, qi)),
            pl.BlockSpec((s, LANES), lambda b, h, qi: (b, h)),
            pl.BlockSpec((s, LANES), lambda b, h, qi: (b, 0)),
            pl.BlockSpec((DV_C, s), lambda b, h, qi: (h, b)),
        ],
        out_specs=pl.BlockSpec((tq, DV_C), lambda b, h, qi: (b * nq + qi, h)),
        out_shape=jax.ShapeDtypeStruct((bsz * s, H_C * DV_C), BF16),
        scratch_shapes=[
            pltpu.VMEM((2 * LANES, tq), BF16),
            pltpu.VMEM((DV_C + ONES_ROWS, tq), F32),
        ],
        compiler_params=_params(("parallel", "parallel", "parallel")),
        name="mla_attention",
    )(qt, qt, cst, kn, krp, vt)


def _rotate_half_cols(w):
    half = w.shape[-1] // 2
    return jnp.concatenate([-w[..., half:], w[..., :half]], axis=-1)


def _prepare_weights(w_in_e, a_log, dt_bias, w_out_e, w_in_c, w_qb, w_out_c, w_mlp1, w_mlp2, w_kvb):
    g0 = 2 * QK_A + 2 * V_A
    prep = {}
    b0 = g0 + 4 * H_A
    t_last = lambda w: jnp.swapaxes(w, -1, -2)
    prep["w_e_main"] = jnp.concatenate([w_in_e[:, :, :g0], w_in_e[:, :, b0 + QK_B:b0 + 2 * QK_B]],
                                       axis=-1).astype(BF16)
    prep["w_e_t"] = t_last(jnp.concatenate([w_in_e[:, :, b0:b0 + QK_B], w_in_e[:, :, b0 + 2 * QK_B:]],
                                           axis=-1)).astype(BF16)
    prep["w_e_gate"] = jnp.pad(w_in_e[:, :, g0:g0 + 4 * H_A], ((0, 0), (0, 0), (0, LANES - 4 * H_A))).astype(BF16)
    n_even = a_log.shape[0]
    pad_row = lambda v: jnp.pad(v.reshape(n_even, 1, 2 * H_A).astype(F32),
                                ((0, 0), (0, 0), (2 * H_A, LANES - 4 * H_A)))
    prep["alog_row"] = pad_row(a_log)
    prep["dtb_row"] = pad_row(dt_bias)
    prep["w_out_e"] = w_out_e.astype(BF16)
    k_rope_w = w_in_c[:, :, Q_LORA + KV_LORA:]
    prep["w_c"] = jnp.concatenate([w_in_c, _rotate_half_cols(k_rope_w)], axis=-1).astype(BF16)
    n_odd = w_qb.shape[0]
    wq = w_qb.reshape(n_odd, Q_LORA, H_C, NOPE_C + ROPE_C)
    wq_rope = wq[..., NOPE_C:]
    prep["w_qb_t"] = t_last(jnp.concatenate([wq[..., :NOPE_C], wq_rope, _rotate_half_cols(wq_rope)], axis=-1).reshape(
        n_odd, Q_LORA, H_C * 2 * LANES)).astype(BF16)
    wkv = w_kvb.reshape(n_odd, KV_LORA, H_C, NOPE_C + DV_C)
    prep["w_kn"] = wkv[..., :NOPE_C].reshape(n_odd, KV_LORA, H_C * NOPE_C).astype(BF16)
    prep["w_v_t"] = t_last(wkv[..., NOPE_C:].reshape(n_odd, KV_LORA, H_C * DV_C)).astype(BF16)
    prep["w_out_c"] = w_out_c.astype(BF16)
    prep["w_mlp1"] = w_mlp1.astype(BF16)
    prep["w_mlp2"] = w_mlp2.astype(BF16)
    return prep


def _rope_table(s):
    inv_freq = ROPE_THETA ** (-jnp.arange(0, ROPE_C, 2, dtype=F32) / ROPE_C)
    ang = jnp.arange(s, dtype=F32)[:, None] * inv_freq[None, :]
    cos, sin = jnp.cos(ang), jnp.sin(ang)
    return jnp.concatenate([cos, cos, sin, sin], axis=-1)


def _trunk(x, pw, bias, norm_mix, norm_mlp, norm_final, conv_w, gdn_norm, diff_lambda, subln, q_norm, kv_norm):
    bsz, s, d = x.shape
    x = x.reshape(bsz * s, d)
    cs = _rope_table(s)
    cst = cs.T
    for layer in range(DEPTH):
        i = layer // 2
        if layer % 2 == 0:
            p, gates = norm_matmul_aux(x, norm_mix[layer], pw["w_e_main"][i], pw["w_e_gate"][i], BF16, 1024, 1024)
            pt = norm_matmul_t(x, 0, d, norm_mix[layer], pw["w_e_t"][i], BF16, 1024, 1024)
            prep = gdn_prep(p, gates, conv_w[i], pw["alog_row"][i], pw["dtb_row"][i], bsz, s)
            o_a = gdn_scan(prep, p, gdn_norm[i], bsz, s)
            lam_init = 0.8 - 0.6 * math.exp(-0.3 * layer)
            o_b = diff_attention(p, pt, bias, diff_lambda[i], subln[i], lam_init, bsz, s)
            x = matmul_res([o_a, o_b], pw["w_out_e"][i], x, 1024, 1024)
        else:
            c = norm_matmul(x, 0, d, norm_mix[layer], pw["w_c"][i], F32, 1024, pw["w_c"].shape[-1])
            qt = norm_matmul_t(c, 0, Q_LORA, q_norm[i], pw["w_qb_t"][i], BF16, 1024, 1024)
            kn = norm_matmul(c, 1, KV_LORA, kv_norm[i], pw["w_kn"][i], BF16, 1024, 1024)
            vt = norm_matmul_t(c, 1, KV_LORA, kv_norm[i], pw["w_v_t"][i], BF16, 1024, 1024)
            krp = rope_k(c, cs, bsz, s)
            o_c = mla_attention(qt, kn, krp, vt, cst, bsz, s)
            x = matmul_res([o_c], pw["w_out_c"][i], x, 1024, 1024)
        x = mlp(x, norm_mlp[layer], pw["w_mlp1"][layer], pw["w_mlp2"][layer], norm_final,
                layer == DEPTH - 1, 1024, 512)
    return x.reshape(bsz, s, d)


def kernel(x_prompt, x_sample, norm_mix, norm_mlp, norm_final, rel_bias, w_in_e, conv_w, a_log, dt_bias, gdn_norm, diff_lambda, subln, w_out_e, w_in_c, q_norm, w_qb, kv_norm, w_kvb, w_out_c, w_mlp1, w_mlp2):
    pw = _prepare_weights(w_in_e, a_log, dt_bias, w_out_e, w_in_c, w_qb, w_out_c, w_mlp1, w_mlp2, w_kvb)
    bias = bias_tiles(rel_bias)
    run = lambda x: _trunk(x, pw, bias, norm_mix, norm_mlp, norm_final, conv_w, gdn_norm, diff_lambda,
                           subln, q_norm, kv_norm)
    return (run(x_prompt), run(x_sample))
```

```python
import functools
import math

import jax
import jax.numpy as jnp
from jax import lax
from jax.experimental import pallas as pl
from jax.experimental.pallas import tpu as pltpu

F32 = jnp.float32
BF16 = jnp.bfloat16

D_MODEL = 2048
DEPTH = 4
H_A = 8
DK_A = 128
DV_A = 128
QK_A = H_A * DK_A
V_A = H_A * DV_A
H_B = 8
DH_B = 64
DV_B = 128
QK_B = H_B * 2 * DH_B
V_B = H_B * DV_B
N_BUCKETS = 32
MAX_DIST = 128
H_C = 16
Q_LORA = 512
KV_LORA = 512
NOPE_C = 128
ROPE_C = 64
DV_C = 128
ROPE_THETA = 10000.0
D_FF = 4 * D_MODEL
EPS = 1e-6

LOG2E = math.log2(math.e)
LANES = 128
BF16_SUBLANES = 16
VMEM_LIMIT = 56 * 1024 * 1024

GDN_PREP_BLOCK = 128
GDN_SCAN_BLOCK = 256
GDN_CHUNK = 64
GDN_PREP_HEADS = 8
GDN_SCAN_HEADS = 4
DIFF_TILE = 512
MLA_TQ = 2048
MLA_TK = 512
ATT_PART = 512
ONES_ROWS = BF16_SUBLANES


def _params(sem):
    return pltpu.CompilerParams(dimension_semantics=sem, vmem_limit_bytes=VMEM_LIMIT)


def _dot(a, b):
    return jnp.dot(a, b, preferred_element_type=F32)


def _dot_nt(a, b):
    return lax.dot_general(a, b, (((1,), (1,)), ((), ())), preferred_element_type=F32)


def _rms(x, g):
    ms = jnp.mean(x * x, axis=-1, keepdims=True)
    return x * lax.rsqrt(ms + EPS) * g


def _norm_matmul_kernel(x_ref, g_ref, w_ref, o_ref, h_ref):
    @pl.when(pl.program_id(1) == 0)
    def _():
        h_ref[...] = _rms(x_ref[...].astype(F32), g_ref[...]).astype(BF16)

    o_ref[...] = _dot(h_ref[...], w_ref[...]).astype(o_ref.dtype)


def norm_matmul(x, kblock, k, g, w, out_dtype, tm, tn):
    t = x.shape[0]
    n = w.shape[1]
    tm = min(tm, t)
    tn = min(tn, n)
    return pl.pallas_call(
        _norm_matmul_kernel,
        grid=(t // tm, n // tn),
        in_specs=[
            pl.BlockSpec((tm, k), lambda i, j: (i, kblock)),
            pl.BlockSpec((1, k), lambda i, j: (0, 0)),
            pl.BlockSpec((k, tn), lambda i, j: (0, j)),
        ],
        out_specs=pl.BlockSpec((tm, tn), lambda i, j: (i, j)),
        out_shape=jax.ShapeDtypeStruct((t, n), out_dtype),
        scratch_shapes=[pltpu.VMEM((tm, k), BF16)],
        compiler_params=_params(("parallel", "arbitrary")),
        name="norm_matmul",
    )(x, g.reshape(1, k).astype(F32), w)


def _norm_matmul_aux_kernel(x_ref, g_ref, w_ref, wa_ref, o_ref, oa_ref, h_ref):
    @pl.when(pl.program_id(1) == 0)
    def _():
        h = _rms(x_ref[...].astype(F32), g_ref[...]).astype(BF16)
        h_ref[...] = h
        oa_ref[...] = _dot(h, wa_ref[...])

    o_ref[...] = _dot(h_ref[...], w_ref[...]).astype(o_ref.dtype)


def norm_matmul_aux(x, g, w, w_aux, out_dtype, tm, tn):
    t, k = x.shape
    n = w.shape[1]
    tm = min(tm, t)
    tn = min(tn, n)
    return pl.pallas_call(
        _norm_matmul_aux_kernel,
        grid=(t // tm, n // tn),
        in_specs=[
            pl.BlockSpec((tm, k), lambda i, j: (i, 0)),
            pl.BlockSpec((1, k), lambda i, j: (0, 0)),
            pl.BlockSpec((k, tn), lambda i, j: (0, j)),
            pl.BlockSpec((k, LANES), lambda i, j: (0, 0)),
        ],
        out_specs=[pl.BlockSpec((tm, tn), lambda i, j: (i, j)),
                   pl.BlockSpec((tm, LANES), lambda i, j: (i, 0))],
        out_shape=[jax.ShapeDtypeStruct((t, n), out_dtype), jax.ShapeDtypeStruct((t, LANES), F32)],
        scratch_shapes=[pltpu.VMEM((tm, k), BF16)],
        compiler_params=_params(("parallel", "arbitrary")),
        name="norm_matmul_aux",
    )(x, g.reshape(1, k).astype(F32), w, w_aux)


def _norm_matmul_t_kernel(x_ref, g_ref, wt_ref, o_ref, h_ref):
    @pl.when(pl.program_id(1) == 0)
    def _():
        h_ref[...] = _rms(x_ref[...].astype(F32), g_ref[...]).astype(BF16)

    o_ref[...] = _dot_nt(wt_ref[...], h_ref[...]).astype(o_ref.dtype)


def norm_matmul_t(x, kblock, k, g, wt, out_dtype, tm, tn):
    t = x.shape[0]
    n = wt.shape[0]
    tm = min(tm, t)
    tn = min(tn, n)
    return pl.pallas_call(
        _norm_matmul_t_kernel,
        grid=(t // tm, n // tn),
        in_specs=[
            pl.BlockSpec((tm, k), lambda i, j: (i, kblock)),
            pl.BlockSpec((1, k), lambda i, j: (0, 0)),
            pl.BlockSpec((tn, k), lambda i, j: (j, 0)),
        ],
        out_specs=pl.BlockSpec((tn, tm), lambda i, j: (j, i)),
        out_shape=jax.ShapeDtypeStruct((n, t), out_dtype),
        scratch_shapes=[pltpu.VMEM((tm, k), BF16)],
        compiler_params=_params(("parallel", "arbitrary")),
        name="norm_matmul_t",
    )(x, g.reshape(1, k).astype(F32), wt)


def _matmul_res_kernel(*refs, widths):
    n = len(widths)
    w_ref, r_ref, o_ref = refs[n:]
    acc = r_ref[...]
    off = 0
    for a_ref, wd in zip(refs[:n], widths):
        acc = acc + _dot(a_ref[...], w_ref[off:off + wd, :])
        off += wd
    o_ref[...] = acc


def matmul_res(parts, w, res, tm, tn):
    t = res.shape[0]
    n = w.shape[1]
    tm = min(tm, t)
    widths = tuple(p.shape[1] for p in parts)
    return pl.pallas_call(
        functools.partial(_matmul_res_kernel, widths=widths),
        grid=(t // tm, n // tn),
        in_specs=[pl.BlockSpec((tm, wd), lambda i, j: (i, 0)) for wd in widths]
        + [
            pl.BlockSpec((sum(widths), tn), lambda i, j: (0, j)),
            pl.BlockSpec((tm, tn), lambda i, j: (i, j)),
        ],
        out_specs=pl.BlockSpec((tm, tn), lambda i, j: (i, j)),
        out_shape=jax.ShapeDtypeStruct((t, n), F32),
        compiler_params=_params(("parallel", "parallel")),
        name="matmul_res",
    )(*parts, w, res)


def _mlp_kernel(x_ref, g_ref, w1_ref, w2_ref, gf_ref, o_ref, h_ref, *, nf, final_norm):
    f = pl.program_id(1)

    @pl.when(f == 0)
    def _():
        x = x_ref[...]
        h_ref[...] = _rms(x, g_ref[...]).astype(BF16)
        o_ref[...] = x

    a = _dot(h_ref[...], w1_ref[...])
    a = jnp.square(jnp.maximum(a, 0.0)).astype(BF16)
    o_ref[...] += _dot(a, w2_ref[...])

    if final_norm:
        @pl.when(f == nf - 1)
        def _():
            o_ref[...] = _rms(o_ref[...], gf_ref[...])


def mlp(x, g, w1, w2, g_final, final_norm, tm, tf):
    t, d = x.shape
    tm = min(tm, t)
    nf = D_FF // tf
    return pl.pallas_call(
        functools.partial(_mlp_kernel, nf=nf, final_norm=final_norm),
        grid=(t // tm, nf),
        in_specs=[
            pl.BlockSpec((tm, d), lambda i, f: (i, 0)),
            pl.BlockSpec((1, d), lambda i, f: (0, 0)),
            pl.BlockSpec((d, tf), lambda i, f: (0, f)),
            pl.BlockSpec((tf, d), lambda i, f: (f, 0)),
            pl.BlockSpec((1, d), lambda i, f: (0, 0)),
        ],
        out_specs=pl.BlockSpec((tm, d), lambda i, f: (i, 0)),
        out_shape=jax.ShapeDtypeStruct((t, d), F32),
        scratch_shapes=[pltpu.VMEM((tm, d), BF16)],
        compiler_params=_params(("parallel", "arbitrary")),
        name="mlp",
    )(x, g.reshape(1, d), w1, w2, g_final.reshape(1, d))


def _softplus(x):
    return jnp.maximum(x, 0.0) + jnp.log1p(jnp.exp(-jnp.abs(x)))


def _gdn_prep_kernel(q_ref, qp_ref, qn_ref, k_ref, kp_ref, kn_ref, v_ref, vp_ref, vn_ref,
                     cwq_ref, cwk_ref, cwv_ref, gt_ref, alog_ref, dtb_ref,
                     qd_ref, kd_ref, w_ref, u_ref, in_ref, egl_ref, *, tb, cc, n):
    hp = pl.program_id(1)
    i = pl.program_id(2)
    width = GDN_PREP_HEADS * LANES
    roww = lax.broadcasted_iota(jnp.int32, (tb, width), 0)

    def conv_silu(x_ref, p_ref, n_ref, cw_ref):
        x = x_ref[...].astype(F32)
        prev = jnp.where(i == 0, 0.0, p_ref[BF16_SUBLANES - 1:BF16_SUBLANES, :].astype(F32))
        nxt = jnp.where(i == n - 1, 0.0, n_ref[0:1, :].astype(F32))
        xm = jnp.where(roww == 0, prev, pltpu.roll(x, 1, axis=0))
        xp = jnp.where(roww == tb - 1, nxt, pltpu.roll(x, tb - 1, axis=0))
        cw = cw_ref[...]
        y = xm * cw[0:1, :] + x * cw[1:2, :] + xp * cw[2:3, :]
        return y * jax.nn.sigmoid(y)

    q_all = conv_silu(q_ref, qp_ref, qn_ref, cwq_ref)
    k_all = conv_silu(k_ref, kp_ref, kn_ref, cwk_ref)
    v_all = conv_silu(v_ref, vp_ref, vn_ref, cwv_ref)

    row = lax.broadcasted_iota(jnp.int32, (tb, LANES), 0)
    lane = lax.broadcasted_iota(jnp.int32, (tb, LANES), 1)
    rin = row & (cc - 1)
    gt = gt_ref[...]
    beta_all = jax.nn.sigmoid(gt)
    g_all = -jnp.exp(alog_ref[...]) * _softplus(gt + dtb_ref[...])
    pre = g_all
    suf = g_all
    s = 1
    while s < cc:
        pre = pre + jnp.where(rin >= s, pltpu.roll(pre, s, axis=0), 0.0)
        suf = suf + jnp.where(rin < cc - s, pltpu.roll(suf, tb - s, axis=0), 0.0)
        s *= 2
    tot = pre + suf - g_all
    pre_t = pre.T
    suf_t = suf.T
    sub_t = lax.broadcasted_iota(jnp.int32, (LANES, tb), 0)

    def col(x, j):
        return jnp.broadcast_to(jnp.sum(jnp.where(lane == j, x, 0.0), axis=1, keepdims=True), (tb, LANES))

    def wide(c):
        return jnp.concatenate([c] * (tb // LANES), axis=1)

    def rowv(xt, j):
        return jnp.sum(jnp.where(sub_t == j, xt, 0.0), axis=0, keepdims=True)

    lgc = cc.bit_length() - 1
    ii = lax.broadcasted_iota(jnp.int32, (tb, tb), 0)
    jj = lax.broadcasted_iota(jnp.int32, (tb, tb), 1)
    same = (ii >> lgc) == (jj >> lgc)
    incl = (same & (ii >= jj), same & (ii <= jj))
    offdiag = ii != jj
    eye = (ii == jj).astype(F32)
    pair = (ii >> 1) == (jj >> 1)
    levels = [((ii >> (lg + 1)) == (jj >> (lg + 1))) & ((ii >> lg) != (jj >> lg)) for lg in range(1, lgc)]

    chains = []
    for e in range(GDN_PREP_HEADS):
        h = hp * GDN_PREP_HEADS + e
        cols = slice(e * LANES, (e + 1) * LANES)
        q, k, v = q_all[:, cols], k_all[:, cols], v_all[:, cols]
        sumsq = lambda t: jnp.broadcast_to(jnp.sum(t * t, axis=-1, keepdims=True), (tb, LANES))
        q = q * (lax.rsqrt(sumsq(q) + EPS) * (DK_A ** -0.5))
        k = k * lax.rsqrt(sumsq(k) + EPS)
        k16 = k.astype(BF16)
        qk = _dot_nt(q.astype(BF16), k16)
        kk = _dot_nt(k16, k16)
        for d in range(2):
            jb = d * H_A + h
            jg = 2 * H_A + d * H_A + h
            gcol, grow = (col(pre, jg), rowv(pre_t, jg)) if d == 0 else (col(suf, jg), rowv(suf_t, jg))
            beta = col(beta_all, jb)
            tcol = col(tot, jg)
            dec = jnp.where(incl[d], jnp.exp(jnp.where(incl[d], wide(gcol) - grow, 0.0)), 0.0)
            eg = jnp.exp(gcol)
            a = jnp.where(offdiag, kk * dec, 0.0) * wide(beta)
            m = qk * dec
            m = sum(m[:, j * LANES:(j + 1) * LANES] for j in range(tb // LANES))
            in_ref[d, e] = m.astype(BF16)
            qd_ref[d, e] = (q * eg).astype(BF16)
            kd_ref[d, e] = (k * jnp.exp(tcol - gcol)).astype(BF16)
            eb = jnp.exp(tcol)
            for c in range(tb // cc):
                egl_ref[d, e, c] = jnp.broadcast_to(eb[c * cc:c * cc + 1, :], (8, LANES))
            rhs = jnp.concatenate([v * beta, k * (beta * eg)], axis=1).astype(BF16)
            chains.append((d, e, a, rhs))

    nas = [-a for (_, _, a, _) in chains]
    xs = [(eye + jnp.where(pair, na, 0.0)).astype(BF16) for na in nas]
    for off in levels:
        ys = [_dot(jnp.where(off, na, 0.0).astype(BF16), x) for na, x in zip(nas, xs)]
        xs = [jnp.where(off, _dot(x, y.astype(BF16)).astype(BF16), x) for x, y in zip(xs, ys)]
    for (d, e, _, rhs), x in zip(chains, xs):
        uw = _dot(x, rhs)
        u_ref[d, e] = uw[:, 0:DV_A]
        w_ref[d, e] = uw[:, DV_A:].astype(BF16)


def gdn_prep(p, gates, conv_w, alog_row, dtb_row, bsz, s):
    tb, cc, hps = GDN_PREP_BLOCK, GDN_CHUNK, GDN_PREP_HEADS
    assert tb % LANES == 0 and cc == LANES // 2
    n = s // tb
    t = bsz * s
    width = hps * LANES
    rpb = tb // BF16_SUBLANES
    last_blk = t // BF16_SUBLANES - 1

    def main(cb):
        return pl.BlockSpec((tb, width), lambda b, hp, i: (b * n + i, cb // hps + hp))

    def prev(cb):
        return pl.BlockSpec((BF16_SUBLANES, width),
                            lambda b, hp, i: (jnp.maximum((b * n + i) * rpb - 1, 0), cb // hps + hp))

    def nxt(cb):
        return pl.BlockSpec((BF16_SUBLANES, width),
                            lambda b, hp, i: (jnp.minimum((b * n + i + 1) * rpb, last_blk), cb // hps + hp))

    def cw(cb):
        return pl.BlockSpec((3, width), lambda b, hp, i: (0, cb // hps + hp))

    row_spec = pl.BlockSpec((1, LANES), lambda b, hp, i: (0, 0))
    seq_out = lambda last: pl.BlockSpec((2, None, hps, tb, last), lambda b, hp, i: (0, b, hp, i, 0))
    out_shape = [
        jax.ShapeDtypeStruct((2, bsz, H_A, s, DK_A), BF16),
        jax.ShapeDtypeStruct((2, bsz, H_A, s, DK_A), BF16),
        jax.ShapeDtypeStruct((2, bsz, H_A, s, DK_A), BF16),
        jax.ShapeDtypeStruct((2, bsz, H_A, s, DV_A), F32),
        jax.ShapeDtypeStruct((2, bsz, H_A, s, LANES), BF16),
        jax.ShapeDtypeStruct((2, bsz, H_A, s // cc, 8, LANES), F32),
    ]
    out_specs = [
        seq_out(DK_A), seq_out(DK_A), seq_out(DK_A), seq_out(DV_A), seq_out(LANES),
        pl.BlockSpec((2, None, hps, tb // cc, 8, LANES), lambda b, hp, i: (0, b, hp, i, 0, 0)),
    ]
    kb0, vb0 = QK_A // LANES, 2 * QK_A // LANES
    return pl.pallas_call(
        functools.partial(_gdn_prep_kernel, tb=tb, cc=cc, n=n),
        grid=(bsz, H_A // hps, n),
        in_specs=[main(0), prev(0), nxt(0), main(kb0), prev(kb0), nxt(kb0), main(vb0), prev(vb0), nxt(vb0),
                  cw(0), cw(kb0), cw(vb0),
                  pl.BlockSpec((tb, LANES), lambda b, hp, i: (b * n + i, 0)),
                  row_spec, row_spec],
        out_specs=out_specs,
        out_shape=out_shape,
        compiler_params=_params(("parallel", "parallel", "parallel")),
        name="gdn_prep",
    )(p, p, p, p, p, p, p, p, p, conv_w, conv_w, conv_w, gates, alog_row, dtb_row)


def _gdn_scan_kernel(qdf, kdf, wf, uf, inf, eglf, qdb, kdb, wb, ub, inb, eglb, z_ref, gn_ref,
                     o_ref, st_ref, acc_ref, *, tb, cc, n, hps):
    i = pl.program_id(2)
    nc = tb // cc

    @pl.when(i == 0)
    def _():
        st_ref[...] = jnp.zeros_like(st_ref)
        acc_ref[...] = jnp.zeros_like(acc_ref)

    fwd = (qdf, kdf, wf, uf, inf, eglf)
    bwd = (qdb, kdb, wb, ub, inb, eglb)
    chains = [(d, e) for e in range(hps) for d in range(2)]
    st = {ch: st_ref[ch[0], ch[1]] for ch in chains}
    for step in range(nc):
        stage = {}
        for d, e in chains:
            qd, kd, w, u, intra, egl = fwd if d == 0 else bwd
            c = step if d == 0 else nc - 1 - step
            rows = slice(c * cc, (c + 1) * cc)
            st16 = st[d, e].astype(BF16)
            stage[d, e] = (c, rows, _dot(w[e, rows, :], st16), _dot(qd[e, rows, :], st16))
        for d, e in chains:
            qd, kd, w, u, intra, egl = fwd if d == 0 else bwd
            c, rows, ws, qs = stage[d, e]
            v16 = (u[e, rows, :] - ws).astype(BF16)
            o = qs + _dot(intra[e, rows, :], jnp.concatenate([v16] * (LANES // cc), axis=0))
            kv = lax.dot_general(kd[e, rows, :], v16, (((0,), (0,)), ((), ())), preferred_element_type=F32)
            st[d, e] = st[d, e] * egl[e, c, 0:1, :] + kv
            row0 = i * tb if d == 0 else (n - 1 - i) * tb
            r0 = pl.multiple_of(row0 + c * cc, cc)
            acc_ref[pl.ds(r0, cc), e * DV_A:(e + 1) * DV_A] += o
    for d, e in chains:
        st_ref[d, e] = st[d, e]

    @pl.when(i == n - 1)
    def _():
        for e in range(hps):
            cols = slice(e * DV_A, (e + 1) * DV_A)
            z = z_ref[:, cols].astype(F32)
            y = _rms(acc_ref[:, cols], gn_ref[...]) * (z * jax.nn.sigmoid(z))
            o_ref[:, cols] = y.astype(o_ref.dtype)


def gdn_scan(prep, p, gdn_norm, bsz, s):
    qd, kd, w, u, intra, egl = prep
    tb, cc, hps = GDN_SCAN_BLOCK, GDN_CHUNK, GDN_SCAN_HEADS
    n = s // tb

    def seq(d, last):
        if d == 0:
            return pl.BlockSpec((None, None, hps, tb, last), lambda b, hp, i: (0, b, hp, i, 0))
        return pl.BlockSpec((None, None, hps, tb, last), lambda b, hp, i: (1, b, hp, n - 1 - i, 0))

    def egl_spec(d):
        blk = (None, None, hps, tb // cc, 8, LANES)
        if d == 0:
            return pl.BlockSpec(blk, lambda b, hp, i: (0, b, hp, i, 0, 0))
        return pl.BlockSpec(blk, lambda b, hp, i: (1, b, hp, n - 1 - i, 0, 0))

    def dir_specs(d):
        return [seq(d, DK_A), seq(d, DK_A), seq(d, DK_A), seq(d, DV_A), seq(d, LANES), egl_spec(d)]

    zb0 = (2 * QK_A + V_A) // (hps * LANES)
    return pl.pallas_call(
        functools.partial(_gdn_scan_kernel, tb=tb, cc=cc, n=n, hps=hps),
        grid=(bsz, H_A // hps, n),
        in_specs=dir_specs(0) + dir_specs(1) + [
            pl.BlockSpec((s, hps * LANES), lambda b, hp, i: (b, zb0 + hp)),
            pl.BlockSpec((1, DV_A), lambda b, hp, i: (0, 0)),
        ],
        out_specs=pl.BlockSpec((s, hps * DV_A), lambda b, hp, i: (b, hp)),
        out_shape=jax.ShapeDtypeStruct((bsz * s, V_A), BF16),
        scratch_shapes=[pltpu.VMEM((2, hps, DK_A, DV_A), F32), pltpu.VMEM((s, hps * DV_A), F32)],
        compiler_params=_params(("parallel", "parallel", "arbitrary")),
        name="gdn_scan",
    )(qd, kd, w, u, intra, egl, qd, kd, w, u, intra, egl, p, gdn_norm.reshape(1, DV_A))


def _bucket_thresholds():
    nb = N_BUCKETS // 2
    max_exact = nb // 2
    out = []
    prev = 0
    for dist in range(max_exact, MAX_DIST):
        val = int(math.log(dist / max_exact) / math.log(MAX_DIST / max_exact) * (nb - max_exact))
        val = min(val, nb - 1 - max_exact)
        if val > prev:
            out.extend([dist] * (val - prev))
            prev = val
    return tuple(out)


def _bias_kernel(rb_ref, o_ref, *, t, thresholds):
    d = pl.program_id(0) - 2
    h = pl.program_id(1)
    nb = N_BUCKETS // 2
    max_exact = nb // 2

    @pl.when(jnp.abs(d) >= 2)
    def _():
        far = jnp.where(d > 0, rb_ref[N_BUCKETS - 1, h], rb_ref[nb - 1, h])
        o_ref[...] = jnp.full((t, t), far * LOG2E, F32)

    @pl.when(jnp.abs(d) < 2)
    def _():
        r = lax.broadcasted_iota(jnp.int32, (t, t), 0)
        c = lax.broadcasted_iota(jnp.int32, (t, t), 1)
        rel = d * t + r - c
        dist = jnp.abs(rel)
        large = jnp.full((t, t), max_exact, jnp.int32)
        for th in thresholds:
            large = large + (dist >= th).astype(jnp.int32)
        bucket = jnp.where(rel > 0, nb, 0) + jnp.where(dist < max_exact, dist, large)
        out = jnp.zeros((t, t), F32)
        for b in range(N_BUCKETS):
            out = jnp.where(bucket == b, rb_ref[b, h], out)
        o_ref[...] = out * LOG2E


def bias_tiles(rel_bias):
    t = DIFF_TILE
    return pl.pallas_call(
        functools.partial(_bias_kernel, t=t, thresholds=_bucket_thresholds()),
        grid=(5, H_B),
        in_specs=[pl.BlockSpec(memory_space=pltpu.SMEM)],
        out_specs=pl.BlockSpec((None, None, t, t), lambda d, h: (d, h, 0, 0)),
        out_shape=jax.ShapeDtypeStruct((5, H_B, t, t), F32),
        compiler_params=_params(("parallel", "parallel")),
        name="t5_bias_tiles",
    )(rel_bias)


def _online_softmax(n_tiles, scores, vt, acc_ref, nq, part, shift=None):
    part = min(part, nq)
    parts = [slice(c, c + part) for c in range(0, nq, part)]
    m = [None] * len(parts)
    dv = acc_ref.shape[0] - ONES_ROWS
    sts = [scores(0, cols) for cols in parts]
    for j in range(n_tiles):
        nxt = [scores(j + 1, cols) for cols in parts] if j + 1 < n_tiles else None
        c = shift(j) if shift is not None else None
        pts = []
        for i, st in enumerate(sts):
            mx = jnp.max(st, axis=0, keepdims=True)
            if c is not None:
                mx = mx + c
            if j == 0:
                m[i], alpha = mx, None
            else:
                m_new = jnp.maximum(m[i], mx)
                alpha = jnp.exp2(m[i] - m_new)
                m[i] = m_new
            ref = m[i] if c is None else m[i] - c
            pts.append((alpha, jnp.exp2(st - ref).astype(BF16)))
        v = vt(j)
        v1 = jnp.concatenate([v, jnp.ones((ONES_ROWS, v.shape[1]), BF16)], axis=0)
        for cols, (alpha, pt16) in zip(parts, pts):
            pv = _dot(v1, pt16)
            acc_ref[:, cols] = pv if j == 0 else alpha * acc_ref[:, cols] + pv
        sts = nxt
    return acc_ref[dv:dv + 1, :]


def _diff_attn_kernel(q_ref, k_ref, vt_ref, b_ref, dl_ref, sg_ref, o_ref, qs_ref, acc_ref, *, t, n_tiles, lam_init):
    qi = pl.program_id(2)
    q = q_ref[...].astype(F32) * (DH_B ** -0.5 * LOG2E)
    sub = lax.broadcasted_iota(jnp.int32, (LANES, t), 0)
    qs_ref[:, 0:t] = jnp.where(sub < DH_B, q, 0.0).astype(BF16)
    qs_ref[:, t:2 * t] = jnp.where(sub >= DH_B, q, 0.0).astype(BF16)

    n_near = min(3, n_tiles)

    def tile_of(r):
        return lax.rem(qi + (n_tiles - 1 + r), n_tiles)

    def start(r):
        return pl.multiple_of(tile_of(r) * t, t)

    def scores(r, cols):
        st = _dot(k_ref[pl.ds(start(r), t), :], qs_ref[:, cols])
        if r >= n_near:
            return st
        b0 = cols.start % t
        width = cols.stop - cols.start
        tile = jnp.clip(tile_of(r) - qi, -2, 2) + 2
        return st + b_ref[tile, :, b0:b0 + width]

    def shift(r):
        if r < n_near:
            return None
        return jnp.where(tile_of(r) > qi, b_ref[4, 0:1, 0:1], b_ref[0, 0:1, 0:1])

    l = _online_softmax(n_tiles, scores, lambda r: vt_ref[:, pl.ds(start(r), t)], acc_ref, 2 * t, min(ATT_PART, t), shift)

    dl = dl_ref[...]
    lam = (jnp.exp(jnp.sum(dl[0:1, :] * dl[1:2, :], axis=-1, keepdims=True))
           - jnp.exp(jnp.sum(dl[2:3, :] * dl[3:4, :], axis=-1, keepdims=True)) + lam_init)
    a = acc_ref[0:DV_B, :] / l
    o = (a[:, 0:t] - lam * a[:, t:2 * t]).T
    o_ref[...] = (_rms(o, sg_ref[...]) * (1.0 - lam_init)).astype(o_ref.dtype)


def diff_attention(p, pt, bias, diff_lambda, subln, lam_init, bsz, s):
    t = min(DIFF_TILE, s)
    assert t >= MAX_DIST
    nq = s // t
    kb0 = (2 * QK_A + 2 * V_A) // LANES
    vb0 = QK_B // LANES
    return pl.pallas_call(
        functools.partial(_diff_attn_kernel, t=t, n_tiles=nq, lam_init=lam_init),
        grid=(bsz, H_B, nq),
        in_specs=[
            pl.BlockSpec((LANES, t), lambda b, h, qi: (h, b * nq + qi)),
            pl.BlockSpec((s, LANES), lambda b, h, qi: (b, kb0 + h)),
            pl.BlockSpec((DV_B, s), lambda b, h, qi: (vb0 + h, b)),
            pl.BlockSpec((5, None, t, t), lambda b, h, qi: (0, h, 0, 0)),
            pl.BlockSpec((4, DH_B), lambda b, h, qi: (0, 0)),
            pl.BlockSpec((1, DV_B), lambda b, h, qi: (0, 0)),
        ],
        out_specs=pl.BlockSpec((t, DV_B), lambda b, h, qi: (b * nq + qi, h)),
        out_shape=jax.ShapeDtypeStruct((bsz * s, V_B), BF16),
        scratch_shapes=[
            pltpu.VMEM((LANES, 2 * t), BF16),
            pltpu.VMEM((DV_B + ONES_ROWS, 2 * t), F32),
        ],
        compiler_params=_params(("parallel", "parallel", "parallel")),
        name="diff_attention",
    )(pt, p, pt, bias, diff_lambda, subln.reshape(1, DV_B))


def _rope_k_kernel(c_ref, cs_ref, o_ref):
    tmp = c_ref[...] * cs_ref[...]
    y = tmp + pltpu.roll(tmp, ROPE_C, axis=1)
    lane = lax.broadcasted_iota(jnp.int32, y.shape, 1)
    o_ref[...] = jnp.where(lane < ROPE_C, y, 0.0).astype(o_ref.dtype)


def rope_k(c, cs, bsz, s):
    tr = min(512, s)
    nb = s // tr
    cb = (Q_LORA + KV_LORA) // LANES
    return pl.pallas_call(
        _rope_k_kernel,
        grid=(bsz, nb),
        in_specs=[
            pl.BlockSpec((tr, LANES), lambda b, i: (b * nb + i, cb)),
            pl.BlockSpec((tr, LANES), lambda b, i: (i, 0)),
        ],
        out_specs=pl.BlockSpec((tr, LANES), lambda b, i: (b * nb + i, 0)),
        out_shape=jax.ShapeDtypeStruct((bsz * s, LANES), BF16),
        compiler_params=_params(("parallel", "parallel")),
        name="rope_k",
    )(c, cs)


def _mla_attn_kernel(qa_ref, qb_ref, cs_ref, kn_ref, kr_ref, vt_ref, o_ref, qs_ref, acc_ref, *, tk, n_tiles):
    scale = (NOPE_C + ROPE_C) ** -0.5 * LOG2E
    tmp = qb_ref[...].astype(F32) * cs_ref[...]
    qr = tmp + pltpu.roll(tmp, ROPE_C, axis=0)
    qs_ref[0:LANES, :] = (qa_ref[...].astype(F32) * scale).astype(BF16)
    qs_ref[LANES:2 * LANES, :] = (qr * scale).astype(BF16)

    kcs = {}

    def scores(j, cols):
        if j not in kcs:
            rows = slice(j * tk, (j + 1) * tk)
            kcs[j] = jnp.concatenate([kn_ref[rows, :], kr_ref[rows, :]], axis=-1)
        return _dot(kcs[j], qs_ref[:, cols])

    l = _online_softmax(n_tiles, scores, lambda j: vt_ref[:, j * tk:(j + 1) * tk], acc_ref, qs_ref.shape[1],
                        ATT_PART)
    o_ref[...] = (acc_ref[0:DV_C, :] / l).T.astype(o_ref.dtype)


def mla_attention(qt, kn, krp, vt, cst, bsz, s):
    tq = min(MLA_TQ, s)
    tk = min(MLA_TK, s)
    nq = s // tq
    return pl.pallas_call(
        functools.partial(_mla_attn_kernel, tk=tk, n_tiles=s // tk),
        grid=(bsz, H_C, nq),
        in_specs=[
            pl.BlockSpec((LANES, tq), lambda b, h, qi: (2 * h, b * nq + qi)),
            pl.BlockSpec((LANES, tq), lambda b, h, qi: (2 * h + 1, b * nq + qi)),
            pl.BlockSpec((LANES, tq), lambda b, h, qi: (0, qi)),
            pl.BlockSpec((s, LANES), lambda b, h, qi: (b, h)),
            pl.BlockSpec((s, LANES), lambda b, h, qi: (b, 0)),
            pl.BlockSpec((DV_C, s), lambda b, h, qi: (h, b)),
        ],
        out_specs=pl.BlockSpec((tq, DV_C), lambda b, h, qi: (b * nq + qi, h)),
        out_shape=jax.ShapeDtypeStruct((bsz * s, H_C * DV_C), BF16),
        scratch_shapes=[
            pltpu.VMEM((2 * LANES, tq), BF16),
            pltpu.VMEM((DV_C + ONES_ROWS, tq), F32),
        ],
        compiler_params=_params(("parallel", "parallel", "parallel")),
        name="mla_attention",
    )(qt, qt, cst, kn, krp, vt)


def _rotate_half_cols(w):
    half = w.shape[-1] // 2
    return jnp.concatenate([-w[..., half:], w[..., :half]], axis=-1)


def _prepare_weights(w_in_e, a_log, dt_bias, w_out_e, w_in_c, w_qb, w_out_c, w_mlp1, w_mlp2, w_kvb):
    g0 = 2 * QK_A + 2 * V_A
    prep = {}
    b0 = g0 + 4 * H_A
    t_last = lambda w: jnp.swapaxes(w, -1, -2)
    prep["w_e_main"] = jnp.concatenate([w_in_e[:, :, :g0], w_in_e[:, :, b0 + QK_B:b0 + 2 * QK_B]],
                                       axis=-1).astype(BF16)
    prep["w_e_t"] = t_last(jnp.concatenate([w_in_e[:, :, b0:b0 + QK_B], w_in_e[:, :, b0 + 2 * QK_B:]],
                                           axis=-1)).astype(BF16)
    prep["w_e_gate"] = jnp.pad(w_in_e[:, :, g0:g0 + 4 * H_A], ((0, 0), (0, 0), (0, LANES - 4 * H_A))).astype(BF16)
    n_even = a_log.shape[0]
    pad_row = lambda v: jnp.pad(v.reshape(n_even, 1, 2 * H_A).astype(F32),
                                ((0, 0), (0, 0), (2 * H_A, LANES - 4 * H_A)))
    prep["alog_row"] = pad_row(a_log)
    prep["dtb_row"] = pad_row(dt_bias)
    prep["w_out_e"] = w_out_e.astype(BF16)
    k_rope_w = w_in_c[:, :, Q_LORA + KV_LORA:]
    prep["w_c"] = jnp.concatenate([w_in_c, _rotate_half_cols(k_rope_w)], axis=-1).astype(BF16)
    n_odd = w_qb.shape[0]
    wq = w_qb.reshape(n_odd, Q_LORA, H_C, NOPE_C + ROPE_C)
    wq_rope = wq[..., NOPE_C:]
    prep["w_qb_t"] = t_last(jnp.concatenate([wq[..., :NOPE_C], wq_rope, _rotate_half_cols(wq_rope)], axis=-1).reshape(
        n_odd, Q_LORA, H_C * 2 * LANES)).astype(BF16)
    wkv = w_kvb.reshape(n_odd, KV_LORA, H_C, NOPE_C + DV_C)
    prep["w_kn"] = wkv[..., :NOPE_C].reshape(n_odd, KV_LORA, H_C * NOPE_C).astype(BF16)
    prep["w_v_t"] = t_last(wkv[..., NOPE_C:].reshape(n_odd, KV_LORA, H_C * DV_C)).astype(BF16)
    prep["w_out_c"] = w_out_c.astype(BF16)
    prep["w_mlp1"] = w_mlp1.astype(BF16)
    prep["w_mlp2"] = w_mlp2.astype(BF16)
    return prep


def _rope_table(s):
    inv_freq = ROPE_THETA ** (-jnp.arange(0, ROPE_C, 2, dtype=F32) / ROPE_C)
    ang = jnp.arange(s, dtype=F32)[:, None] * inv_freq[None, :]
    cos, sin = jnp.cos(ang), jnp.sin(ang)
    return jnp.concatenate([cos, cos, sin, sin], axis=-1)


def _trunk(x, pw, bias, norm_mix, norm_mlp, norm_final, conv_w, gdn_norm, diff_lambda, subln, q_norm, kv_norm):
    bsz, s, d = x.shape
    x = x.reshape(bsz * s, d)
    cs = _rope_table(s)
    cst = cs.T
    for layer in range(DEPTH):
        i = layer // 2
        if layer % 2 == 0:
            p, gates = norm_matmul_aux(x, norm_mix[layer], pw["w_e_main"][i], pw["w_e_gate"][i], BF16, 1024, 1024)
            pt = norm_matmul_t(x, 0, d, norm_mix[layer], pw["w_e_t"][i], BF16, 1024, 1024)
            prep = gdn_prep(p, gates, conv_w[i], pw["alog_row"][i], pw["dtb_row"][i], bsz, s)
            o_a = gdn_scan(prep, p, gdn_norm[i], bsz, s)
            lam_init = 0.8 - 0.6 * math.exp(-0.3 * layer)
            o_b = diff_attention(p, pt, bias, diff_lambda[i], subln[i], lam_init, bsz, s)
            x = matmul_res([o_a, o_b], pw["w_out_e"][i], x, 1024, 1024)
        else:
            c = norm_matmul(x, 0, d, norm_mix[layer], pw["w_c"][i], F32, 1024, pw["w_c"].shape[-1])
            qt = norm_matmul_t(c, 0, Q_LORA, q_norm[i], pw["w_qb_t"][i], BF16, 1024, 1024)
            kn = norm_matmul(c, 1, KV_LORA, kv_norm[i], pw["w_kn"][i], BF16, 1024, 1024)
            vt = norm_matmul_t(c, 1, KV_LORA, kv_norm[i], pw["w_v_t"][i], BF16, 1024, 1024)
            krp = rope_k(c, cs, bsz, s)
            o_c = mla_attention(qt, kn, krp, vt, cst, bsz, s)
            x = matmul_res([o_c], pw["w_out_c"][i], x, 1024, 1024)
        x = mlp(x, norm_mlp[layer], pw["w_mlp1"][layer], pw["w_mlp2"][layer], norm_final,
                layer == DEPTH - 1, 1024, 512)
    return x.reshape(bsz, s, d)


def kernel(x_prompt, x_sample, norm_mix, norm_mlp, norm_final, rel_bias, w_in_e, conv_w, a_log, dt_bias, gdn_norm, diff_lambda, subln, w_out_e, w_in_c, q_norm, w_qb, kv_norm, w_kvb, w_out_c, w_mlp1, w_mlp2):
    pw = _prepare_weights(w_in_e, a_log, dt_bias, w_out_e, w_in_c, w_qb, w_out_c, w_mlp1, w_mlp2, w_kvb)
    bias = bias_tiles(rel_bias)
    run = lambda x: _trunk(x, pw, bias, norm_mix, norm_mlp, norm_final, conv_w, gdn_norm, diff_lambda,
                           subln, q_norm, kv_norm)
    return (run(x_prompt), run(x_sample))
```

```python
import functools
import math

import jax
import jax.numpy as jnp
from jax import lax
from jax.experimental import pallas as pl
from jax.experimental.pallas import tpu as pltpu

F32 = jnp.float32
BF16 = jnp.bfloat16

D_MODEL = 2048
DEPTH = 4
H_A = 8
DK_A = 128
DV_A = 128
QK_A = H_A * DK_A
V_A = H_A * DV_A
H_B = 8
DH_B = 64
DV_B = 128
QK_B = H_B * 2 * DH_B
V_B = H_B * DV_B
N_BUCKETS = 32
MAX_DIST = 128
H_C = 16
Q_LORA = 512
KV_LORA = 512
NOPE_C = 128
ROPE_C = 64
DV_C = 128
ROPE_THETA = 10000.0
D_FF = 4 * D_MODEL
EPS = 1e-6

LOG2E = math.log2(math.e)
LANES = 128
F32_SUBLANES = 8
BF16_SUBLANES = 16
VMEM_LIMIT = 56 * 1024 * 1024
MLP_VMEM_LIMIT = 60 * 1024 * 1024

GDN_PREP_BLOCK = 128
GDN_SCAN_BLOCK = 256
GDN_CHUNK = 64
GDN_PREP_HEADS = 8
GDN_SCAN_HEADS = 4
DIFF_TILE = 512
BIAS_REACH = 2
N_BIAS_TILES = 2 * BIAS_REACH + 1
MLA_TQ = 2048
MLA_TK = 512
ATT_PART = 512
ONES_ROWS = BF16_SUBLANES


def _params(sem, vmem_limit=VMEM_LIMIT):
    return pltpu.CompilerParams(dimension_semantics=sem, vmem_limit_bytes=vmem_limit)


def _dot(a, b):
    return jnp.dot(a, b, preferred_element_type=F32)


def _dot_nt(a, b):
    return lax.dot_general(a, b, (((1,), (1,)), ((), ())), preferred_element_type=F32)


def _rms(x, g):
    ms = jnp.mean(x * x, axis=-1, keepdims=True)
    return x * lax.rsqrt(ms + EPS) * g


def _norm_matmul_kernel(x_ref, g_ref, w_ref, o_ref, h_ref):
    @pl.when(pl.program_id(1) == 0)
    def _():
        h_ref[...] = _rms(x_ref[...].astype(F32), g_ref[...]).astype(BF16)

    o_ref[...] = _dot(h_ref[...], w_ref[...]).astype(o_ref.dtype)


def norm_matmul(x, kblock, k, g, w, out_dtype, tm, tn):
    t = x.shape[0]
    n = w.shape[1]
    tm = min(tm, t)
    tn = min(tn, n)
    return pl.pallas_call(
        _norm_matmul_kernel,
        grid=(t // tm, n // tn),
        in_specs=[
            pl.BlockSpec((tm, k), lambda i, j: (i, kblock)),
            pl.BlockSpec((1, k), lambda i, j: (0, 0)),
            pl.BlockSpec((k, tn), lambda i, j: (0, j)),
        ],
        out_specs=pl.BlockSpec((tm, tn), lambda i, j: (i, j)),
        out_shape=jax.ShapeDtypeStruct((t, n), out_dtype),
        scratch_shapes=[pltpu.VMEM((tm, k), BF16)],
        compiler_params=_params(("parallel", "arbitrary")),
        name="norm_matmul",
    )(x, g.reshape(1, k).astype(F32), w)


def _norm_matmul_aux_kernel(x_ref, g_ref, w_ref, wa_ref, o_ref, oa_ref, h_ref):
    @pl.when(pl.program_id(1) == 0)
    def _():
        h = _rms(x_ref[...].astype(F32), g_ref[...]).astype(BF16)
        h_ref[...] = h
        oa_ref[...] = _dot(h, wa_ref[...])

    o_ref[...] = _dot(h_ref[...], w_ref[...]).astype(o_ref.dtype)


def norm_matmul_aux(x, g, w, w_aux, out_dtype, tm, tn):
    t, k = x.shape
    n = w.shape[1]
    tm = min(tm, t)
    tn = min(tn, n)
    return pl.pallas_call(
        _norm_matmul_aux_kernel,
        grid=(t // tm, n // tn),
        in_specs=[
            pl.BlockSpec((tm, k), lambda i, j: (i, 0)),
            pl.BlockSpec((1, k), lambda i, j: (0, 0)),
            pl.BlockSpec((k, tn), lambda i, j: (0, j)),
            pl.BlockSpec((k, LANES), lambda i, j: (0, 0)),
        ],
        out_specs=[pl.BlockSpec((tm, tn), lambda i, j: (i, j)),
                   pl.BlockSpec((tm, LANES), lambda i, j: (i, 0))],
        out_shape=[jax.ShapeDtypeStruct((t, n), out_dtype), jax.ShapeDtypeStruct((t, LANES), F32)],
        scratch_shapes=[pltpu.VMEM((tm, k), BF16)],
        compiler_params=_params(("parallel", "arbitrary")),
        name="norm_matmul_aux",
    )(x, g.reshape(1, k).astype(F32), w, w_aux)


def _norm_matmul_t_kernel(x_ref, g_ref, wt_ref, o_ref, h_ref):
    @pl.when(pl.program_id(1) == 0)
    def _():
        h_ref[...] = _rms(x_ref[...].astype(F32), g_ref[...]).astype(BF16)

    o_ref[...] = _dot_nt(wt_ref[...], h_ref[...]).astype(o_ref.dtype)


def norm_matmul_t(x, kblock, k, g, wt, out_dtype, tm, tn):
    t = x.shape[0]
    n = wt.shape[0]
    tm = min(tm, t)
    tn = min(tn, n)
    return pl.pallas_call(
        _norm_matmul_t_kernel,
        grid=(t // tm, n // tn),
        in_specs=[
            pl.BlockSpec((tm, k), lambda i, j: (i, kblock)),
            pl.BlockSpec((1, k), lambda i, j: (0, 0)),
            pl.BlockSpec((tn, k), lambda i, j: (j, 0)),
        ],
        out_specs=pl.BlockSpec((tn, tm), lambda i, j: (j, i)),
        out_shape=jax.ShapeDtypeStruct((n, t), out_dtype),
        scratch_shapes=[pltpu.VMEM((tm, k), BF16)],
        compiler_params=_params(("parallel", "arbitrary")),
        name="norm_matmul_t",
    )(x, g.reshape(1, k).astype(F32), wt)


def _matmul_res_kernel(*refs, widths):
    n = len(widths)
    w_ref, r_ref, o_ref = refs[n:]
    acc = r_ref[...]
    off = 0
    for a_ref, wd in zip(refs[:n], widths):
        acc = acc + _dot(a_ref[...], w_ref[off:off + wd, :])
        off += wd
    o_ref[...] = acc


def matmul_res(parts, w, res, tm, tn):
    t = res.shape[0]
    n = w.shape[1]
    tm = min(tm, t)
    widths = tuple(p.shape[1] for p in parts)
    return pl.pallas_call(
        functools.partial(_matmul_res_kernel, widths=widths),
        grid=(t // tm, n // tn),
        in_specs=[pl.BlockSpec((tm, wd), lambda i, j: (i, 0)) for wd in widths]
        + [
            pl.BlockSpec((sum(widths), tn), lambda i, j: (0, j)),
            pl.BlockSpec((tm, tn), lambda i, j: (i, j)),
        ],
        out_specs=pl.BlockSpec((tm, tn), lambda i, j: (i, j)),
        out_shape=jax.ShapeDtypeStruct((t, n), F32),
        compiler_params=_params(("parallel", "parallel")),
        name="matmul_res",
    )(*parts, w, res)


def _mlp_kernel(x_ref, g_ref, w1_ref, w2_ref, gf_ref, o_ref, h_ref, *, nf, final_norm):
    f = pl.program_id(1)

    @pl.when(f == 0)
    def _():
        x = x_ref[...]
        h_ref[...] = _rms(x, g_ref[...]).astype(BF16)
        o_ref[...] = x

    a = _dot(h_ref[...], w1_ref[...])
    a = jnp.square(jnp.maximum(a, 0.0)).astype(BF16)
    o_ref[...] += _dot(a, w2_ref[...])

    if final_norm:
        @pl.when(f == nf - 1)
        def _():
            o_ref[...] = _rms(o_ref[...], gf_ref[...])


def mlp(x, g, w1, w2, g_final, final_norm, tm, tf):
    t, d = x.shape
    tm = min(tm, t)
    nf = D_FF // tf
    return pl.pallas_call(
        functools.partial(_mlp_kernel, nf=nf, final_norm=final_norm),
        grid=(t // tm, nf),
        in_specs=[
            pl.BlockSpec((tm, d), lambda i, f: (i, 0)),
            pl.BlockSpec((1, d), lambda i, f: (0, 0)),
            pl.BlockSpec((d, tf), lambda i, f: (0, f)),
            pl.BlockSpec((tf, d), lambda i, f: (f, 0)),
            pl.BlockSpec((1, d), lambda i, f: (0, 0)),
        ],
        out_specs=pl.BlockSpec((tm, d), lambda i, f: (i, 0)),
        out_shape=jax.ShapeDtypeStruct((t, d), F32),
        scratch_shapes=[pltpu.VMEM((tm, d), BF16)],
        compiler_params=_params(("parallel", "arbitrary"), MLP_VMEM_LIMIT),
        name="mlp",
    )(x, g.reshape(1, d), w1, w2, g_final.reshape(1, d))


def _softplus(x):
    return jnp.maximum(x, 0.0) + jnp.log1p(jnp.exp(-jnp.abs(x)))


def _gdn_prep_kernel(q_ref, qp_ref, qn_ref, k_ref, kp_ref, kn_ref, v_ref, vp_ref, vn_ref,
                     cwq_ref, cwk_ref, cwv_ref, gt_ref, alog_ref, dtb_ref,
                     qd_ref, kd_ref, w_ref, u_ref, in_ref, egl_ref, *, tb, cc, n):
    hp = pl.program_id(1)
    i = pl.program_id(2)
    width = GDN_PREP_HEADS * LANES
    roww = lax.broadcasted_iota(jnp.int32, (tb, width), 0)

    def conv_silu(x_ref, p_ref, n_ref, cw_ref):
        x = x_ref[...].astype(F32)
        prev = jnp.where(i == 0, 0.0, p_ref[BF16_SUBLANES - 1:BF16_SUBLANES, :].astype(F32))
        nxt = jnp.where(i == n - 1, 0.0, n_ref[0:1, :].astype(F32))
        xm = jnp.where(roww == 0, prev, pltpu.roll(x, 1, axis=0))
        xp = jnp.where(roww == tb - 1, nxt, pltpu.roll(x, tb - 1, axis=0))
        cw = cw_ref[...]
        y = xm * cw[0:1, :] + x * cw[1:2, :] + xp * cw[2:3, :]
        return y * jax.nn.sigmoid(y)

    q_all = conv_silu(q_ref, qp_ref, qn_ref, cwq_ref)
    k_all = conv_silu(k_ref, kp_ref, kn_ref, cwk_ref)
    v_all = conv_silu(v_ref, vp_ref, vn_ref, cwv_ref)

    row = lax.broadcasted_iota(jnp.int32, (tb, LANES), 0)
    lane = lax.broadcasted_iota(jnp.int32, (tb, LANES), 1)
    rin = row & (cc - 1)
    gt = gt_ref[...]
    beta_all = jax.nn.sigmoid(gt)
    g_all = -jnp.exp(alog_ref[...]) * _softplus(gt + dtb_ref[...])
    pre = g_all
    suf = g_all
    s = 1
    while s < cc:
        pre = pre + jnp.where(rin >= s, pltpu.roll(pre, s, axis=0), 0.0)
        suf = suf + jnp.where(rin < cc - s, pltpu.roll(suf, tb - s, axis=0), 0.0)
        s *= 2
    tot = pre + suf - g_all
    pre_t = pre.T
    suf_t = suf.T
    sub_t = lax.broadcasted_iota(jnp.int32, (LANES, tb), 0)

    def col(x, j):
        return jnp.broadcast_to(jnp.sum(jnp.where(lane == j, x, 0.0), axis=1, keepdims=True), (tb, LANES))

    def wide(c):
        return jnp.concatenate([c] * (tb // LANES), axis=1)

    def rowv(xt, j):
        return jnp.sum(jnp.where(sub_t == j, xt, 0.0), axis=0, keepdims=True)

    lgc = cc.bit_length() - 1
    ii = lax.broadcasted_iota(jnp.int32, (tb, tb), 0)
    jj = lax.broadcasted_iota(jnp.int32, (tb, tb), 1)
    same = (ii >> lgc) == (jj >> lgc)
    incl = (same & (ii >= jj), same & (ii <= jj))
    offdiag = ii != jj
    eye = (ii == jj).astype(F32)
    pair = (ii >> 1) == (jj >> 1)
    levels = [((ii >> (lg + 1)) == (jj >> (lg + 1))) & ((ii >> lg) != (jj >> lg)) for lg in range(1, lgc)]

    chains = []
    for e in range(GDN_PREP_HEADS):
        h = hp * GDN_PREP_HEADS + e
        cols = slice(e * LANES, (e + 1) * LANES)
        q, k, v = q_all[:, cols], k_all[:, cols], v_all[:, cols]
        sumsq = lambda t: jnp.broadcast_to(jnp.sum(t * t, axis=-1, keepdims=True), (tb, LANES))
        q = q * (lax.rsqrt(sumsq(q) + EPS) * (DK_A ** -0.5))
        k = k * lax.rsqrt(sumsq(k) + EPS)
        k16 = k.astype(BF16)
        qk = _dot_nt(q.astype(BF16), k16)
        kk = _dot_nt(k16, k16)
        for d in range(2):
            jb = d * H_A + h
            jg = 2 * H_A + d * H_A + h
            gcol, grow = (col(pre, jg), rowv(pre_t, jg)) if d == 0 else (col(suf, jg), rowv(suf_t, jg))
            beta = col(beta_all, jb)
            tcol = col(tot, jg)
            dec = jnp.where(incl[d], jnp.exp(jnp.where(incl[d], wide(gcol) - grow, 0.0)), 0.0)
            eg = jnp.exp(gcol)
            a = jnp.where(offdiag, kk * dec, 0.0) * wide(beta)
            m = qk * dec
            m = sum(m[:, j * LANES:(j + 1) * LANES] for j in range(tb // LANES))
            in_ref[d, e] = m.astype(BF16)
            qd_ref[d, e] = (q * eg).astype(BF16)
            kd_ref[d, e] = (k * jnp.exp(tcol - gcol)).astype(BF16)
            eb = jnp.exp(tcol)
            for c in range(tb // cc):
                egl_ref[d, e, c] = jnp.broadcast_to(eb[c * cc:c * cc + 1, :], (F32_SUBLANES, LANES))
            rhs = jnp.concatenate([v * beta, k * (beta * eg)], axis=1).astype(BF16)
            chains.append((d, e, a, rhs))

    nas = [-a for (_, _, a, _) in chains]
    xs = [(eye + jnp.where(pair, na, 0.0)).astype(BF16) for na in nas]
    for off in levels:
        ys = [_dot(jnp.where(off, na, 0.0).astype(BF16), x) for na, x in zip(nas, xs)]
        xs = [jnp.where(off, _dot(x, y.astype(BF16)).astype(BF16), x) for x, y in zip(xs, ys)]
    for (d, e, _, rhs), x in zip(chains, xs):
        uw = _dot(x, rhs)
        u_ref[d, e] = uw[:, 0:DV_A]
        w_ref[d, e] = uw[:, DV_A:].astype(BF16)


def gdn_prep(p, gates, conv_w, alog_row, dtb_row, bsz, s):
    tb, cc, hps = GDN_PREP_BLOCK, GDN_CHUNK, GDN_PREP_HEADS
    assert tb % LANES == 0 and cc == LANES // 2
    n = s // tb
    t = bsz * s
    width = hps * LANES
    rpb = tb // BF16_SUBLANES
    last_blk = t // BF16_SUBLANES - 1

    def main(cb):
        return pl.BlockSpec((tb, width), lambda b, hp, i: (b * n + i, cb // hps + hp))

    def prev(cb):
        return pl.BlockSpec((BF16_SUBLANES, width),
                            lambda b, hp, i: (jnp.maximum((b * n + i) * rpb - 1, 0), cb // hps + hp))

    def nxt(cb):
        return pl.BlockSpec((BF16_SUBLANES, width),
                            lambda b, hp, i: (jnp.minimum((b * n + i + 1) * rpb, last_blk), cb // hps + hp))

    def cw(cb):
        return pl.BlockSpec((3, width), lambda b, hp, i: (0, cb // hps + hp))

    row_spec = pl.BlockSpec((1, LANES), lambda b, hp, i: (0, 0))
    seq_out = lambda last: pl.BlockSpec((2, None, hps, tb, last), lambda b, hp, i: (0, b, hp, i, 0))
    out_shape = [
        jax.ShapeDtypeStruct((2, bsz, H_A, s, DK_A), BF16),
        jax.ShapeDtypeStruct((2, bsz, H_A, s, DK_A), BF16),
        jax.ShapeDtypeStruct((2, bsz, H_A, s, DK_A), BF16),
        jax.ShapeDtypeStruct((2, bsz, H_A, s, DV_A), F32),
        jax.ShapeDtypeStruct((2, bsz, H_A, s, LANES), BF16),
        jax.ShapeDtypeStruct((2, bsz, H_A, s // cc, F32_SUBLANES, LANES), F32),
    ]
    out_specs = [
        seq_out(DK_A), seq_out(DK_A), seq_out(DK_A), seq_out(DV_A), seq_out(LANES),
        pl.BlockSpec((2, None, hps, tb // cc, F32_SUBLANES, LANES), lambda b, hp, i: (0, b, hp, i, 0, 0)),
    ]
    kb0, vb0 = QK_A // LANES, 2 * QK_A // LANES
    return pl.pallas_call(
        functools.partial(_gdn_prep_kernel, tb=tb, cc=cc, n=n),
        grid=(bsz, H_A // hps, n),
        in_specs=[main(0), prev(0), nxt(0), main(kb0), prev(kb0), nxt(kb0), main(vb0), prev(vb0), nxt(vb0),
                  cw(0), cw(kb0), cw(vb0),
                  pl.BlockSpec((tb, LANES), lambda b, hp, i: (b * n + i, 0)),
                  row_spec, row_spec],
        out_specs=out_specs,
        out_shape=out_shape,
        compiler_params=_params(("parallel", "parallel", "parallel")),
        name="gdn_prep",
    )(p, p, p, p, p, p, p, p, p, conv_w, conv_w, conv_w, gates, alog_row, dtb_row)


def _gdn_scan_kernel(qdf, kdf, wf, uf, inf, eglf, qdb, kdb, wb, ub, inb, eglb, z_ref, gn_ref,
                     o_ref, st_ref, acc_ref, *, tb, cc, n, hps):
    i = pl.program_id(2)
    nc = tb // cc

    @pl.when(i == 0)
    def _():
        st_ref[...] = jnp.zeros_like(st_ref)
        acc_ref[...] = jnp.zeros_like(acc_ref)

    fwd = (qdf, kdf, wf, uf, inf, eglf)
    bwd = (qdb, kdb, wb, ub, inb, eglb)
    chains = [(d, e) for e in range(hps) for d in range(2)]
    st = {ch: st_ref[ch[0], ch[1]] for ch in chains}
    for step in range(nc):
        stage = {}
        for d, e in chains:
            qd, kd, w, u, intra, egl = fwd if d == 0 else bwd
            c = step if d == 0 else nc - 1 - step
            rows = slice(c * cc, (c + 1) * cc)
            st16 = st[d, e].astype(BF16)
            stage[d, e] = (c, rows, _dot(w[e, rows, :], st16), _dot(qd[e, rows, :], st16))
        for d, e in chains:
            qd, kd, w, u, intra, egl = fwd if d == 0 else bwd
            c, rows, ws, qs = stage[d, e]
            v16 = (u[e, rows, :] - ws).astype(BF16)
            o = qs + _dot(intra[e, rows, :], jnp.concatenate([v16] * (LANES // cc), axis=0))
            kv = lax.dot_general(kd[e, rows, :], v16, (((0,), (0,)), ((), ())), preferred_element_type=F32)
            st[d, e] = st[d, e] * egl[e, c, 0:1, :] + kv
            row0 = i * tb if d == 0 else (n - 1 - i) * tb
            r0 = pl.multiple_of(row0 + c * cc, cc)
            acc_ref[pl.ds(r0, cc), e * DV_A:(e + 1) * DV_A] += o
    for d, e in chains:
        st_ref[d, e] = st[d, e]

    @pl.when(i == n - 1)
    def _():
        for e in range(hps):
            cols = slice(e * DV_A, (e + 1) * DV_A)
            z = z_ref[:, cols].astype(F32)
            y = _rms(acc_ref[:, cols], gn_ref[...]) * (z * jax.nn.sigmoid(z))
            o_ref[:, cols] = y.astype(o_ref.dtype)


def gdn_scan(prep, p, gdn_norm, bsz, s):
    qd, kd, w, u, intra, egl = prep
    tb, cc, hps = GDN_SCAN_BLOCK, GDN_CHUNK, GDN_SCAN_HEADS
    n = s // tb

    def seq(d, last):
        if d == 0:
            return pl.BlockSpec((None, None, hps, tb, last), lambda b, hp, i: (0, b, hp, i, 0))
        return pl.BlockSpec((None, None, hps, tb, last), lambda b, hp, i: (1, b, hp, n - 1 - i, 0))

    def egl_spec(d):
        blk = (None, None, hps, tb // cc, F32_SUBLANES, LANES)
        if d == 0:
            return pl.BlockSpec(blk, lambda b, hp, i: (0, b, hp, i, 0, 0))
        return pl.BlockSpec(blk, lambda b, hp, i: (1, b, hp, n - 1 - i, 0, 0))

    def dir_specs(d):
        return [seq(d, DK_A), seq(d, DK_A), seq(d, DK_A), seq(d, DV_A), seq(d, LANES), egl_spec(d)]

    zb0 = (2 * QK_A + V_A) // (hps * LANES)
    return pl.pallas_call(
        functools.partial(_gdn_scan_kernel, tb=tb, cc=cc, n=n, hps=hps),
        grid=(bsz, H_A // hps, n),
        in_specs=dir_specs(0) + dir_specs(1) + [
            pl.BlockSpec((s, hps * LANES), lambda b, hp, i: (b, zb0 + hp)),
            pl.BlockSpec((1, DV_A), lambda b, hp, i: (0, 0)),
        ],
        out_specs=pl.BlockSpec((s, hps * DV_A), lambda b, hp, i: (b, hp)),
        out_shape=jax.ShapeDtypeStruct((bsz * s, V_A), BF16),
        scratch_shapes=[pltpu.VMEM((2, hps, DK_A, DV_A), F32), pltpu.VMEM((s, hps * DV_A), F32)],
        compiler_params=_params(("parallel", "parallel", "arbitrary")),
        name="gdn_scan",
    )(qd, kd, w, u, intra, egl, qd, kd, w, u, intra, egl, p, gdn_norm.reshape(1, DV_A))


def _bucket_thresholds():
    nb = N_BUCKETS // 2
    max_exact = nb // 2
    out = []
    prev = 0
    for dist in range(max_exact, MAX_DIST):
        val = int(math.log(dist / max_exact) / math.log(MAX_DIST / max_exact) * (nb - max_exact))
        val = min(val, nb - 1 - max_exact)
        if val > prev:
            out.extend([dist] * (val - prev))
            prev = val
    return tuple(out)


def _bias_kernel(rb_ref, o_ref, *, t, thresholds):
    d = pl.program_id(0) - BIAS_REACH
    h = pl.program_id(1)
    nb = N_BUCKETS // 2
    max_exact = nb // 2

    @pl.when(jnp.abs(d) >= BIAS_REACH)
    def _():
        far = jnp.where(d > 0, rb_ref[N_BUCKETS - 1, h], rb_ref[nb - 1, h])
        o_ref[...] = jnp.full((t, t), far * LOG2E, F32)

    @pl.when(jnp.abs(d) < BIAS_REACH)
    def _():
        r = lax.broadcasted_iota(jnp.int32, (t, t), 0)
        c = lax.broadcasted_iota(jnp.int32, (t, t), 1)
        rel = d * t + r - c
        dist = jnp.abs(rel)
        large = jnp.full((t, t), max_exact, jnp.int32)
        for th in thresholds:
            large = large + (dist >= th).astype(jnp.int32)
        bucket = jnp.where(rel > 0, nb, 0) + jnp.where(dist < max_exact, dist, large)
        out = jnp.zeros((t, t), F32)
        for b in range(N_BUCKETS):
            out = jnp.where(bucket == b, rb_ref[b, h], out)
        o_ref[...] = out * LOG2E


def bias_tiles(rel_bias):
    t = DIFF_TILE
    return pl.pallas_call(
        functools.partial(_bias_kernel, t=t, thresholds=_bucket_thresholds()),
        grid=(N_BIAS_TILES, H_B),
        in_specs=[pl.BlockSpec(memory_space=pltpu.SMEM)],
        out_specs=pl.BlockSpec((None, None, t, t), lambda d, h: (d, h, 0, 0)),
        out_shape=jax.ShapeDtypeStruct((N_BIAS_TILES, H_B, t, t), F32),
        compiler_params=_params(("parallel", "parallel")),
        name="t5_bias_tiles",
    )(rel_bias)


def _online_softmax(n_tiles, scores, vt, acc_ref, nq, part, shift=None):
    part = min(part, nq)
    parts = [slice(c, c + part) for c in range(0, nq, part)]
    m = [None] * len(parts)
    dv = acc_ref.shape[0] - ONES_ROWS
    sts = [scores(0, cols) for cols in parts]
    for j in range(n_tiles):
        nxt = [scores(j + 1, cols) for cols in parts] if j + 1 < n_tiles else None
        c = shift(j) if shift is not None else None
        pts = []
        for i, st in enumerate(sts):
            mx = jnp.max(st, axis=0, keepdims=True)
            if c is not None:
                mx = mx + c
            if j == 0:
                m[i], alpha = mx, None
            else:
                m_new = jnp.maximum(m[i], mx)
                alpha = jnp.exp2(m[i] - m_new)
                m[i] = m_new
            ref = m[i] if c is None else m[i] - c
            pts.append((alpha, jnp.exp2(st - ref).astype(BF16)))
        v = vt(j)
        v1 = jnp.concatenate([v, jnp.ones((ONES_ROWS, v.shape[1]), BF16)], axis=0)
        for cols, (alpha, pt16) in zip(parts, pts):
            pv = _dot(v1, pt16)
            acc_ref[:, cols] = pv if j == 0 else alpha * acc_ref[:, cols] + pv
        sts = nxt
    return acc_ref[dv:dv + 1, :]


def _diff_attn_kernel(q_ref, k_ref, vt_ref, b_ref, dl_ref, sg_ref, o_ref, qs_ref, acc_ref, *, t, n_tiles, lam_init):
    qi = pl.program_id(2)
    q = q_ref[...].astype(F32) * (DH_B ** -0.5 * LOG2E)
    sub = lax.broadcasted_iota(jnp.int32, (LANES, t), 0)
    qs_ref[:, 0:t] = jnp.where(sub < DH_B, q, 0.0).astype(BF16)
    qs_ref[:, t:2 * t] = jnp.where(sub >= DH_B, q, 0.0).astype(BF16)

    n_near = min(2 * BIAS_REACH - 1, n_tiles)

    def tile_of(r):
        return lax.rem(qi + (n_tiles - 1 + r), n_tiles)

    def start(r):
        return pl.multiple_of(tile_of(r) * t, t)

    def scores(r, cols):
        st = _dot(k_ref[pl.ds(start(r), t), :], qs_ref[:, cols])
        if r >= n_near:
            return st
        b0 = cols.start % t
        width = cols.stop - cols.start
        tile = jnp.clip(tile_of(r) - qi, -BIAS_REACH, BIAS_REACH) + BIAS_REACH
        return st + b_ref[tile, :, b0:b0 + width]

    def shift(r):
        if r < n_near:
            return None
        return jnp.where(tile_of(r) > qi, b_ref[N_BIAS_TILES - 1, 0:1, 0:1], b_ref[0, 0:1, 0:1])

    l = _online_softmax(n_tiles, scores, lambda r: vt_ref[:, pl.ds(start(r), t)], acc_ref, 2 * t, min(ATT_PART, t), shift)

    dl = dl_ref[...]
    lam = (jnp.exp(jnp.sum(dl[0:1, :] * dl[1:2, :], axis=-1, keepdims=True))
           - jnp.exp(jnp.sum(dl[2:3, :] * dl[3:4, :], axis=-1, keepdims=True)) + lam_init)
    a = acc_ref[0:DV_B, :] / l
    o = (a[:, 0:t] - lam * a[:, t:2 * t]).T
    o_ref[...] = (_rms(o, sg_ref[...]) * (1.0 - lam_init)).astype(o_ref.dtype)


def diff_attention(p, pt, bias, diff_lambda, subln, lam_init, bsz, s):
    t = min(DIFF_TILE, s)
    assert t >= MAX_DIST
    nq = s // t
    kb0 = (2 * QK_A + 2 * V_A) // LANES
    vb0 = QK_B // LANES
    return pl.pallas_call(
        functools.partial(_diff_attn_kernel, t=t, n_tiles=nq, lam_init=lam_init),
        grid=(bsz, H_B, nq),
        in_specs=[
            pl.BlockSpec((LANES, t), lambda b, h, qi: (h, b * nq + qi)),
            pl.BlockSpec((s, LANES), lambda b, h, qi: (b, kb0 + h)),
            pl.BlockSpec((DV_B, s), lambda b, h, qi: (vb0 + h, b)),
            pl.BlockSpec((N_BIAS_TILES, None, t, t), lambda b, h, qi: (0, h, 0, 0)),
            pl.BlockSpec((4, DH_B), lambda b, h, qi: (0, 0)),
            pl.BlockSpec((1, DV_B), lambda b, h, qi: (0, 0)),
        ],
        out_specs=pl.BlockSpec((t, DV_B), lambda b, h, qi: (b * nq + qi, h)),
        out_shape=jax.ShapeDtypeStruct((bsz * s, V_B), BF16),
        scratch_shapes=[
            pltpu.VMEM((LANES, 2 * t), BF16),
            pltpu.VMEM((DV_B + ONES_ROWS, 2 * t), F32),
        ],
        compiler_params=_params(("parallel", "parallel", "parallel")),
        name="diff_attention",
    )(pt, p, pt, bias, diff_lambda, subln.reshape(1, DV_B))


def _rope_k_kernel(c_ref, cs_ref, o_ref):
    tmp = c_ref[...] * cs_ref[...]
    y = tmp + pltpu.roll(tmp, ROPE_C, axis=1)
    lane = lax.broadcasted_iota(jnp.int32, y.shape, 1)
    o_ref[...] = jnp.where(lane < ROPE_C, y, 0.0).astype(o_ref.dtype)


def rope_k(c, cs, bsz, s):
    tr = min(512, s)
    nb = s // tr
    cb = (Q_LORA + KV_LORA) // LANES
    return pl.pallas_call(
        _rope_k_kernel,
        grid=(bsz, nb),
        in_specs=[
            pl.BlockSpec((tr, LANES), lambda b, i: (b * nb + i, cb)),
            pl.BlockSpec((tr, LANES), lambda b, i: (i, 0)),
        ],
        out_specs=pl.BlockSpec((tr, LANES), lambda b, i: (b * nb + i, 0)),
        out_shape=jax.ShapeDtypeStruct((bsz * s, LANES), BF16),
        compiler_params=_params(("parallel", "parallel")),
        name="rope_k",
    )(c, cs)


def _mla_attn_kernel(qa_ref, qb_ref, cs_ref, kn_ref, kr_ref, vt_ref, o_ref, qs_ref, acc_ref, *, tk, n_tiles):
    scale = (NOPE_C + ROPE_C) ** -0.5 * LOG2E
    tmp = qb_ref[...].astype(F32) * cs_ref[...]
    qr = tmp + pltpu.roll(tmp, ROPE_C, axis=0)
    qs_ref[0:LANES, :] = (qa_ref[...].astype(F32) * scale).astype(BF16)
    qs_ref[LANES:2 * LANES, :] = (qr * scale).astype(BF16)

    kcs = {}

    def scores(j, cols):
        if j not in kcs:
            rows = slice(j * tk, (j + 1) * tk)
            kcs[j] = jnp.concatenate([kn_ref[rows, :], kr_ref[rows, :]], axis=-1)
        return _dot(kcs[j], qs_ref[:, cols])

    l = _online_softmax(n_tiles, scores, lambda j: vt_ref[:, j * tk:(j + 1) * tk], acc_ref, qs_ref.shape[1],
                        ATT_PART)
    o_ref[...] = (acc_ref[0:DV_C, :] / l).T.astype(o_ref.dtype)


def mla_attention(qt, kn, krp, vt, cst, bsz, s):
    tq = min(MLA_TQ, s)
    tk = min(MLA_TK, s)
    nq = s // tq
    return pl.pallas_call(
        functools.partial(_mla_attn_kernel, tk=tk, n_tiles=s // tk),
        grid=(bsz, H_C, nq),
        in_specs=[
            pl.BlockSpec((LANES, tq), lambda b, h, qi: (2 * h, b * nq + qi)),
            pl.BlockSpec((LANES, tq), lambda b, h, qi: (2 * h + 1, b * nq + qi)),
            pl.BlockSpec((LANES, tq), lambda b, h, qi: (0, qi)),
            pl.BlockSpec((s, LANES), lambda b, h, qi: (b, h)),
            pl.BlockSpec((s, LANES), lambda b, h, qi: (b, 0)),
            pl.BlockSpec((DV_C, s), lambda b, h, qi: (h, b)),
        ],
        out_specs=pl.BlockSpec((tq, DV_C), lambda b, h, qi: (b * nq + qi, h)),
        out_shape=jax.ShapeDtypeStruct((bsz * s, H_C * DV_C), BF16),
        scratch_shapes=[
            pltpu.VMEM((2 * LANES, tq), BF16),
            pltpu.VMEM((DV_C + ONES_ROWS, tq), F32),
        ],
        compiler_params=_params(("parallel", "parallel", "parallel")),
        name="mla_attention",
    )(qt, qt, cst, kn, krp, vt)


def _rotate_half_cols(w):
    half = w.shape[-1] // 2
    return jnp.concatenate([-w[..., half:], w[..., :half]], axis=-1)


def _prepare_weights(w_in_e, a_log, dt_bias, w_out_e, w_in_c, w_qb, w_out_c, w_mlp1, w_mlp2, w_kvb):
    g0 = 2 * QK_A + 2 * V_A
    prep = {}
    b0 = g0 + 4 * H_A
    t_last = lambda w: jnp.swapaxes(w, -1, -2)
    prep["w_e_main"] = jnp.concatenate([w_in_e[:, :, :g0], w_in_e[:, :, b0 + QK_B:b0 + 2 * QK_B]],
                                       axis=-1).astype(BF16)
    prep["w_e_t"] = t_last(jnp.concatenate([w_in_e[:, :, b0:b0 + QK_B], w_in_e[:, :, b0 + 2 * QK_B:]],
                                           axis=-1)).astype(BF16)
    prep["w_e_gate"] = jnp.pad(w_in_e[:, :, g0:g0 + 4 * H_A], ((0, 0), (0, 0), (0, LANES - 4 * H_A))).astype(BF16)
    n_even = a_log.shape[0]
    pad_row = lambda v: jnp.pad(v.reshape(n_even, 1, 2 * H_A).astype(F32),
                                ((0, 0), (0, 0), (2 * H_A, LANES - 4 * H_A)))
    prep["alog_row"] = pad_row(a_log)
    prep["dtb_row"] = pad_row(dt_bias)
    prep["w_out_e"] = w_out_e.astype(BF16)
    k_rope_w = w_in_c[:, :, Q_LORA + KV_LORA:]
    prep["w_c"] = jnp.concatenate([w_in_c, _rotate_half_cols(k_rope_w)], axis=-1).astype(BF16)
    n_odd = w_qb.shape[0]
    wq = w_qb.reshape(n_odd, Q_LORA, H_C, NOPE_C + ROPE_C)
    wq_rope = wq[..., NOPE_C:]
    prep["w_qb_t"] = t_last(jnp.concatenate([wq[..., :NOPE_C], wq_rope, _rotate_half_cols(wq_rope)], axis=-1).reshape(
        n_odd, Q_LORA, H_C * 2 * LANES)).astype(BF16)
    wkv = w_kvb.reshape(n_odd, KV_LORA, H_C, NOPE_C + DV_C)
    prep["w_kn"] = wkv[..., :NOPE_C].reshape(n_odd, KV_LORA, H_C * NOPE_C).astype(BF16)
    prep["w_v_t"] = t_last(wkv[..., NOPE_C:].reshape(n_odd, KV_LORA, H_C * DV_C)).astype(BF16)
    prep["w_out_c"] = w_out_c.astype(BF16)
    prep["w_mlp1"] = w_mlp1.astype(BF16)
    prep["w_mlp2"] = w_mlp2.astype(BF16)
    return prep


def _rope_table(s):
    inv_freq = ROPE_THETA ** (-jnp.arange(0, ROPE_C, 2, dtype=F32) / ROPE_C)
    ang = jnp.arange(s, dtype=F32)[:, None] * inv_freq[None, :]
    cos, sin = jnp.cos(ang), jnp.sin(ang)
    return jnp.concatenate([cos, cos, sin, sin], axis=-1)


def _trunk(x, pw, bias, norm_mix, norm_mlp, norm_final, conv_w, gdn_norm, diff_lambda, subln, q_norm, kv_norm):
    bsz, s, d = x.shape
    x = x.reshape(bsz * s, d)
    cs = _rope_table(s)
    cst = cs.T
    for layer in range(DEPTH):
        i = layer // 2
        if layer % 2 == 0:
            p, gates = norm_matmul_aux(x, norm_mix[layer], pw["w_e_main"][i], pw["w_e_gate"][i], BF16, 1024, 1024)
            pt = norm_matmul_t(x, 0, d, norm_mix[layer], pw["w_e_t"][i], BF16, 1024, 1024)
            prep = gdn_prep(p, gates, conv_w[i], pw["alog_row"][i], pw["dtb_row"][i], bsz, s)
            o_a = gdn_scan(prep, p, gdn_norm[i], bsz, s)
            lam_init = 0.8 - 0.6 * math.exp(-0.3 * layer)
            o_b = diff_attention(p, pt, bias, diff_lambda[i], subln[i], lam_init, bsz, s)
            x = matmul_res([o_a, o_b], pw["w_out_e"][i], x, 1024, 1024)
        else:
            c = norm_matmul(x, 0, d, norm_mix[layer], pw["w_c"][i], F32, 1024, pw["w_c"].shape[-1])
            qt = norm_matmul_t(c, 0, Q_LORA, q_norm[i], pw["w_qb_t"][i], BF16, 1024, 1024)
            kn = norm_matmul(c, 1, KV_LORA, kv_norm[i], pw["w_kn"][i], BF16, 1024, 1024)
            vt = norm_matmul_t(c, 1, KV_LORA, kv_norm[i], pw["w_v_t"][i], BF16, 1024, 1024)
            krp = rope_k(c, cs, bsz, s)
            o_c = mla_attention(qt, kn, krp, vt, cst, bsz, s)
            x = matmul_res([o_c], pw["w_out_c"][i], x, 1024, 1024)
        x = mlp(x, norm_mlp[layer], pw["w_mlp1"][layer], pw["w_mlp2"][layer], norm_final,
                layer == DEPTH - 1, 1024, 1024)
    return x.reshape(bsz, s, d)


def kernel(x_prompt, x_sample, norm_mix, norm_mlp, norm_final, rel_bias, w_in_e, conv_w, a_log, dt_bias, gdn_norm, diff_lambda, subln, w_out_e, w_in_c, q_norm, w_qb, kv_norm, w_kvb, w_out_c, w_mlp1, w_mlp2):
    pw = _prepare_weights(w_in_e, a_log, dt_bias, w_out_e, w_in_c, w_qb, w_out_c, w_mlp1, w_mlp2, w_kvb)
    bias = bias_tiles(rel_bias)
    run = lambda x: _trunk(x, pw, bias, norm_mix, norm_mlp, norm_final, conv_w, gdn_norm, diff_lambda,
                           subln, q_norm, kv_norm)
    return (run(x_prompt), run(x_sample))
```

```python
import functools
import math

import jax
import jax.numpy as jnp
from jax import lax
from jax.experimental import pallas as pl
from jax.experimental.pallas import tpu as pltpu

F32 = jnp.float32
BF16 = jnp.bfloat16

D_MODEL = 2048
DEPTH = 4
H_A = 8
DK_A = 128
DV_A = 128
QK_A = H_A * DK_A
V_A = H_A * DV_A
H_B = 8
DH_B = 64
DV_B = 128
QK_B = H_B * 2 * DH_B
V_B = H_B * DV_B
N_BUCKETS = 32
MAX_DIST = 128
H_C = 16
Q_LORA = 512
KV_LORA = 512
NOPE_C = 128
ROPE_C = 64
DV_C = 128
ROPE_THETA = 10000.0
D_FF = 4 * D_MODEL
EPS = 1e-6

LOG2E = math.log2(math.e)
LANES = 128
F32_SUBLANES = 8
BF16_SUBLANES = 16
VMEM_LIMIT = 56 * 1024 * 1024
MLP_VMEM_LIMIT = 60 * 1024 * 1024

GDN_PREP_BLOCK = 128
GDN_SCAN_BLOCK = 256
GDN_CHUNK = 64
GDN_PREP_HEADS = 8
GDN_SCAN_HEADS = 4
DIFF_TILE = 512
BIAS_REACH = 2
N_BIAS_TILES = 2 * BIAS_REACH + 1
MLA_TQ = 2048
MLA_TK = 512
ATT_PART = 512
ONES_ROWS = BF16_SUBLANES


def _params(sem, vmem_limit=VMEM_LIMIT):
    return pltpu.CompilerParams(dimension_semantics=sem, vmem_limit_bytes=vmem_limit)


def _dot(a, b):
    return jnp.dot(a, b, preferred_element_type=F32)


def _dot_nt(a, b):
    return lax.dot_general(a, b, (((1,), (1,)), ((), ())), preferred_element_type=F32)


def _rms(x, g):
    ms = jnp.mean(x * x, axis=-1, keepdims=True)
    return x * lax.rsqrt(ms + EPS) * g


def _norm_matmul_kernel(x_ref, g_ref, w_ref, o_ref, h_ref):
    @pl.when(pl.program_id(1) == 0)
    def _():
        h_ref[...] = _rms(x_ref[...].astype(F32), g_ref[...]).astype(BF16)

    o_ref[...] = _dot(h_ref[...], w_ref[...]).astype(o_ref.dtype)


def norm_matmul(x, kblock, k, g, w, out_dtype, tm, tn):
    t = x.shape[0]
    n = w.shape[1]
    tm = min(tm, t)
    tn = min(tn, n)
    return pl.pallas_call(
        _norm_matmul_kernel,
        grid=(t // tm, n // tn),
        in_specs=[
            pl.BlockSpec((tm, k), lambda i, j: (i, kblock)),
            pl.BlockSpec((1, k), lambda i, j: (0, 0)),
            pl.BlockSpec((k, tn), lambda i, j: (0, j)),
        ],
        out_specs=pl.BlockSpec((tm, tn), lambda i, j: (i, j)),
        out_shape=jax.ShapeDtypeStruct((t, n), out_dtype),
        scratch_shapes=[pltpu.VMEM((tm, k), BF16)],
        compiler_params=_params(("parallel", "arbitrary")),
        name="norm_matmul",
    )(x, g.reshape(1, k).astype(F32), w)


def _norm_matmul_aux_kernel(x_ref, g_ref, w_ref, wa_ref, o_ref, oa_ref, h_ref):
    @pl.when(pl.program_id(1) == 0)
    def _():
        h = _rms(x_ref[...].astype(F32), g_ref[...]).astype(BF16)
        h_ref[...] = h
        oa_ref[...] = _dot(h, wa_ref[...])

    o_ref[...] = _dot(h_ref[...], w_ref[...]).astype(o_ref.dtype)


def norm_matmul_aux(x, g, w, w_aux, out_dtype, tm, tn):
    t, k = x.shape
    n = w.shape[1]
    tm = min(tm, t)
    tn = min(tn, n)
    return pl.pallas_call(
        _norm_matmul_aux_kernel,
        grid=(t // tm, n // tn),
        in_specs=[
            pl.BlockSpec((tm, k), lambda i, j: (i, 0)),
            pl.BlockSpec((1, k), lambda i, j: (0, 0)),
            pl.BlockSpec((k, tn), lambda i, j: (0, j)),
            pl.BlockSpec((k, LANES), lambda i, j: (0, 0)),
        ],
        out_specs=[pl.BlockSpec((tm, tn), lambda i, j: (i, j)),
                   pl.BlockSpec((tm, LANES), lambda i, j: (i, 0))],
        out_shape=[jax.ShapeDtypeStruct((t, n), out_dtype), jax.ShapeDtypeStruct((t, LANES), F32)],
        scratch_shapes=[pltpu.VMEM((tm, k), BF16)],
        compiler_params=_params(("parallel", "arbitrary")),
        name="norm_matmul_aux",
    )(x, g.reshape(1, k).astype(F32), w, w_aux)


def _norm_matmul_t_kernel(x_ref, g_ref, wt_ref, o_ref, h_ref):
    @pl.when(pl.program_id(1) == 0)
    def _():
        h_ref[...] = _rms(x_ref[...].astype(F32), g_ref[...]).astype(BF16)

    o_ref[...] = _dot_nt(wt_ref[...], h_ref[...]).astype(o_ref.dtype)


def norm_matmul_t(x, kblock, k, g, wt, out_dtype, tm, tn):
    t = x.shape[0]
    n = wt.shape[0]
    tm = min(tm, t)
    tn = min(tn, n)
    return pl.pallas_call(
        _norm_matmul_t_kernel,
        grid=(t // tm, n // tn),
        in_specs=[
            pl.BlockSpec((tm, k), lambda i, j: (i, kblock)),
            pl.BlockSpec((1, k), lambda i, j: (0, 0)),
            pl.BlockSpec((tn, k), lambda i, j: (j, 0)),
        ],
        out_specs=pl.BlockSpec((tn, tm), lambda i, j: (j, i)),
        out_shape=jax.ShapeDtypeStruct((n, t), out_dtype),
        scratch_shapes=[pltpu.VMEM((tm, k), BF16)],
        compiler_params=_params(("parallel", "arbitrary")),
        name="norm_matmul_t",
    )(x, g.reshape(1, k).astype(F32), wt)


def _matmul_res_kernel(*refs, widths):
    n = len(widths)
    w_ref, r_ref, o_ref = refs[n:]
    acc = r_ref[...]
    off = 0
    for a_ref, wd in zip(refs[:n], widths):
        acc = acc + _dot(a_ref[...], w_ref[off:off + wd, :])
        off += wd
    o_ref[...] = acc


def matmul_res(parts, w, res, tm, tn):
    t = res.shape[0]
    n = w.shape[1]
    tm = min(tm, t)
    widths = tuple(p.shape[1] for p in parts)
    return pl.pallas_call(
        functools.partial(_matmul_res_kernel, widths=widths),
        grid=(t // tm, n // tn),
        in_specs=[pl.BlockSpec((tm, wd), lambda i, j: (i, 0)) for wd in widths]
        + [
            pl.BlockSpec((sum(widths), tn), lambda i, j: (0, j)),
            pl.BlockSpec((tm, tn), lambda i, j: (i, j)),
        ],
        out_specs=pl.BlockSpec((tm, tn), lambda i, j: (i, j)),
        out_shape=jax.ShapeDtypeStruct((t, n), F32),
        compiler_params=_params(("parallel", "parallel")),
        name="matmul_res",
    )(*parts, w, res)


def _mlp_kernel(x_ref, g_ref, w1_ref, w2_ref, gf_ref, o_ref, h_ref, *, nf, final_norm):
    f = pl.program_id(1)

    @pl.when(f == 0)
    def _():
        x = x_ref[...]
        h_ref[...] = _rms(x, g_ref[...]).astype(BF16)
        o_ref[...] = x

    a = _dot(h_ref[...], w1_ref[...])
    a = jnp.square(jnp.maximum(a, 0.0)).astype(BF16)
    o_ref[...] += _dot(a, w2_ref[...])

    if final_norm:
        @pl.when(f == nf - 1)
        def _():
            o_ref[...] = _rms(o_ref[...], gf_ref[...])


def mlp(x, g, w1, w2, g_final, final_norm, tm, tf):
    t, d = x.shape
    tm = min(tm, t)
    nf = D_FF // tf
    return pl.pallas_call(
        functools.partial(_mlp_kernel, nf=nf, final_norm=final_norm),
        grid=(t // tm, nf),
        in_specs=[
            pl.BlockSpec((tm, d), lambda i, f: (i, 0)),
            pl.BlockSpec((1, d), lambda i, f: (0, 0)),
            pl.BlockSpec((d, tf), lambda i, f: (0, f)),
            pl.BlockSpec((tf, d), lambda i, f: (f, 0)),
            pl.BlockSpec((1, d), lambda i, f: (0, 0)),
        ],
        out_specs=pl.BlockSpec((tm, d), lambda i, f: (i, 0)),
        out_shape=jax.ShapeDtypeStruct((t, d), F32),
        scratch_shapes=[pltpu.VMEM((tm, d), BF16)],
        compiler_params=_params(("parallel", "arbitrary"), MLP_VMEM_LIMIT),
        name="mlp",
    )(x, g.reshape(1, d), w1, w2, g_final.reshape(1, d))


def _softplus(x):
    return jnp.maximum(x, 0.0) + jnp.log1p(jnp.exp(-jnp.abs(x)))


def _gdn_prep_kernel(q_ref, qp_ref, qn_ref, k_ref, kp_ref, kn_ref, v_ref, vp_ref, vn_ref,
                     cwq_ref, cwk_ref, cwv_ref, gt_ref, alog_ref, dtb_ref,
                     qd_ref, kd_ref, w_ref, u_ref, in_ref, egl_ref, *, tb, cc, n):
    hp = pl.program_id(1)
    i = pl.program_id(2)
    width = GDN_PREP_HEADS * LANES
    roww = lax.broadcasted_iota(jnp.int32, (tb, width), 0)

    def conv_silu(x_ref, p_ref, n_ref, cw_ref):
        x = x_ref[...].astype(F32)
        prev = jnp.where(i == 0, 0.0, p_ref[BF16_SUBLANES - 1:BF16_SUBLANES, :].astype(F32))
        nxt = jnp.where(i == n - 1, 0.0, n_ref[0:1, :].astype(F32))
        xm = jnp.where(roww == 0, prev, pltpu.roll(x, 1, axis=0))
        xp = jnp.where(roww == tb - 1, nxt, pltpu.roll(x, tb - 1, axis=0))
        cw = cw_ref[...]
        y = xm * cw[0:1, :] + x * cw[1:2, :] + xp * cw[2:3, :]
        return y * jax.nn.sigmoid(y)

    q_all = conv_silu(q_ref, qp_ref, qn_ref, cwq_ref)
    k_all = conv_silu(k_ref, kp_ref, kn_ref, cwk_ref)
    v_all = conv_silu(v_ref, vp_ref, vn_ref, cwv_ref)

    row = lax.broadcasted_iota(jnp.int32, (tb, LANES), 0)
    lane = lax.broadcasted_iota(jnp.int32, (tb, LANES), 1)
    rin = row & (cc - 1)
    gt = gt_ref[...]
    beta_all = jax.nn.sigmoid(gt)
    g_all = -jnp.exp(alog_ref[...]) * _softplus(gt + dtb_ref[...])
    pre = g_all
    suf = g_all
    s = 1
    while s < cc:
        pre = pre + jnp.where(rin >= s, pltpu.roll(pre, s, axis=0), 0.0)
        suf = suf + jnp.where(rin < cc - s, pltpu.roll(suf, tb - s, axis=0), 0.0)
        s *= 2
    tot = pre + suf - g_all
    pre_t = pre.T
    suf_t = suf.T
    sub_t = lax.broadcasted_iota(jnp.int32, (LANES, tb), 0)

    def col(x, j):
        return jnp.broadcast_to(jnp.sum(jnp.where(lane == j, x, 0.0), axis=1, keepdims=True), (tb, LANES))

    def wide(c):
        return jnp.concatenate([c] * (tb // LANES), axis=1)

    def rowv(xt, j):
        return jnp.sum(jnp.where(sub_t == j, xt, 0.0), axis=0, keepdims=True)

    lgc = cc.bit_length() - 1
    ii = lax.broadcasted_iota(jnp.int32, (tb, tb), 0)
    jj = lax.broadcasted_iota(jnp.int32, (tb, tb), 1)
    same = (ii >> lgc) == (jj >> lgc)
    incl = (same & (ii >= jj), same & (ii <= jj))
    offdiag = ii != jj
    eye = (ii == jj).astype(F32)
    pair = (ii >> 1) == (jj >> 1)
    levels = [((ii >> (lg + 1)) == (jj >> (lg + 1))) & ((ii >> lg) != (jj >> lg)) for lg in range(1, lgc)]

    chains = []
    for e in range(GDN_PREP_HEADS):
        h = hp * GDN_PREP_HEADS + e
        cols = slice(e * LANES, (e + 1) * LANES)
        q, k, v = q_all[:, cols], k_all[:, cols], v_all[:, cols]
        sumsq = lambda t: jnp.broadcast_to(jnp.sum(t * t, axis=-1, keepdims=True), (tb, LANES))
        q = q * (lax.rsqrt(sumsq(q) + EPS) * (DK_A ** -0.5))
        k = k * lax.rsqrt(sumsq(k) + EPS)
        k16 = k.astype(BF16)
        qk = _dot_nt(q.astype(BF16), k16)
        kk = _dot_nt(k16, k16)
        for d in range(2):
            jb = d * H_A + h
            jg = 2 * H_A + d * H_A + h
            gcol, grow = (col(pre, jg), rowv(pre_t, jg)) if d == 0 else (col(suf, jg), rowv(suf_t, jg))
            beta = col(beta_all, jb)
            tcol = col(tot, jg)
            dec = jnp.where(incl[d], jnp.exp(jnp.where(incl[d], wide(gcol) - grow, 0.0)), 0.0)
            eg = jnp.exp(gcol)
            a = jnp.where(offdiag, kk * dec, 0.0) * wide(beta)
            m = qk * dec
            m = sum(m[:, j * LANES:(j + 1) * LANES] for j in range(tb // LANES))
            in_ref[d, e] = m.astype(BF16)
            qd_ref[d, e] = (q * eg).astype(BF16)
            kd_ref[d, e] = (k * jnp.exp(tcol - gcol)).astype(BF16)
            eb = jnp.exp(tcol)
            for c in range(tb // cc):
                egl_ref[d, e, c] = jnp.broadcast_to(eb[c * cc:c * cc + 1, :], (F32_SUBLANES, LANES))
            rhs = jnp.concatenate([v * beta, k * (beta * eg)], axis=1).astype(BF16)
            chains.append((d, e, a, rhs))

    nas = [-a for (_, _, a, _) in chains]
    xs = [(eye + jnp.where(pair, na, 0.0)).astype(BF16) for na in nas]
    for off in levels:
        ys = [_dot(jnp.where(off, na, 0.0).astype(BF16), x) for na, x in zip(nas, xs)]
        xs = [jnp.where(off, _dot(x, y.astype(BF16)).astype(BF16), x) for x, y in zip(xs, ys)]
    for (d, e, _, rhs), x in zip(chains, xs):
        uw = _dot(x, rhs)
        u_ref[d, e] = uw[:, 0:DV_A]
        w_ref[d, e] = uw[:, DV_A:].astype(BF16)


def gdn_prep(p, gates, conv_w, alog_row, dtb_row, bsz, s):
    tb, cc, hps = GDN_PREP_BLOCK, GDN_CHUNK, GDN_PREP_HEADS
    assert tb % LANES == 0 and cc == LANES // 2
    n = s // tb
    t = bsz * s
    width = hps * LANES
    rpb = tb // BF16_SUBLANES
    last_blk = t // BF16_SUBLANES - 1

    def main(cb):
        return pl.BlockSpec((tb, width), lambda b, hp, i: (b * n + i, cb // hps + hp))

    def prev(cb):
        return pl.BlockSpec((BF16_SUBLANES, width),
                            lambda b, hp, i: (jnp.maximum((b * n + i) * rpb - 1, 0), cb // hps + hp))

    def nxt(cb):
        return pl.BlockSpec((BF16_SUBLANES, width),
                            lambda b, hp, i: (jnp.minimum((b * n + i + 1) * rpb, last_blk), cb // hps + hp))

    def cw(cb):
        return pl.BlockSpec((3, width), lambda b, hp, i: (0, cb // hps + hp))

    row_spec = pl.BlockSpec((1, LANES), lambda b, hp, i: (0, 0))
    seq_out = lambda last: pl.BlockSpec((2, None, hps, tb, last), lambda b, hp, i: (0, b, hp, i, 0))
    out_shape = [
        jax.ShapeDtypeStruct((2, bsz, H_A, s, DK_A), BF16),
        jax.ShapeDtypeStruct((2, bsz, H_A, s, DK_A), BF16),
        jax.ShapeDtypeStruct((2, bsz, H_A, s, DK_A), BF16),
        jax.ShapeDtypeStruct((2, bsz, H_A, s, DV_A), F32),
        jax.ShapeDtypeStruct((2, bsz, H_A, s, LANES), BF16),
        jax.ShapeDtypeStruct((2, bsz, H_A, s // cc, F32_SUBLANES, LANES), F32),
    ]
    out_specs = [
        seq_out(DK_A), seq_out(DK_A), seq_out(DK_A), seq_out(DV_A), seq_out(LANES),
        pl.BlockSpec((2, None, hps, tb // cc, F32_SUBLANES, LANES), lambda b, hp, i: (0, b, hp, i, 0, 0)),
    ]
    kb0, vb0 = QK_A // LANES, 2 * QK_A // LANES
    return pl.pallas_call(
        functools.partial(_gdn_prep_kernel, tb=tb, cc=cc, n=n),
        grid=(bsz, H_A // hps, n),
        in_specs=[main(0), prev(0), nxt(0), main(kb0), prev(kb0), nxt(kb0), main(vb0), prev(vb0), nxt(vb0),
                  cw(0), cw(kb0), cw(vb0),
                  pl.BlockSpec((tb, LANES), lambda b, hp, i: (b * n + i, 0)),
                  row_spec, row_spec],
        out_specs=out_specs,
        out_shape=out_shape,
        compiler_params=_params(("parallel", "parallel", "parallel")),
        name="gdn_prep",
    )(p, p, p, p, p, p, p, p, p, conv_w, conv_w, conv_w, gates, alog_row, dtb_row)


def _gdn_scan_kernel(qdf, kdf, wf, uf, inf, eglf, qdb, kdb, wb, ub, inb, eglb, z_ref, gn_ref,
                     o_ref, st_ref, acc_ref, *, tb, cc, n, hps):
    i = pl.program_id(2)
    nc = tb // cc

    @pl.when(i == 0)
    def _():
        st_ref[...] = jnp.zeros_like(st_ref)
        acc_ref[...] = jnp.zeros_like(acc_ref)

    fwd = (qdf, kdf, wf, uf, inf, eglf)
    bwd = (qdb, kdb, wb, ub, inb, eglb)
    chains = [(d, e) for e in range(hps) for d in range(2)]
    st = {ch: st_ref[ch[0], ch[1]] for ch in chains}
    for step in range(nc):
        stage = {}
        for d, e in chains:
            qd, kd, w, u, intra, egl = fwd if d == 0 else bwd
            c = step if d == 0 else nc - 1 - step
            rows = slice(c * cc, (c + 1) * cc)
            st16 = st[d, e].astype(BF16)
            stage[d, e] = (c, rows, _dot(w[e, rows, :], st16), _dot(qd[e, rows, :], st16))
        for d, e in chains:
            qd, kd, w, u, intra, egl = fwd if d == 0 else bwd
            c, rows, ws, qs = stage[d, e]
            v16 = (u[e, rows, :] - ws).astype(BF16)
            o = qs + _dot(intra[e, rows, :], jnp.concatenate([v16] * (LANES // cc), axis=0))
            kv = lax.dot_general(kd[e, rows, :], v16, (((0,), (0,)), ((), ())), preferred_element_type=F32)
            st[d, e] = st[d, e] * egl[e, c, 0:1, :] + kv
            row0 = i * tb if d == 0 else (n - 1 - i) * tb
            r0 = pl.multiple_of(row0 + c * cc, cc)
            acc_ref[pl.ds(r0, cc), e * DV_A:(e + 1) * DV_A] += o
    for d, e in chains:
        st_ref[d, e] = st[d, e]

    @pl.when(i == n - 1)
    def _():
        for e in range(hps):
            cols = slice(e * DV_A, (e + 1) * DV_A)
            z = z_ref[:, cols].astype(F32)
            y = _rms(acc_ref[:, cols], gn_ref[...]) * (z * jax.nn.sigmoid(z))
            o_ref[:, cols] = y.astype(o_ref.dtype)


def gdn_scan(prep, p, gdn_norm, bsz, s):
    qd, kd, w, u, intra, egl = prep
    tb, cc, hps = GDN_SCAN_BLOCK, GDN_CHUNK, GDN_SCAN_HEADS
    n = s // tb

    def seq(d, last):
        if d == 0:
            return pl.BlockSpec((None, None, hps, tb, last), lambda b, hp, i: (0, b, hp, i, 0))
        return pl.BlockSpec((None, None, hps, tb, last), lambda b, hp, i: (1, b, hp, n - 1 - i, 0))

    def egl_spec(d):
        blk = (None, None, hps, tb // cc, F32_SUBLANES, LANES)
        if d == 0:
            return pl.BlockSpec(blk, lambda b, hp, i: (0, b, hp, i, 0, 0))
        return pl.BlockSpec(blk, lambda b, hp, i: (1, b, hp, n - 1 - i, 0, 0))

    def dir_specs(d):
        return [seq(d, DK_A), seq(d, DK_A), seq(d, DK_A), seq(d, DV_A), seq(d, LANES), egl_spec(d)]

    zb0 = (2 * QK_A + V_A) // (hps * LANES)
    return pl.pallas_call(
        functools.partial(_gdn_scan_kernel, tb=tb, cc=cc, n=n, hps=hps),
        grid=(bsz, H_A // hps, n),
        in_specs=dir_specs(0) + dir_specs(1) + [
            pl.BlockSpec((s, hps * LANES), lambda b, hp, i: (b, zb0 + hp)),
            pl.BlockSpec((1, DV_A), lambda b, hp, i: (0, 0)),
        ],
        out_specs=pl.BlockSpec((s, hps * DV_A), lambda b, hp, i: (b, hp)),
        out_shape=jax.ShapeDtypeStruct((bsz * s, V_A), BF16),
        scratch_shapes=[pltpu.VMEM((2, hps, DK_A, DV_A), F32), pltpu.VMEM((s, hps * DV_A), F32)],
        compiler_params=_params(("parallel", "parallel", "arbitrary")),
        name="gdn_scan",
    )(qd, kd, w, u, intra, egl, qd, kd, w, u, intra, egl, p, gdn_norm.reshape(1, DV_A))


def _bucket_thresholds():
    nb = N_BUCKETS // 2
    max_exact = nb // 2
    out = []
    prev = 0
    for dist in range(max_exact, MAX_DIST):
        val = int(math.log(dist / max_exact) / math.log(MAX_DIST / max_exact) * (nb - max_exact))
        val = min(val, nb - 1 - max_exact)
        if val > prev:
            out.extend([dist] * (val - prev))
            prev = val
    return tuple(out)


def _bias_kernel(rb_ref, o_ref, *, t, thresholds):
    d = pl.program_id(0) - BIAS_REACH
    h = pl.program_id(1)
    nb = N_BUCKETS // 2
    max_exact = nb // 2

    @pl.when(jnp.abs(d) >= BIAS_REACH)
    def _():
        far = jnp.where(d > 0, rb_ref[N_BUCKETS - 1, h], rb_ref[nb - 1, h])
        o_ref[...] = jnp.full((t, t), far * LOG2E, F32)

    @pl.when(jnp.abs(d) < BIAS_REACH)
    def _():
        r = lax.broadcasted_iota(jnp.int32, (t, t), 0)
        c = lax.broadcasted_iota(jnp.int32, (t, t), 1)
        rel = d * t + r - c
        dist = jnp.abs(rel)
        large = jnp.full((t, t), max_exact, jnp.int32)
        for th in thresholds:
            large = large + (dist >= th).astype(jnp.int32)
        bucket = jnp.where(rel > 0, nb, 0) + jnp.where(dist < max_exact, dist, large)
        out = jnp.zeros((t, t), F32)
        for b in range(N_BUCKETS):
            out = jnp.where(bucket == b, rb_ref[b, h], out)
        o_ref[...] = out * LOG2E


def bias_tiles(rel_bias):
    t = DIFF_TILE
    return pl.pallas_call(
        functools.partial(_bias_kernel, t=t, thresholds=_bucket_thresholds()),
        grid=(N_BIAS_TILES, H_B),
        in_specs=[pl.BlockSpec(memory_space=pltpu.SMEM)],
        out_specs=pl.BlockSpec((None, None, t, t), lambda d, h: (d, h, 0, 0)),
        out_shape=jax.ShapeDtypeStruct((N_BIAS_TILES, H_B, t, t), F32),
        compiler_params=_params(("parallel", "parallel")),
        name="t5_bias_tiles",
    )(rel_bias)


def _online_softmax(n_tiles, scores, vt, acc_ref, nq, part, shift=None):
    part = min(part, nq)
    parts = [slice(c, c + part) for c in range(0, nq, part)]
    m = [None] * len(parts)
    dv = acc_ref.shape[0] - ONES_ROWS
    sts = [scores(0, cols) for cols in parts]
    for j in range(n_tiles):
        nxt = [scores(j + 1, cols) for cols in parts] if j + 1 < n_tiles else None
        c = shift(j) if shift is not None else None
        pts = []
        for i, st in enumerate(sts):
            mx = jnp.max(st, axis=0, keepdims=True)
            if c is not None:
                mx = mx + c
            if j == 0:
                m[i], alpha = mx, None
            else:
                m_new = jnp.maximum(m[i], mx)
                alpha = jnp.exp2(m[i] - m_new)
                m[i] = m_new
            ref = m[i] if c is None else m[i] - c
            pts.append((alpha, jnp.exp2(st - ref).astype(BF16)))
        v = vt(j)
        v1 = jnp.concatenate([v, jnp.ones((ONES_ROWS, v.shape[1]), BF16)], axis=0)
        for cols, (alpha, pt16) in zip(parts, pts):
            pv = _dot(v1, pt16)
            acc_ref[:, cols] = pv if j == 0 else alpha * acc_ref[:, cols] + pv
        sts = nxt
    return acc_ref[dv:dv + 1, :]


def _diff_attn_kernel(q_ref, k_ref, vt_ref, b_ref, dl_ref, sg_ref, o_ref, qs_ref, acc_ref, *, t, n_tiles, lam_init):
    qi = pl.program_id(2)
    q = q_ref[...].astype(F32) * (DH_B ** -0.5 * LOG2E)
    sub = lax.broadcasted_iota(jnp.int32, (LANES, t), 0)
    qs_ref[:, 0:t] = jnp.where(sub < DH_B, q, 0.0).astype(BF16)
    qs_ref[:, t:2 * t] = jnp.where(sub >= DH_B, q, 0.0).astype(BF16)

    n_near = min(2 * BIAS_REACH - 1, n_tiles)

    def tile_of(r):
        return lax.rem(qi + (n_tiles - 1 + r), n_tiles)

    def start(r):
        return pl.multiple_of(tile_of(r) * t, t)

    def scores(r, cols):
        st = _dot(k_ref[pl.ds(start(r), t), :], qs_ref[:, cols])
        if r >= n_near:
            return st
        b0 = cols.start % t
        width = cols.stop - cols.start
        tile = jnp.clip(tile_of(r) - qi, -BIAS_REACH, BIAS_REACH) + BIAS_REACH
        return st + b_ref[tile, :, b0:b0 + width]

    def shift(r):
        if r < n_near:
            return None
        return jnp.where(tile_of(r) > qi, b_ref[N_BIAS_TILES - 1, 0:1, 0:1], b_ref[0, 0:1, 0:1])

    l = _online_softmax(n_tiles, scores, lambda r: vt_ref[:, pl.ds(start(r), t)], acc_ref, 2 * t, min(ATT_PART, t), shift)

    dl = dl_ref[...]
    lam = (jnp.exp(jnp.sum(dl[0:1, :] * dl[1:2, :], axis=-1, keepdims=True))
           - jnp.exp(jnp.sum(dl[2:3, :] * dl[3:4, :], axis=-1, keepdims=True)) + lam_init)
    a = acc_ref[0:DV_B, :] / l
    o = (a[:, 0:t] - lam * a[:, t:2 * t]).T
    o_ref[...] = (_rms(o, sg_ref[...]) * (1.0 - lam_init)).astype(o_ref.dtype)


def diff_attention(p, pt, bias, diff_lambda, subln, lam_init, bsz, s):
    t = min(DIFF_TILE, s)
    assert t >= MAX_DIST
    nq = s // t
    kb0 = (2 * QK_A + 2 * V_A) // LANES
    vb0 = QK_B // LANES
    return pl.pallas_call(
        functools.partial(_diff_attn_kernel, t=t, n_tiles=nq, lam_init=lam_init),
        grid=(bsz, H_B, nq),
        in_specs=[
            pl.BlockSpec((LANES, t), lambda b, h, qi: (h, b * nq + qi)),
            pl.BlockSpec((s, LANES), lambda b, h, qi: (b, kb0 + h)),
            pl.BlockSpec((DV_B, s), lambda b, h, qi: (vb0 + h, b)),
            pl.BlockSpec((N_BIAS_TILES, None, t, t), lambda b, h, qi: (0, h, 0, 0)),
            pl.BlockSpec((4, DH_B), lambda b, h, qi: (0, 0)),
            pl.BlockSpec((1, DV_B), lambda b, h, qi: (0, 0)),
        ],
        out_specs=pl.BlockSpec((t, DV_B), lambda b, h, qi: (b * nq + qi, h)),
        out_shape=jax.ShapeDtypeStruct((bsz * s, V_B), BF16),
        scratch_shapes=[
            pltpu.VMEM((LANES, 2 * t), BF16),
            pltpu.VMEM((DV_B + ONES_ROWS, 2 * t), F32),
        ],
        compiler_params=_params(("parallel", "parallel", "parallel")),
        name="diff_attention",
    )(pt, p, pt, bias, diff_lambda, subln.reshape(1, DV_B))


def _rope_k_kernel(c_ref, cs_ref, o_ref):
    tmp = c_ref[...] * cs_ref[...]
    y = tmp + pltpu.roll(tmp, ROPE_C, axis=1)
    lane = lax.broadcasted_iota(jnp.int32, y.shape, 1)
    o_ref[...] = jnp.where(lane < ROPE_C, y, 0.0).astype(o_ref.dtype)


def rope_k(c, cs, bsz, s):
    tr = min(512, s)
    nb = s // tr
    cb = (Q_LORA + KV_LORA) // LANES
    return pl.pallas_call(
        _rope_k_kernel,
        grid=(bsz, nb),
        in_specs=[
            pl.BlockSpec((tr, LANES), lambda b, i: (b * nb + i, cb)),
            pl.BlockSpec((tr, LANES), lambda b, i: (i, 0)),
        ],
        out_specs=pl.BlockSpec((tr, LANES), lambda b, i: (b * nb + i, 0)),
        out_shape=jax.ShapeDtypeStruct((bsz * s, LANES), BF16),
        compiler_params=_params(("parallel", "parallel")),
        name="rope_k",
    )(c, cs)


def _mla_attn_kernel(qa_ref, qb_ref, cs_ref, kn_ref, kr_ref, vt_ref, o_ref, qs_ref, acc_ref, *, tk, n_tiles):
    scale = (NOPE_C + ROPE_C) ** -0.5 * LOG2E
    tmp = qb_ref[...].astype(F32) * cs_ref[...]
    qr = tmp + pltpu.roll(tmp, ROPE_C, axis=0)
    qs_ref[0:LANES, :] = (qa_ref[...].astype(F32) * scale).astype(BF16)
    qs_ref[LANES:2 * LANES, :] = (qr * scale).astype(BF16)

    kcs = {}

    def scores(j, cols):
        if j not in kcs:
            rows = slice(j * tk, (j + 1) * tk)
            kcs[j] = jnp.concatenate([kn_ref[rows, :], kr_ref[rows, :]], axis=-1)
        return _dot(kcs[j], qs_ref[:, cols])

    l = _online_softmax(n_tiles, scores, lambda j: vt_ref[:, j * tk:(j + 1) * tk], acc_ref, qs_ref.shape[1],
                        ATT_PART)
    o_ref[...] = (acc_ref[0:DV_C, :] / l).T.astype(o_ref.dtype)


def mla_attention(qt, kn, krp, vt, cst, bsz, s):
    tq = min(MLA_TQ, s)
    tk = min(MLA_TK, s)
    nq = s // tq
    return pl.pallas_call(
        functools.partial(_mla_attn_kernel, tk=tk, n_tiles=s // tk),
        grid=(bsz, H_C, nq),
        in_specs=[
            pl.BlockSpec((LANES, tq), lambda b, h, qi: (2 * h, b * nq + qi)),
            pl.BlockSpec((LANES, tq), lambda b, h, qi: (2 * h + 1, b * nq + qi)),
            pl.BlockSpec((LANES, tq), lambda b, h, qi: (0, qi)),
            pl.BlockSpec((s, LANES), lambda b, h, qi: (b, h)),
            pl.BlockSpec((s, LANES), lambda b, h, qi: (b, 0)),
            pl.BlockSpec((DV_C, s), lambda b, h, qi: (h, b)),
        ],
        out_specs=pl.BlockSpec((tq, DV_C), lambda b, h, qi: (b * nq + qi, h)),
        out_shape=jax.ShapeDtypeStruct((bsz * s, H_C * DV_C), BF16),
        scratch_shapes=[
            pltpu.VMEM((2 * LANES, tq), BF16),
            pltpu.VMEM((DV_C + ONES_ROWS, tq), F32),
        ],
        compiler_params=_params(("parallel", "parallel", "parallel")),
        name="mla_attention",
    )(qt, qt, cst, kn, krp, vt)


def _rotate_half_cols(w):
    half = w.shape[-1] // 2
    return jnp.concatenate([-w[..., half:], w[..., :half]], axis=-1)


def _prepare_weights(w_in_e, a_log, dt_bias, w_out_e, w_in_c, w_qb, w_out_c, w_mlp1, w_mlp2, w_kvb):
    g0 = 2 * QK_A + 2 * V_A
    prep = {}
    b0 = g0 + 4 * H_A
    t_last = lambda w: jnp.swapaxes(w, -1, -2)
    prep["w_e_main"] = jnp.concatenate([w_in_e[:, :, :g0], w_in_e[:, :, b0 + QK_B:b0 + 2 * QK_B]],
                                       axis=-1).astype(BF16)
    prep["w_e_t"] = t_last(jnp.concatenate([w_in_e[:, :, b0:b0 + QK_B], w_in_e[:, :, b0 + 2 * QK_B:]],
                                           axis=-1)).astype(BF16)
    prep["w_e_gate"] = jnp.pad(w_in_e[:, :, g0:g0 + 4 * H_A], ((0, 0), (0, 0), (0, LANES - 4 * H_A))).astype(BF16)
    n_even = a_log.shape[0]
    pad_row = lambda v: jnp.pad(v.reshape(n_even, 1, 2 * H_A).astype(F32),
                                ((0, 0), (0, 0), (2 * H_A, LANES - 4 * H_A)))
    prep["alog_row"] = pad_row(a_log)
    prep["dtb_row"] = pad_row(dt_bias)
    prep["w_out_e"] = w_out_e.astype(BF16)
    k_rope_w = w_in_c[:, :, Q_LORA + KV_LORA:]
    prep["w_c"] = jnp.concatenate([w_in_c, _rotate_half_cols(k_rope_w)], axis=-1).astype(BF16)
    n_odd = w_qb.shape[0]
    wq = w_qb.reshape(n_odd, Q_LORA, H_C, NOPE_C + ROPE_C)
    wq_rope = wq[..., NOPE_C:]
    prep["w_qb_t"] = t_last(jnp.concatenate([wq[..., :NOPE_C], wq_rope, _rotate_half_cols(wq_rope)], axis=-1).reshape(
        n_odd, Q_LORA, H_C * 2 * LANES)).astype(BF16)
    wkv = w_kvb.reshape(n_odd, KV_LORA, H_C, NOPE_C + DV_C)
    prep["w_kn"] = wkv[..., :NOPE_C].reshape(n_odd, KV_LORA, H_C * NOPE_C).astype(BF16)
    prep["w_v_t"] = t_last(wkv[..., NOPE_C:].reshape(n_odd, KV_LORA, H_C * DV_C)).astype(BF16)
    prep["w_out_c"] = w_out_c.astype(BF16)
    prep["w_mlp1"] = w_mlp1.astype(BF16)
    prep["w_mlp2"] = w_mlp2.astype(BF16)
    return prep


def _rope_table(s):
    inv_freq = ROPE_THETA ** (-jnp.arange(0, ROPE_C, 2, dtype=F32) / ROPE_C)
    ang = jnp.arange(s, dtype=F32)[:, None] * inv_freq[None, :]
    cos, sin = jnp.cos(ang), jnp.sin(ang)
    return jnp.concatenate([cos, cos, sin, sin], axis=-1)


def _trunk(x, pw, bias, norm_mix, norm_mlp, norm_final, conv_w, gdn_norm, diff_lambda, subln, q_norm, kv_norm):
    bsz, s, d = x.shape
    x = x.reshape(bsz * s, d)
    cs = _rope_table(s)
    cst = cs.T
    for layer in range(DEPTH):
        i = layer // 2
        if layer % 2 == 0:
            p, gates = norm_matmul_aux(x, norm_mix[layer], pw["w_e_main"][i], pw["w_e_gate"][i], BF16, 1024, 1024)
            pt = norm_matmul_t(x, 0, d, norm_mix[layer], pw["w_e_t"][i], BF16, 1024, 2048)
            prep = gdn_prep(p, gates, conv_w[i], pw["alog_row"][i], pw["dtb_row"][i], bsz, s)
            o_a = gdn_scan(prep, p, gdn_norm[i], bsz, s)
            lam_init = 0.8 - 0.6 * math.exp(-0.3 * layer)
            o_b = diff_attention(p, pt, bias, diff_lambda[i], subln[i], lam_init, bsz, s)
            x = matmul_res([o_a, o_b], pw["w_out_e"][i], x, 512, 2048)
        else:
            c = norm_matmul(x, 0, d, norm_mix[layer], pw["w_c"][i], F32, 1024, pw["w_c"].shape[-1])
            qt = norm_matmul_t(c, 0, Q_LORA, q_norm[i], pw["w_qb_t"][i], BF16, 1024, 4096)
            kn = norm_matmul(c, 1, KV_LORA, kv_norm[i], pw["w_kn"][i], BF16, 1024, 2048)
            vt = norm_matmul_t(c, 1, KV_LORA, kv_norm[i], pw["w_v_t"][i], BF16, 1024, 2048)
            krp = rope_k(c, cs, bsz, s)
            o_c = mla_attention(qt, kn, krp, vt, cst, bsz, s)
            x = matmul_res([o_c], pw["w_out_c"][i], x, 512, 2048)
        x = mlp(x, norm_mlp[layer], pw["w_mlp1"][layer], pw["w_mlp2"][layer], norm_final,
                layer == DEPTH - 1, 1024, 1024)
    return x.reshape(bsz, s, d)


def kernel(x_prompt, x_sample, norm_mix, norm_mlp, norm_final, rel_bias, w_in_e, conv_w, a_log, dt_bias, gdn_norm, diff_lambda, subln, w_out_e, w_in_c, q_norm, w_qb, kv_norm, w_kvb, w_out_c, w_mlp1, w_mlp2):
    pw = _prepare_weights(w_in_e, a_log, dt_bias, w_out_e, w_in_c, w_qb, w_out_c, w_mlp1, w_mlp2, w_kvb)
    bias = bias_tiles(rel_bias)
    run = lambda x: _trunk(x, pw, bias, norm_mix, norm_mlp, norm_final, conv_w, gdn_norm, diff_lambda,
                           subln, q_norm, kv_norm)
    return (run(x_prompt), run(x_sample))
```
